```python
import jax, jax.numpy as jnp
from jax import lax
import numpy as np

D_MODEL = 1024
BATCH = 8
SEQ = 2048
DEPTH = 2
DEC_BATCH = 128
DEC_SEQ = 1
PAST_LEN = 8192
PAGE_SIZE = 128

EPS = 1e-6
D_A = D_MODEL
CONV_A_W = 3
D_SSM = 2 * D_MODEL
SSM_HEADDIM = 64
H_SSM = D_SSM // SSM_HEADDIM
G_SSM = 8
N_SSM = 128
SSM_CONV_W = 4
SSM_CONV_DIM = D_SSM + 2 * G_SSM * N_SSM
SSD_CHUNK = 128
H_ATT = 16
KV_ATT = 4
HD_ATT = 64
ROT_DIM = HD_ATT // 4
ROPE_THETA = 500000.0
WINDOW = 128
MEM_LEN = 256
X_H = 4
X_HD = D_MODEL // X_H
D_FF = 2816
FFN_CONV_W = 3
N_BRANCH = 3
IN_SIZES = (D_A, D_A, D_A, D_SSM, SSM_CONV_DIM, H_SSM, H_ATT * HD_ATT, KV_ATT * HD_ATT, KV_ATT * HD_ATT, N_BRANCH * D_MODEL)
IN_DIM = sum(IN_SIZES)

kernel_name = 'hybrid_gated_parallel_decoder_step'


def split_cols(z, sizes):
    idx = np.cumsum(np.array(sizes))[:-1].tolist()
    return jnp.split(z, idx, axis=-1)


def rmsnorm(x, g):
    xf = x.astype(jnp.float32)
    y = xf * lax.rsqrt(jnp.mean(xf * xf, axis=-1, keepdims=True) + EPS)
    return (y * g.astype(jnp.float32)).astype(x.dtype)


def causal_dwconv(x, buf, w, b=None):
    L = x.shape[1]
    xx = jnp.concatenate([buf.astype(x.dtype), x], axis=1)
    y = xx[:, 0:L] * w[0]
    for k in range(1, w.shape[0]):
        y = y + xx[:, k:k + L] * w[k]
    if b is not None:
        y = y + b
    return y, xx[:, L:]


def partial_rope(x, pos):
    half = ROT_DIM // 2
    inv = ROPE_THETA ** (-jnp.arange(half, dtype=jnp.float32) / half)
    ang = pos.astype(jnp.float32)[:, None] * inv[None, :]
    cos = jnp.cos(ang)[None, :, None, :]
    sin = jnp.sin(ang)[None, :, None, :]
    xr = x[..., :ROT_DIM].astype(jnp.float32)
    x1, x2 = xr[..., :half], xr[..., half:]
    rot = jnp.concatenate([x1 * cos - x2 * sin, x2 * cos + x1 * sin], axis=-1)
    return jnp.concatenate([rot.astype(x.dtype), x[..., ROT_DIM:]], axis=-1)


def sink_attention(q, k, v, q_pos, k_pos, sinks):
    s = jnp.einsum('bnqgrd,bnkgd->bngrqk', q, k).astype(jnp.float32) * (HD_ATT ** -0.5)
    diff = q_pos[:, :, None] - k_pos[:, None, :]
    mask = (diff >= 0) & (diff < WINDOW) & (k_pos[:, None, :] >= 0)
    s = jnp.where(mask[None, :, None, None], s, -jnp.inf)
    sink = sinks.astype(jnp.float32).reshape(KV_ATT, H_ATT // KV_ATT)[None, None, :, :, None, None]
    m = jnp.maximum(jnp.max(s, axis=-1, keepdims=True), sink)
    e = jnp.exp(s - m)
    prob = e / (jnp.sum(e, axis=-1, keepdims=True) + jnp.exp(sink - m))
    return jnp.einsum('bngrqk,bnkgd->bnqgrd', prob.astype(v.dtype), v)


def ssd_scan(x, dt, A, B, C, h0):
    b, L, H, P = x.shape
    G, N = B.shape[2], B.shape[3]
    R = H // G
    Q = min(SSD_CHUNK, L)
    nc = -(-L // Q)
    pad = nc * Q - L
    f32 = jnp.float32
    x = jnp.pad(x.astype(f32), ((0, 0), (0, pad), (0, 0), (0, 0))).reshape(b, nc, Q, G, R, P)
    dt = jnp.pad(dt.astype(f32), ((0, 0), (0, pad), (0, 0))).reshape(b, nc, Q, G, R)
    B = jnp.pad(B.astype(f32), ((0, 0), (0, pad), (0, 0), (0, 0))).reshape(b, nc, Q, G, N)
    C = jnp.pad(C.astype(f32), ((0, 0), (0, pad), (0, 0), (0, 0))).reshape(b, nc, Q, G, N)
    cs = jnp.cumsum(dt * A.reshape(G, R), axis=2)
    xdt = x * dt[..., None]
    seg = cs[:, :, :, None] - cs[:, :, None, :]
    causal = jnp.tril(jnp.ones((Q, Q), dtype=bool))[:, :, None, None]
    Lmat = jnp.exp(jnp.where(causal, seg, -jnp.inf))
    CB = jnp.einsum('bclgn,bcsgn->bclsg', C, B)
    y_diag = jnp.einsum('bclsg,bclsgr,bcsgrp->bclgrp', CB, Lmat, xdt)
    decay_s = jnp.exp(cs[:, :, -1:] - cs)
    states = jnp.einsum('bcsgn,bcsgr,bcsgrp->bcgrpn', B, decay_s, xdt)
    chunk_decay = jnp.exp(cs[:, :, -1])

    def step(hc, inp):
        st, dec = inp
        return hc * dec[..., None, None] + st, hc

    h_last, h_prev = lax.scan(step, h0.astype(f32).reshape(b, G, R, P, N),
                              (jnp.moveaxis(states, 1, 0), jnp.moveaxis(chunk_decay, 1, 0)))
    h_prev = jnp.moveaxis(h_prev, 0, 1)
    y_off = jnp.einsum('bclgn,bcgrpn,bclgr->bclgrp', C, h_prev, jnp.exp(cs))
    y = (y_diag + y_off).reshape(b, nc * Q, H, P)[:, :L]
    return y, h_last.reshape(b, H, P, N)


def gated_parallel_mixer(h, pos, buf_a, buf_ssm, ssm0, swa_k, swa_v, p):
    b, L, _ = h.shape
    (v_a, gb_a, gc_a, z_s, xbc, dt_raw, q, k, v, gates) = split_cols(h @ p['w_in'], IN_SIZES)
    u_a, new_buf_a = causal_dwconv(gc_a * v_a, buf_a, p['conv_a_w'])
    y_a = (gb_a * u_a) @ p['w_a_out']
    xbc, new_buf_ssm = causal_dwconv(xbc, buf_ssm, p['ssm_conv_w'], p['ssm_conv_b'])
    xbc = jax.nn.silu(xbc)
    xs, Bs, Cs = split_cols(xbc, (D_SSM, G_SSM * N_SSM, G_SSM * N_SSM))
    dt = jax.nn.softplus((dt_raw + p['ssm_dt_bias']).astype(jnp.float32))
    A = -jnp.exp(p['ssm_a_log'].astype(jnp.float32))
    xh = xs.reshape(b, L, H_SSM, SSM_HEADDIM)
    y_s, new_ssm = ssd_scan(xh, dt, A, Bs.reshape(b, L, G_SSM, N_SSM), Cs.reshape(b, L, G_SSM, N_SSM), ssm0)
    y_s = (y_s + xh.astype(jnp.float32) * p['ssm_d'].astype(jnp.float32)[:, None]).astype(h.dtype)
    y_s = rmsnorm(y_s.reshape(b, L, D_SSM) * jax.nn.silu(z_s), p['ssm_norm'])
    y_b = y_s @ p['w_ssm_out']
    q = partial_rope(q.reshape(b, L, H_ATT, HD_ATT), pos).reshape(b, L, KV_ATT, H_ATT // KV_ATT, HD_ATT)
    k = partial_rope(k.reshape(b, L, KV_ATT, HD_ATT), pos)
    v = v.reshape(b, L, KV_ATT, HD_ATT)
    if swa_k is None:
        nb = L // WINDOW
        qb = q.reshape(b, nb, WINDOW, KV_ATT, H_ATT // KV_ATT, HD_ATT)
        zpad = jnp.zeros((b, WINDOW, KV_ATT, HD_ATT), k.dtype)
        kp = jnp.concatenate([zpad, k], axis=1).reshape(b, nb + 1, WINDOW, KV_ATT, HD_ATT)
        vp = jnp.concatenate([zpad, v], axis=1).reshape(b, nb + 1, WINDOW, KV_ATT, HD_ATT)
        kb = jnp.concatenate([kp[:, :-1], kp[:, 1:]], axis=2)
        vb = jnp.concatenate([vp[:, :-1], vp[:, 1:]], axis=2)
        q_pos = pos.reshape(nb, WINDOW)
        k_pos = (jnp.arange(nb)[:, None] - 1) * WINDOW + jnp.arange(2 * WINDOW)[None, :]
        o = sink_attention(qb, kb, vb, q_pos, k_pos, p['attn_sinks'])
        new_k, new_v = k[:, L - WINDOW:], v[:, L - WINDOW:]
    else:
        n_buf = swa_k.shape[1]
        kk = jnp.concatenate([swa_k.astype(k.dtype), k], axis=1)
        vv = jnp.concatenate([swa_v.astype(v.dtype), v], axis=1)
        k_pos = pos[0] - n_buf + jnp.arange(n_buf + L)
        o = sink_attention(q[:, None], kk[:, None], vv[:, None], pos[None], k_pos[None], p['attn_sinks'])
        new_k, new_v = kk[:, L:], vv[:, L:]
    y_c = o.reshape(b, L, H_ATT * HD_ATT) @ p['w_attn_out']
    g_a, g_b, g_c = split_cols(jax.nn.sigmoid(gates), (D_MODEL, D_MODEL, D_MODEL))
    out = (g_a * y_a + g_b * y_b + g_c * y_c) @ p['w_out']
    return out, (new_buf_a, new_buf_ssm, new_ssm, new_k, new_v)


def memory_kv(mem, p):
    b, M, _ = mem.shape
    m = rmsnorm(mem, p['norm_mem'])
    return (m @ p['w_xk']).reshape(b, M, X_H, X_HD), (m @ p['w_xv']).reshape(b, M, X_H, X_HD)


def cross_attention(h, mem_k, mem_v, p):
    b, L, _ = h.shape
    q = (h @ p['w_xq']).reshape(b, L, X_H, X_HD)
    s = jnp.einsum('bqhd,bkhd->bhqk', q, mem_k.astype(h.dtype)).astype(jnp.float32) * (X_HD ** -0.5)
    a = jax.nn.softmax(s, axis=-1).astype(h.dtype)
    o = jnp.einsum('bhqk,bkhd->bqhd', a, mem_v.astype(h.dtype)).reshape(b, L, X_H * X_HD)
    return o @ p['w_xo']


def conv_ffn(h, buf, p):
    a, g = split_cols(h @ p['w_ffn_in'], (D_FF, D_FF))
    a, new_buf = causal_dwconv(a, buf, p['ffn_conv_w'], p['ffn_conv_b'])
    return (jax.nn.silu(a) * g) @ p['w_ffn_out'], new_buf


def decoder_layer(x, pos, st, mem_k, mem_v, p):
    buf_a, buf_ssm, ssm0, swa_k, swa_v, buf_ffn = st
    m, mix_state = gated_parallel_mixer(rmsnorm(x, p['norm_mix_pre']), pos, buf_a, buf_ssm, ssm0, swa_k, swa_v, p)
    x = x + rmsnorm(m, p['norm_mix_post'])
    x = x + rmsnorm(cross_attention(rmsnorm(x, p['norm_x_pre']), mem_k, mem_v, p), p['norm_x_post'])
    f, new_buf_ffn = conv_ffn(rmsnorm(x, p['norm_ffn_pre']), buf_ffn, p)
    x = x + rmsnorm(f, p['norm_ffn_post'])
    return x, mix_state + (new_buf_ffn,)


def setup_inputs(seed: int = 0) -> dict:
    key = jax.random.key(seed)
    ks = list(jax.random.split(key, 48))

    def nrm(i, shape, scale):
        return scale * jax.random.normal(ks[i], shape, jnp.float32)

    def gain(i, shape):
        return 1.0 + nrm(i, shape, 0.05)

    n_rows = min(WINDOW, PAST_LEN)
    dt0 = jnp.exp(jax.random.uniform(ks[40], (DEPTH, H_SSM), jnp.float32, np.log(1e-3), np.log(1e-1)))
    return {
        'x_prompt': nrm(0, (BATCH, SEQ, D_MODEL), 1.0),
        'x_sample': nrm(1, (DEC_BATCH, DEC_SEQ, D_MODEL), 1.0),
        'mem_prompt': nrm(2, (BATCH, MEM_LEN, D_MODEL), 1.0),
        'state_conv_a': nrm(3, (DEPTH, DEC_BATCH, CONV_A_W - 1, D_A), 1.0),
        'state_ssm_conv': nrm(4, (DEPTH, DEC_BATCH, SSM_CONV_W - 1, SSM_CONV_DIM), 1.0),
        'state_ssm': nrm(5, (DEPTH, DEC_BATCH, H_SSM, SSM_HEADDIM, N_SSM), 0.1),
        'cache_swa_k': nrm(6, (DEPTH, DEC_BATCH, n_rows, KV_ATT, HD_ATT), 1.0),
        'cache_swa_v': nrm(7, (DEPTH, DEC_BATCH, n_rows, KV_ATT, HD_ATT), 1.0),
        'cache_mem_k': nrm(8, (DEPTH, DEC_BATCH, MEM_LEN, X_H, X_HD), 1.0),
        'cache_mem_v': nrm(9, (DEPTH, DEC_BATCH, MEM_LEN, X_H, X_HD), 1.0),
        'state_ffn_conv': nrm(10, (DEPTH, DEC_BATCH, FFN_CONV_W - 1, D_FF), 1.0),
        'norm_mix_pre': gain(11, (DEPTH, D_MODEL)),
        'norm_mix_post': gain(12, (DEPTH, D_MODEL)),
        'w_in': nrm(13, (DEPTH, D_MODEL, IN_DIM), D_MODEL ** -0.5),
        'conv_a_w': nrm(14, (DEPTH, CONV_A_W, D_A), CONV_A_W ** -0.5),
        'w_a_out': nrm(15, (DEPTH, D_A, D_MODEL), D_A ** -0.5),
        'ssm_conv_w': nrm(16, (DEPTH, SSM_CONV_W, SSM_CONV_DIM), SSM_CONV_W ** -0.5),
        'ssm_conv_b': nrm(17, (DEPTH, SSM_CONV_DIM), 0.02),
        'ssm_dt_bias': dt0 + jnp.log(-jnp.expm1(-dt0)),
        'ssm_a_log': jnp.log(jax.random.uniform(ks[18], (DEPTH, H_SSM), jnp.float32, 1.0, 16.0)),
        'ssm_d': 1.0 + nrm(19, (DEPTH, H_SSM), 0.1),
        'ssm_norm': gain(20, (DEPTH, D_SSM)),
        'w_ssm_out': nrm(21, (DEPTH, D_SSM, D_MODEL), D_SSM ** -0.5),
        'attn_sinks': nrm(22, (DEPTH, H_ATT), 0.5),
        'w_attn_out': nrm(23, (DEPTH, H_ATT * HD_ATT, D_MODEL), (H_ATT * HD_ATT) ** -0.5),
        'w_out': nrm(24, (DEPTH, D_MODEL, D_MODEL), D_MODEL ** -0.5),
        'norm_x_pre': gain(25, (DEPTH, D_MODEL)),
        'norm_x_post': gain(26, (DEPTH, D_MODEL)),
        'norm_mem': gain(27, (DEPTH, D_MODEL)),
        'w_xq': nrm(28, (DEPTH, D_MODEL, X_H * X_HD), D_MODEL ** -0.5),
        'w_xk': nrm(29, (DEPTH, D_MODEL, X_H * X_HD), D_MODEL ** -0.5),
        'w_xv': nrm(30, (DEPTH, D_MODEL, X_H * X_HD), D_MODEL ** -0.5),
        'w_xo': nrm(31, (DEPTH, X_H * X_HD, D_MODEL), (X_H * X_HD) ** -0.5),
        'norm_ffn_pre': gain(32, (DEPTH, D_MODEL)),
        'norm_ffn_post': gain(33, (DEPTH, D_MODEL)),
        'w_ffn_in': nrm(34, (DEPTH, D_MODEL, 2 * D_FF), D_MODEL ** -0.5),
        'ffn_conv_w': nrm(35, (DEPTH, FFN_CONV_W, D_FF), FFN_CONV_W ** -0.5),
        'ffn_conv_b': nrm(36, (DEPTH, D_FF), 0.02),
        'w_ffn_out': nrm(37, (DEPTH, D_FF, D_MODEL), D_FF ** -0.5),
    }


def reference(x_prompt, x_sample, mem_prompt, state_conv_a, state_ssm_conv, state_ssm, cache_swa_k, cache_swa_v,
              cache_mem_k, cache_mem_v, state_ffn_conv, norm_mix_pre, norm_mix_post, w_in, conv_a_w, w_a_out,
              ssm_conv_w, ssm_conv_b, ssm_dt_bias, ssm_a_log, ssm_d, ssm_norm, w_ssm_out, attn_sinks, w_attn_out,
              w_out, norm_x_pre, norm_x_post, norm_mem, w_xq, w_xk, w_xv, w_xo, norm_ffn_pre, norm_ffn_post,
              w_ffn_in, ffn_conv_w, ffn_conv_b, w_ffn_out):
    bp, Lp, _ = x_prompt.shape
    Ls = x_sample.shape[1]
    pos_p = jnp.arange(Lp, dtype=jnp.int32)
    pos_s = PAST_LEN + jnp.arange(Ls, dtype=jnp.int32)
    yp, ys = x_prompt, x_sample
    new_p, new_s = [], []
    for l in range(DEPTH):
        p = {'norm_mix_pre': norm_mix_pre[l], 'norm_mix_post': norm_mix_post[l], 'w_in': w_in[l],
             'conv_a_w': conv_a_w[l], 'w_a_out': w_a_out[l], 'ssm_conv_w': ssm_conv_w[l],
             'ssm_conv_b': ssm_conv_b[l], 'ssm_dt_bias': ssm_dt_bias[l], 'ssm_a_log': ssm_a_log[l],
             'ssm_d': ssm_d[l], 'ssm_norm': ssm_norm[l], 'w_ssm_out': w_ssm_out[l], 'attn_sinks': attn_sinks[l],
             'w_attn_out': w_attn_out[l], 'w_out': w_out[l], 'norm_x_pre': norm_x_pre[l],
             'norm_x_post': norm_x_post[l], 'norm_mem': norm_mem[l], 'w_xq': w_xq[l], 'w_xk': w_xk[l],
             'w_xv': w_xv[l], 'w_xo': w_xo[l], 'norm_ffn_pre': norm_ffn_pre[l], 'norm_ffn_post': norm_ffn_post[l],
             'w_ffn_in': w_ffn_in[l], 'ffn_conv_w': ffn_conv_w[l], 'ffn_conv_b': ffn_conv_b[l],
             'w_ffn_out': w_ffn_out[l]}
        zero_st = (jnp.zeros((bp, CONV_A_W - 1, D_A), yp.dtype),
                   jnp.zeros((bp, SSM_CONV_W - 1, SSM_CONV_DIM), yp.dtype),
                   jnp.zeros((bp, H_SSM, SSM_HEADDIM, N_SSM), jnp.float32),
                   None, None,
                   jnp.zeros((bp, FFN_CONV_W - 1, D_FF), yp.dtype))
        mk, mv = memory_kv(mem_prompt, p)
        yp, st_p = decoder_layer(yp, pos_p, zero_st, mk, mv, p)
        new_p.append(st_p + (mk, mv))
        st_in = (state_conv_a[l], state_ssm_conv[l], state_ssm[l], cache_swa_k[l], cache_swa_v[l], state_ffn_conv[l])
        ys, st_s = decoder_layer(ys, pos_s, st_in, cache_mem_k[l], cache_mem_v[l], p)
        new_s.append(st_s)
    p_conv_a = jnp.stack([s[0] for s in new_p])
    p_ssm_conv = jnp.stack([s[1] for s in new_p])
    p_ssm = jnp.stack([s[2] for s in new_p])
    p_swa_k = jnp.stack([s[3] for s in new_p])
    p_swa_v = jnp.stack([s[4] for s in new_p])
    p_ffn_conv = jnp.stack([s[5] for s in new_p])
    p_mem_k = jnp.stack([s[6] for s in new_p])
    p_mem_v = jnp.stack([s[7] for s in new_p])
    s_conv_a = jnp.stack([s[0] for s in new_s])
    s_ssm_conv = jnp.stack([s[1] for s in new_s])
    s_ssm = jnp.stack([s[2] for s in new_s])
    s_swa_k = jnp.stack([s[3] for s in new_s])
    s_swa_v = jnp.stack([s[4] for s in new_s])
    s_ffn_conv = jnp.stack([s[5] for s in new_s])
    return (yp, ys, p_conv_a, p_ssm_conv, p_ssm, p_swa_k, p_swa_v, p_ffn_conv, p_mem_k, p_mem_v,
            s_conv_a, s_ssm_conv, s_ssm, s_swa_k, s_swa_v, s_ffn_conv)
```

```python
import functools

import numpy as np
import jax
import jax.numpy as jnp
from jax import lax
from jax.experimental import pallas as pl
from jax.experimental.pallas import tpu as pltpu

F32 = jnp.float32
BF16 = jnp.bfloat16

EPS = 1e-6
D_MODEL = 1024
D_A = D_MODEL
CONV_A_W = 3
D_SSM = 2 * D_MODEL
SSM_HEADDIM = 64
H_SSM = D_SSM // SSM_HEADDIM
G_SSM = 8
N_SSM = 128
SSM_CONV_W = 4
SSM_CONV_DIM = D_SSM + 2 * G_SSM * N_SSM
SSD_CHUNK = 128
H_ATT = 16
KV_ATT = 4
HD_ATT = 64
ROT_DIM = HD_ATT // 4
ROPE_THETA = 500000.0
WINDOW = 128
X_H = 4
X_HD = D_MODEL // X_H
D_FF = 2816
FFN_CONV_W = 3
PAST_LEN = 8192

LANES = 128
SUBLANES = 8
VMEM_LIMIT = 48 * 1024 * 1024
VMEM_LIMIT_BIG = 56 * 1024 * 1024

OFF_XBC = 0
OFF_ZS = OFF_XBC + SSM_CONV_DIM
OFF_ZA = OFF_ZS + D_SSM
OFF_GATES = OFF_ZA + 3 * D_A
OFF_Q = OFF_GATES + 3 * D_MODEL
OFF_K = OFF_Q + H_ATT * HD_ATT
OFF_V = OFF_K + KV_ATT * HD_ATT
N_MAIN = OFF_V + KV_ATT * HD_ATT
FF_CHUNK = D_FF // 2


def _cparams(n_axes, vmem=VMEM_LIMIT):
    return pltpu.CompilerParams(dimension_semantics=("arbitrary",) * n_axes,
                                vmem_limit_bytes=vmem)


def _rms(x, g):
    return x * lax.rsqrt(jnp.mean(x * x, axis=-1, keepdims=True) + EPS) * g


def _sigmoid(x):
    return 0.5 + 0.5 * jnp.tanh(0.5 * x)


def _silu(x):
    hx = 0.5 * x
    return hx + hx * jnp.tanh(hx)


def _softplus(x):
    return jnp.maximum(x, 0.0) + jnp.log(1.0 + jnp.exp(-jnp.abs(x)))


def _dot(a, b):
    return jnp.dot(a, b, preferred_element_type=F32)


def _dot_nt(a, b):
    return lax.dot_general(a, b, (((1,), (1,)), ((), ())), preferred_element_type=F32)


def _dot_exact_rhs(a01, v):
    hi = v.astype(BF16)
    r1 = v - hi.astype(F32)
    mid = r1.astype(BF16)
    lo = (r1 - mid.astype(F32)).astype(BF16)
    return _dot(a01, hi) + _dot(a01, mid) + _dot(a01, lo)


def _dot_exact_lhs(v, b01):
    hi = v.astype(BF16)
    r1 = v - hi.astype(F32)
    mid = r1.astype(BF16)
    lo = (r1 - mid.astype(F32)).astype(BF16)
    return _dot(hi, b01) + _dot(mid, b01) + _dot(lo, b01)


def _rope(x, c, s1, s2):
    outs = []
    for i in range(x.shape[1] // LANES):
        xi = x[:, i * LANES:(i + 1) * LANES]
        outs.append(xi * c + pltpu.roll(xi, LANES - ROT_DIM // 2, 1) * s1
                    + pltpu.roll(xi, ROT_DIM // 2, 1) * s2)
    return outs[0] if len(outs) == 1 else jnp.concatenate(outs, axis=1)


def _norm_mm_kernel(x_ref, g_ref, w_ref, o_ref, h_scr):
    @pl.when(pl.program_id(1) == 0)
    def _():
        h_scr[...] = _rms(x_ref[...], g_ref[...]).astype(BF16)

    o_ref[...] = _dot(h_scr[...], w_ref[...]).astype(o_ref.dtype)


def _norm_matmul(x, g, w, out_dtype, tm, tn):
    m, k = x.shape
    n = w.shape[1]
    return pl.pallas_call(
        _norm_mm_kernel,
        grid=(m // tm, n // tn),
        in_specs=[pl.BlockSpec((tm, k), lambda i, j: (i, 0)),
                  pl.BlockSpec((1, k), lambda i, j: (0, 0)),
                  pl.BlockSpec((k, tn), lambda i, j: (0, j))],
        out_specs=pl.BlockSpec((tm, tn), lambda i, j: (i, j)),
        out_shape=jax.ShapeDtypeStruct((m, n), out_dtype),
        scratch_shapes=[pltpu.VMEM((tm, k), BF16)],
        compiler_params=_cparams(2),
        name="norm_matmul",
    )(x, g, w)


def _inproj_kernel(x_ref, g_ref, w_ref, wdt_ref, z_ref, dt_ref, h_scr):
    @pl.when(pl.program_id(1) == 0)
    def _():
        h = _rms(x_ref[...], g_ref[...]).astype(BF16)
        h_scr[...] = h
        dt_ref[...] = _dot(h, wdt_ref[...])

    z_ref[...] = _dot(h_scr[...], w_ref[...]).astype(z_ref.dtype)


def _in_proj(x, g, w, wdt, tm, tn):
    m, k = x.shape
    n = w.shape[1]
    return pl.pallas_call(
        _inproj_kernel,
        grid=(m // tm, n // tn),
        in_specs=[pl.BlockSpec((tm, k), lambda i, j: (i, 0)),
                  pl.BlockSpec((1, k), lambda i, j: (0, 0)),
                  pl.BlockSpec((k, tn), lambda i, j: (0, j)),
                  pl.BlockSpec((k, LANES), lambda i, j: (0, 0))],
        out_specs=[pl.BlockSpec((tm, tn), lambda i, j: (i, j)),
                   pl.BlockSpec((tm, LANES), lambda i, j: (i, 0))],
        out_shape=[jax.ShapeDtypeStruct((m, n), BF16),
                   jax.ShapeDtypeStruct((m, LANES), F32)],
        scratch_shapes=[pltpu.VMEM((tm, k), BF16)],
        compiler_params=_cparams(2),
        name="in_proj",
    )(x, g, w, wdt)


def _mm_norm_res_kernel(a_ref, w_ref, g_ref, x_ref, o_ref):
    y = _dot(a_ref[...].astype(BF16), w_ref[...])
    o_ref[...] = x_ref[...] + _rms(y, g_ref[...])


def _mm_norm_res(a, w, g, x):
    m, k = a.shape
    n = w.shape[1]
    return pl.pallas_call(
        _mm_norm_res_kernel,
        grid=(1,),
        in_specs=[pl.BlockSpec((m, k), lambda i: (0, 0)),
                  pl.BlockSpec((k, n), lambda i: (0, 0)),
                  pl.BlockSpec((1, n), lambda i: (0, 0)),
                  pl.BlockSpec((m, n), lambda i: (0, 0))],
        out_specs=pl.BlockSpec((m, n), lambda i: (0, 0)),
        out_shape=jax.ShapeDtypeStruct((m, n), F32),
        compiler_params=_cparams(1),
        name="mm_norm_res",
    )(a, w, g, x)


def _ssd_kernel(xbc_ref, zs_ref, dtr_ref, cw_ref, cb_ref, dtb_ref, alog_ref, dx_ref, nrm_ref, e_ref,
                y_ref, hout_ref, cbuf_ref, x_scr, ht_scr):
    c = pl.program_id(1)
    q = xbc_ref.shape[0]
    pad = SUBLANES

    @pl.when(c == 0)
    def _():
        x_scr[...] = jnp.zeros((pad, SSM_CONV_DIM), F32)
        ht_scr[...] = jnp.zeros(ht_scr.shape, F32)

    xin = xbc_ref[...].astype(F32)
    ext = jnp.concatenate([x_scr[...], xin], axis=0)
    cw = cw_ref[...]
    xc = cb_ref[...] + xin * cw[SSM_CONV_W - 1:SSM_CONV_W, :]
    for k in range(SSM_CONV_W - 1):
        xc = xc + pltpu.roll(ext, SSM_CONV_W - 1 - k, 0)[pad:pad + q, :] * cw[k:k + 1, :]
    xc = _silu(xc)
    tail = xin[q - pad:q, :]
    x_scr[...] = tail
    cbuf_ref[0] = tail

    xs = xc[:, :D_SSM]
    bs = xc[:, D_SSM:D_SSM + G_SSM * N_SSM]
    cs_in = xc[:, D_SSM + G_SSM * N_SSM:]

    dt = _softplus(dtr_ref[...] + dtb_ref[...])
    a = dt * (-jnp.exp(alog_ref[...]))
    row = lax.broadcasted_iota(jnp.int32, (q, q), 0)
    col = lax.broadcasted_iota(jnp.int32, (q, q), 1)
    tri = row >= col
    cs = _dot_exact_rhs(jnp.where(tri, 1.0, 0.0).astype(BF16), a)
    cs_last = cs[q - 1:q, :]
    cs_t = cs.T
    dt_t = dt.T
    wdd = dt * jnp.exp(cs_last - cs)
    cdec = jnp.broadcast_to(jnp.exp(cs_last), (SUBLANES, LANES))
    cdec_x = _dot_exact_lhs(cdec, e_ref[...])[0:1, :]

    lane = lax.broadcasted_iota(jnp.int32, (q, LANES), 1)
    heads_per_group = H_SSM // G_SSM
    y_tiles = []
    for g in range(G_SSM):
        bg = bs[:, g * N_SSM:(g + 1) * N_SSM]
        cg = cs_in[:, g * N_SSM:(g + 1) * N_SSM]
        cb = _dot_nt(cg.astype(BF16), bg.astype(BF16))
        xdd_tiles = []
        for pr in range(heads_per_group // 2):
            hp = g * (heads_per_group // 2) + pr
            xpair = xs[:, hp * LANES:(hp + 1) * LANES]
            wp = jnp.concatenate([xpair, ht_scr[:, hp * LANES:(hp + 1) * LANES]], axis=0).astype(BF16)
            parts = []
            wsel = []
            for r2 in range(2):
                h = 2 * hp + r2
                csb = jnp.broadcast_to(cs[:, h:h + 1], (q, LANES))
                seg = csb - cs_t[h:h + 1, :]
                lmat = jnp.exp(jnp.where(tri, seg, -jnp.inf))
                mmat = cb * lmat * dt_t[h:h + 1, :]
                csc = cg * jnp.exp(csb)
                lhs = jnp.concatenate([mmat, csc], axis=1).astype(BF16)
                parts.append(_dot(lhs, wp))
                wsel.append(jnp.broadcast_to(wdd[:, h:h + 1], (q, LANES)))
            y_tiles.append(jnp.where(lane < SSM_HEADDIM, parts[0], parts[1]))
            xdd_tiles.append(xpair * jnp.where(lane < SSM_HEADDIM, wsel[0], wsel[1]))
        xdd = jnp.concatenate(xdd_tiles, axis=1).astype(BF16)
        st_t = _dot(bg.T.astype(BF16), xdd)
        lo, hi = g * heads_per_group * SSM_HEADDIM, (g + 1) * heads_per_group * SSM_HEADDIM
        ht_scr[:, lo:hi] = ht_scr[:, lo:hi] * cdec_x[:, lo:hi] + st_t

    y = jnp.concatenate(y_tiles, axis=1) + xs * dx_ref[...]
    y = y * _silu(zs_ref[...].astype(F32))
    y_ref[...] = _rms(y, nrm_ref[...]).astype(y_ref.dtype)

    @pl.when(c == pl.num_programs(1) - 1)
    def _():
        hout_ref[0] = ht_scr[...].T


def _ssd_prompt(z, dtr, p, nb, seq):
    q = SSD_CHUNK
    nc = seq // q
    row = lambda b, c: b * nc + c
    const = lambda b, c: (0, 0)
    return pl.pallas_call(
        _ssd_kernel,
        grid=(nb, nc),
        in_specs=[pl.BlockSpec((q, SSM_CONV_DIM), lambda b, c: (row(b, c), OFF_XBC // SSM_CONV_DIM)),
                  pl.BlockSpec((q, D_SSM), lambda b, c: (row(b, c), OFF_ZS // D_SSM)),
                  pl.BlockSpec((q, LANES), lambda b, c: (row(b, c), 0)),
                  pl.BlockSpec((SUBLANES, SSM_CONV_DIM), const),
                  pl.BlockSpec((1, SSM_CONV_DIM), const),
                  pl.BlockSpec((1, LANES), const),
                  pl.BlockSpec((1, LANES), const),
                  pl.BlockSpec((1, D_SSM), const),
                  pl.BlockSpec((1, D_SSM), const),
                  pl.BlockSpec((LANES, D_SSM), const)],
        out_specs=[pl.BlockSpec((q, D_SSM), lambda b, c: (row(b, c), 0)),
                   pl.BlockSpec((1, D_SSM, N_SSM), lambda b, c: (b, 0, 0)),
                   pl.BlockSpec((1, SUBLANES, SSM_CONV_DIM), lambda b, c: (b, 0, 0))],
        out_shape=[jax.ShapeDtypeStruct((nb * seq, D_SSM), BF16),
                   jax.ShapeDtypeStruct((nb, D_SSM, N_SSM), F32),
                   jax.ShapeDtypeStruct((nb, SUBLANES, SSM_CONV_DIM), F32)],
        scratch_shapes=[pltpu.VMEM((SUBLANES, SSM_CONV_DIM), F32),
                        pltpu.VMEM((N_SSM, D_SSM), F32)],
        compiler_params=_cparams(2),
        name="ssd_prompt",
    )(z, z, dtr, p["ssm_conv_w"], p["ssm_conv_b"], p["dt_bias"], p["a_log"], p["d_x"], p["ssm_norm"],
      p["expand"])


def _swa_kernel(sink_ref, q_ref, k_ref, v_ref, c_ref, s1_ref, s2_ref, o_ref, kn_ref, vn_ref,
                kp_scr, vp_scr):
    qb = pl.program_id(1)
    w = WINDOW
    kvw = KV_ATT * HD_ATT

    @pl.when(qb == 0)
    def _():
        kp_scr[...] = jnp.zeros((w, kvw), F32)
        vp_scr[...] = jnp.zeros((w, kvw), F32)

    c, s1, s2 = c_ref[...], s1_ref[...], s2_ref[...]
    qv = _rope(q_ref[...].astype(F32), c, s1, s2) * (HD_ATT ** -0.5)
    ko = _rope(k_ref[...].astype(F32), c, s1, s2)
    vo = v_ref[...].astype(F32)
    kn_ref[0] = ko
    vn_ref[0] = vo
    kcat = jnp.concatenate([kp_scr[...], ko], axis=0)
    vcat = jnp.concatenate([vp_scr[...], vo], axis=0)
    kp_scr[...] = ko
    vp_scr[...] = vo
    vcat_t = vcat.T.astype(BF16)

    rep = H_ATT // KV_ATT
    keyj = lax.broadcasted_iota(jnp.int32, (2 * w, w), 0)
    qryl = lax.broadcasted_iota(jnp.int32, (2 * w, w), 1)
    valid = (keyj > qryl) & (keyj <= qryl + w) & ((qb > 0) | (keyj >= w))
    valid = jnp.concatenate([valid] * rep, axis=1)
    lane_kv = lax.broadcasted_iota(jnp.int32, (2 * w, LANES), 1)
    lane_q = lax.broadcasted_iota(jnp.int32, (w, LANES), 1)
    lo_kv, lo_q = lane_kv < HD_ATT, lane_q < HD_ATT

    o_t = []
    for slab in range(kvw // LANES):
        a_k = kcat[:, slab * LANES:(slab + 1) * LANES]
        b_k = pltpu.roll(a_k, HD_ATT, 1)
        for gi in range(2):
            g = 2 * slab + gi
            kdup = jnp.where(lo_kv, a_k, b_k) if gi == 0 else jnp.where(lo_kv, b_k, a_k)
            q_tiles, sink_tiles = [], []
            for pr in range(rep // 2):
                qp = qv[:, (rep // 2 * g + pr) * LANES:(rep // 2 * g + pr + 1) * LANES]
                q_tiles.append(jnp.where(lo_q, qp, 0.0))
                q_tiles.append(jnp.where(lo_q, 0.0, qp))
                for r2 in range(2):
                    sink_tiles.append(jnp.full((1, w), sink_ref[rep * g + 2 * pr + r2], F32))
            qs = jnp.concatenate(q_tiles, axis=0).astype(BF16)
            sink = jnp.concatenate(sink_tiles, axis=1)
            s = _dot_nt(kdup.astype(BF16), qs)
            s = jnp.where(valid, s, -jnp.inf)
            m = jnp.maximum(jnp.max(s, axis=0, keepdims=True), sink)
            e = jnp.exp(s - m)
            den = jnp.sum(e, axis=0, keepdims=True) + jnp.exp(sink - m)
            o_g = _dot(vcat_t[g * HD_ATT:(g + 1) * HD_ATT, :], e.astype(BF16))
            o_g = o_g * (1.0 / den)
            for r in range(rep):
                o_t.append(o_g[:, r * w:(r + 1) * w])
    o_ref[...] = jnp.concatenate(o_t, axis=0).T.astype(o_ref.dtype)


def _swa_prompt(z, p, nb, seq):
    w = WINDOW
    nq = seq // w
    kvw = KV_ATT * HD_ATT
    row = lambda b, i: b * nq + i
    tab = pl.BlockSpec((w, LANES), lambda b, i: (i, 0))
    return pl.pallas_call(
        _swa_kernel,
        grid=(nb, nq),
        in_specs=[pl.BlockSpec(memory_space=pltpu.SMEM),
                  pl.BlockSpec((w, H_ATT * HD_ATT), lambda b, i: (row(b, i), OFF_Q // (H_ATT * HD_ATT))),
                  pl.BlockSpec((w, kvw), lambda b, i: (row(b, i), OFF_K // kvw)),
                  pl.BlockSpec((w, kvw), lambda b, i: (row(b, i), OFF_V // kvw)),
                  tab, tab, tab],
        out_specs=[pl.BlockSpec((w, H_ATT * HD_ATT), lambda b, i: (row(b, i), 0)),
                   pl.BlockSpec((1, w, kvw), lambda b, i: (b, 0, 0)),
                   pl.BlockSpec((1, w, kvw), lambda b, i: (b, 0, 0))],
        out_shape=[jax.ShapeDtypeStruct((nb * seq, H_ATT * HD_ATT), BF16),
                   jax.ShapeDtypeStruct((nb, w, kvw), F32),
                   jax.ShapeDtypeStruct((nb, w, kvw), F32)],
        scratch_shapes=[pltpu.VMEM((w, kvw), F32), pltpu.VMEM((w, kvw), F32)],
        compiler_params=_cparams(2),
        name="swa_prompt",
    )(p["sinks"], z, z, z, p["rope_c"], p["rope_s1"], p["rope_s2"])


def _tail_body(cv, u, gb, ys, oc, ga, gbg, gcg, x, wa_ref, ws_ref, wc_ref, wo_ref, gpost_ref):
    y_a = _dot((gb * u).astype(BF16), wa_ref[...])
    y_b = _dot(ys, ws_ref[...])
    y_c = _dot(oc, wc_ref[...])
    merged = _sigmoid(ga) * y_a + _sigmoid(gbg) * y_b + _sigmoid(gcg) * y_c
    out = _dot(merged.astype(BF16), wo_ref[...])
    return x + _rms(out, gpost_ref[...])


def _tail_prompt_kernel(va_ref, gb_ref, gc_ref, ys_ref, oc_ref, ga_ref, gbg_ref, gcg_ref, x_ref,
                        cw_ref, wa_ref, ws_ref, wc_ref, wo_ref, gpost_ref, xo_ref, bufo_ref, cv_scr):
    s = pl.program_id(1)
    tm = x_ref.shape[0]
    pad = SUBLANES

    @pl.when(s == 0)
    def _():
        cv_scr[0:pad, :] = jnp.zeros((pad, D_A), F32)

    cv = gc_ref[...].astype(F32) * va_ref[...].astype(F32)
    cv_scr[pad:pad + tm, :] = cv
    cw = cw_ref[...]
    u = cv * cw[CONV_A_W - 1:CONV_A_W, :]
    for k in range(CONV_A_W - 1):
        off = pad - (CONV_A_W - 1) + k
        u = u + cv_scr[off:off + tm, :] * cw[k:k + 1, :]
    tail = cv_scr[tm:tm + pad, :]
    cv_scr[0:pad, :] = tail
    bufo_ref[0] = tail
    xo_ref[...] = _tail_body(cv, u, gb_ref[...].astype(F32), ys_ref[...], oc_ref[...],
                             ga_ref[...].astype(F32), gbg_ref[...].astype(F32), gcg_ref[...].astype(F32),
                             x_ref[...], wa_ref, ws_ref, wc_ref, wo_ref, gpost_ref)


def _tail_prompt(z, ys, oc, x, p, nb, seq, tm):
    ns = seq // tm
    row = lambda b, s: b * ns + s
    const = lambda b, s: (0, 0)
    zblk = lambda off: pl.BlockSpec((tm, D_MODEL), lambda b, s: (row(b, s), off // D_MODEL))
    return pl.pallas_call(
        _tail_prompt_kernel,
        grid=(nb, ns),
        in_specs=[zblk(OFF_ZA), zblk(OFF_ZA + D_A), zblk(OFF_ZA + 2 * D_A),
                  pl.BlockSpec((tm, D_SSM), lambda b, s: (row(b, s), 0)),
                  pl.BlockSpec((tm, H_ATT * HD_ATT), lambda b, s: (row(b, s), 0)),
                  zblk(OFF_GATES), zblk(OFF_GATES + D_MODEL), zblk(OFF_GATES + 2 * D_MODEL),
                  pl.BlockSpec((tm, D_MODEL), lambda b, s: (row(b, s), 0)),
                  pl.BlockSpec((SUBLANES, D_A), const),
                  pl.BlockSpec((D_A, D_MODEL), const),
                  pl.BlockSpec((D_SSM, D_MODEL), const),
                  pl.BlockSpec((H_ATT * HD_ATT, D_MODEL), const),
                  pl.BlockSpec((D_MODEL, D_MODEL), const),
                  pl.BlockSpec((1, D_MODEL), const)],
        out_specs=[pl.BlockSpec((tm, D_MODEL), lambda b, s: (row(b, s), 0)),
                   pl.BlockSpec((1, SUBLANES, D_A), lambda b, s: (b, 0, 0))],
        out_shape=[jax.ShapeDtypeStruct((nb * seq, D_MODEL), F32),
                   jax.ShapeDtypeStruct((nb, SUBLANES, D_A), F32)],
        scratch_shapes=[pltpu.VMEM((tm + SUBLANES, D_A), F32)],
        compiler_params=_cparams(2),
        name="tail_prompt",
    )(z, z, z, ys, oc, z, z, z, x, p["conv_a_w"], p["w_a_out"], p["w_ssm_out"], p["w_attn_out"],
      p["w_out"], p["norm_mix_post"])


def _tail_sample_kernel(va_ref, gb_ref, gc_ref, ys_ref, oc_ref, ga_ref, gbg_ref, gcg_ref, x_ref,
                        b0_ref, b1_ref, cw_ref, wa_ref, ws_ref, wc_ref, wo_ref, gpost_ref,
                        xo_ref, cvo_ref):
    cv = gc_ref[...].astype(F32) * va_ref[...].astype(F32)
    cw = cw_ref[...]
    u = b0_ref[...] * cw[0:1, :] + b1_ref[...] * cw[1:2, :] + cv * cw[2:3, :]
    cvo_ref[...] = cv
    xo_ref[...] = _tail_body(cv, u, gb_ref[...].astype(F32), ys_ref[...], oc_ref[...],
                             ga_ref[...].astype(F32), gbg_ref[...].astype(F32), gcg_ref[...].astype(F32),
                             x_ref[...], wa_ref, ws_ref, wc_ref, wo_ref, gpost_ref)


def _tail_sample(z, ys, oc, x, b0, b1, p):
    m = x.shape[0]
    const = lambda i: (0, 0)
    zblk = lambda off: pl.BlockSpec((m, D_MODEL), lambda i: (0, off // D_MODEL))
    full = lambda a: pl.BlockSpec(a.shape, const)
    w_c = p["w_attn_out_exp"]
    return pl.pallas_call(
        _tail_sample_kernel,
        grid=(1,),
        in_specs=[zblk(OFF_ZA), zblk(OFF_ZA + D_A), zblk(OFF_ZA + 2 * D_A),
                  full(ys), full(oc),
                  zblk(OFF_GATES), zblk(OFF_GATES + D_MODEL), zblk(OFF_GATES + 2 * D_MODEL),
                  full(x), full(b0), full(b1),
                  full(p["conv_a_w"]), full(p["w_a_out"]), full(p["w_ssm_out"]), full(w_c),
                  full(p["w_out"]), full(p["norm_mix_post"])],
        out_specs=[pl.BlockSpec((m, D_MODEL), const), pl.BlockSpec((m, D_A), const)],
        out_shape=[jax.ShapeDtypeStruct((m, D_MODEL), F32), jax.ShapeDtypeStruct((m, D_A), F32)],
        compiler_params=_cparams(1, VMEM_LIMIT_BIG),
        name="tail_sample",
    )(z, z, z, ys, oc, z, z, z, x, b0, b1, p["conv_a_w"], p["w_a_out"], p["w_ssm_out"], w_c,
      p["w_out"], p["norm_mix_post"])


def _xattn_prompt_kernel(x_ref, k_ref, v_ref, gpre_ref, wq_ref, wo_ref, gpost_ref, xo_ref):
    x = x_ref[...]
    h = _rms(x, gpre_ref[...]).astype(BF16)
    qv = (_dot(h, wq_ref[...]) * (X_HD ** -0.5)).astype(BF16)
    o_tiles = []
    for hh in range(X_H):
        kh = k_ref[:, hh * X_HD:(hh + 1) * X_HD].astype(BF16)
        vh = v_ref[:, hh * X_HD:(hh + 1) * X_HD].astype(BF16)
        s = _dot_nt(qv[:, hh * X_HD:(hh + 1) * X_HD], kh)
        m = jnp.max(s, axis=-1, keepdims=True)
        e = jnp.exp(s - m)
        pb = (e * (1.0 / jnp.sum(e, axis=-1, keepdims=True))).astype(BF16)
        o_tiles.append(_dot(pb, vh))
    o = jnp.concatenate(o_tiles, axis=1).astype(BF16)
    xo_ref[...] = x + _rms(_dot(o, wo_ref[...]), gpost_ref[...])


def _xattn_prompt(x, mk, mv, p, nb, seq, mem_len, tm):
    ns = seq // tm
    const = lambda b, s: (0, 0)
    return pl.pallas_call(
        _xattn_prompt_kernel,
        grid=(nb, ns),
        in_specs=[pl.BlockSpec((tm, D_MODEL), lambda b, s: (b * ns + s, 0)),
                  pl.BlockSpec((mem_len, X_H * X_HD), lambda b, s: (b, 0)),
                  pl.BlockSpec((mem_len, X_H * X_HD), lambda b, s: (b, 0)),
                  pl.BlockSpec((1, D_MODEL), const),
                  pl.BlockSpec((D_MODEL, X_H * X_HD), const),
                  pl.BlockSpec((X_H * X_HD, D_MODEL), const),
                  pl.BlockSpec((1, D_MODEL), const)],
        out_specs=pl.BlockSpec((tm, D_MODEL), lambda b, s: (b * ns + s, 0)),
        out_shape=jax.ShapeDtypeStruct((nb * seq, D_MODEL), F32),
        compiler_params=_cparams(2),
        name="xattn_prompt",
    )(x, mk, mv, p["norm_x_pre"], p["w_xq"], p["w_xo"], p["norm_x_post"])


def _ffn_finish(kf, part, x_ref, gpost_ref, xo_ref, acc_scr):
    @pl.when(kf == 0)
    def _():
        acc_scr[...] = part

    @pl.when(kf > 0)
    def _():
        acc_scr[...] = acc_scr[...] + part

    @pl.when(kf == pl.num_programs(2) - 1)
    def _():
        xo_ref[...] = x_ref[...] + _rms(acc_scr[...], gpost_ref[...])


def _ffn_prompt_kernel(x_ref, gpre_ref, wa_ref, wg_ref, cw_ref, cb_ref, wo_ref, gpost_ref,
                       xo_ref, bufo0_ref, bufo1_ref, h_scr, acc_scr, a_scr):
    s = pl.program_id(1)
    kf = pl.program_id(2)
    tm = x_ref.shape[0]
    pad = SUBLANES

    @pl.when(kf == 0)
    def _():
        h_scr[...] = _rms(x_ref[...], gpre_ref[...]).astype(BF16)

    @pl.when(s == 0)
    def _():
        a_scr[kf, 0:pad, :] = jnp.zeros((pad, FF_CHUNK), F32)

    h = h_scr[...]
    a = _dot(h, wa_ref[...])
    gate = _dot(h, wg_ref[...])
    a_scr[kf, pad:pad + tm, :] = a
    cw = cw_ref[...]
    ac = cb_ref[...] + a * cw[FFN_CONV_W - 1:FFN_CONV_W, :]
    for k in range(FFN_CONV_W - 1):
        off = pad - (FFN_CONV_W - 1) + k
        ac = ac + a_scr[kf, off:off + tm, :] * cw[k:k + 1, :]
    tail = a_scr[kf, tm:tm + pad, :]
    a_scr[kf, 0:pad, :] = tail

    @pl.when(kf == 0)
    def _():
        bufo0_ref[0] = tail

    @pl.when(kf == 1)
    def _():
        bufo1_ref[0] = tail

    part = _dot((_silu(ac) * gate).astype(BF16), wo_ref[...])
    _ffn_finish(kf, part, x_ref, gpost_ref, xo_ref, acc_scr)


def _ffn_prompt(x, p, nb, seq, tm):
    ns = seq // tm
    nk = D_FF // FF_CHUNK
    const = lambda b, s, k: (0, 0)
    return pl.pallas_call(
        _ffn_prompt_kernel,
        grid=(nb, ns, nk),
        in_specs=[pl.BlockSpec((tm, D_MODEL), lambda b, s, k: (b * ns + s, 0)),
                  pl.BlockSpec((1, D_MODEL), const),
                  pl.BlockSpec((D_MODEL, FF_CHUNK), lambda b, s, k: (0, k)),
                  pl.BlockSpec((D_MODEL, FF_CHUNK), lambda b, s, k: (0, nk + k)),
                  pl.BlockSpec((SUBLANES, FF_CHUNK), lambda b, s, k: (0, k)),
                  pl.BlockSpec((1, FF_CHUNK), lambda b, s, k: (0, k)),
                  pl.BlockSpec((FF_CHUNK, D_MODEL), lambda b, s, k: (k, 0)),
                  pl.BlockSpec((1, D_MODEL), const)],
        out_specs=[pl.BlockSpec((tm, D_MODEL), lambda b, s, k: (b * ns + s, 0)),
                   pl.BlockSpec((1, SUBLANES, FF_CHUNK), lambda b, s, k: (b, 0, 0)),
                   pl.BlockSpec((1, SUBLANES, FF_CHUNK), lambda b, s, k: (b, 0, 0))],
        out_shape=[jax.ShapeDtypeStruct((nb * seq, D_MODEL), F32),
                   jax.ShapeDtypeStruct((nb, SUBLANES, FF_CHUNK), F32),
                   jax.ShapeDtypeStruct((nb, SUBLANES, FF_CHUNK), F32)],
        scratch_shapes=[pltpu.VMEM((tm, D_MODEL), BF16),
                        pltpu.VMEM((tm, D_MODEL), F32),
                        pltpu.VMEM((nk, tm + SUBLANES, FF_CHUNK), F32)],
        compiler_params=_cparams(3, VMEM_LIMIT_BIG),
        name="ffn_prompt",
    )(x, p["norm_ffn_pre"], p["w_ffn_in"], p["w_ffn_in"], p["ffn_conv_w"], p["ffn_conv_b"],
      p["w_ffn_out"], p["norm_ffn_post"])


def _ffn_sample_kernel(x_ref, gpre_ref, wa_ref, wg_ref, cw_ref, cb_ref, b0_ref, b1_ref, wo_ref,
                       gpost_ref, xo_ref, ao_ref, h_scr, acc_scr):
    kf = pl.program_id(2)

    @pl.when(kf == 0)
    def _():
        h_scr[...] = _rms(x_ref[...], gpre_ref[...]).astype(BF16)

    h = h_scr[...]
    a = _dot(h, wa_ref[...])
    gate = _dot(h, wg_ref[...])
    ao_ref[...] = a
    cw = cw_ref[...]
    ac = cb_ref[...] + b0_ref[...] * cw[0:1, :] + b1_ref[...] * cw[1:2, :] + a * cw[2:3, :]
    part = _dot((_silu(ac) * gate).astype(BF16), wo_ref[...])
    _ffn_finish(kf, part, x_ref, gpost_ref, xo_ref, acc_scr)


def _ffn_sample(x, b0, b1, p):
    m = x.shape[0]
    nk = D_FF // FF_CHUNK
    const = lambda b, s, k: (0, 0)
    chunk = lambda b, s, k: (0, k)
    return pl.pallas_call(
        _ffn_sample_kernel,
        grid=(1, 1, nk),
        in_specs=[pl.BlockSpec((m, D_MODEL), const),
                  pl.BlockSpec((1, D_MODEL), const),
                  pl.BlockSpec((D_MODEL, FF_CHUNK), chunk),
                  pl.BlockSpec((D_MODEL, FF_CHUNK), lambda b, s, k: (0, nk + k)),
                  pl.BlockSpec((SUBLANES, FF_CHUNK), chunk),
                  pl.BlockSpec((1, FF_CHUNK), chunk),
                  pl.BlockSpec((m, FF_CHUNK), chunk),
                  pl.BlockSpec((m, FF_CHUNK), chunk),
                  pl.BlockSpec((FF_CHUNK, D_MODEL), lambda b, s, k: (k, 0)),
                  pl.BlockSpec((1, D_MODEL), const)],
        out_specs=[pl.BlockSpec((m, D_MODEL), const),
                   pl.BlockSpec((m, FF_CHUNK), chunk)],
        out_shape=[jax.ShapeDtypeStruct((m, D_MODEL), F32),
                   jax.ShapeDtypeStruct((m, D_FF), F32)],
        scratch_shapes=[pltpu.VMEM((m, D_MODEL), BF16),
                        pltpu.VMEM((m, D_MODEL), F32)],
        compiler_params=_cparams(3),
        name="ffn_sample",
    )(x, p["norm_ffn_pre"], p["w_ffn_in"], p["w_ffn_in"], p["ffn_conv_w"], p["ffn_conv_b"], b0, b1,
      p["w_ffn_out"], p["norm_ffn_post"])


def _dec_prep_kernel(xbc_ref, dtr_ref, b0_ref, b1_ref, b2_ref, cw_ref, cb_ref, dtb_ref, alog_ref,
                     dx_ref, e_ref, xdt_t_ref, dec_t_ref, bs_ref, cs_ref, xsd_ref):
    cw = cw_ref[...]
    xc = (cb_ref[...] + b0_ref[...] * cw[0:1, :] + b1_ref[...] * cw[1:2, :] + b2_ref[...] * cw[2:3, :]
          + xbc_ref[...].astype(F32) * cw[3:4, :])
    xc = _silu(xc)
    xs = xc[:, :D_SSM]
    bs_ref[...] = xc[:, D_SSM:D_SSM + G_SSM * N_SSM]
    cs_ref[...] = xc[:, D_SSM + G_SSM * N_SSM:]
    dt = _softplus(dtr_ref[...] + dtb_ref[...])
    dec = jnp.exp(dt * (-jnp.exp(alog_ref[...])))
    e = e_ref[...]
    xdt_t_ref[...] = (xs * _dot_exact_lhs(dt, e)).T
    dec_t_ref[...] = _dot_exact_lhs(dec, e).T
    xsd_ref[...] = xs * dx_ref[...]


def _dec_prep(z, dtr, b0, b1, b2, p):
    m = dtr.shape[0]
    const = lambda i: (0, 0)
    full = lambda a: pl.BlockSpec(a.shape, const)
    return pl.pallas_call(
        _dec_prep_kernel,
        grid=(1,),
        in_specs=[pl.BlockSpec((m, SSM_CONV_DIM), lambda i: (0, OFF_XBC // SSM_CONV_DIM)),
                  full(dtr), full(b0), full(b1), full(b2), full(p["ssm_conv_w"]), full(p["ssm_conv_b"]),
                  full(p["dt_bias"]), full(p["a_log"]), full(p["d_x"]), full(p["expand"])],
        out_specs=[pl.BlockSpec((D_SSM, m), const), pl.BlockSpec((D_SSM, m), const),
                   pl.BlockSpec((m, G_SSM * N_SSM), const), pl.BlockSpec((m, G_SSM * N_SSM), const),
                   pl.BlockSpec((m, D_SSM), const)],
        out_shape=[jax.ShapeDtypeStruct((D_SSM, m), F32), jax.ShapeDtypeStruct((D_SSM, m), F32),
                   jax.ShapeDtypeStruct((m, G_SSM * N_SSM), F32),
                   jax.ShapeDtypeStruct((m, G_SSM * N_SSM), F32),
                   jax.ShapeDtypeStruct((m, D_SSM), F32)],
        compiler_params=_cparams(1),
        name="dec_prep",
    )(z, dtr, b0, b1, b2, p["ssm_conv_w"], p["ssm_conv_b"], p["dt_bias"], p["a_log"], p["d_x"],
      p["expand"])


def _dec_state_kernel(st_ref, xdt_t_ref, dec_t_ref, bs_ref, cs_ref, *rest):
    so_ref, yt_ref = rest[-2:]
    b = pl.program_id(0)
    nb = xdt_t_ref.shape[1]

    @pl.when(b == 0)
    def _():
        yt_ref[...] = jnp.zeros(yt_ref.shape, F32)

    sel = lax.broadcasted_iota(jnp.int32, (D_SSM, nb), 1) == b
    xcol = jnp.sum(jnp.where(sel, xdt_t_ref[...], 0.0), axis=-1, keepdims=True)
    dcol = jnp.sum(jnp.where(sel, dec_t_ref[...], 0.0), axis=-1, keepdims=True)
    brow = bs_ref[pl.ds(b, 1), :]
    crow = cs_ref[pl.ds(b, 1), :]
    rows_per_group = D_SSM // G_SSM
    bfull = jnp.concatenate(
        [jnp.broadcast_to(brow[:, g * N_SSM:(g + 1) * N_SSM], (rows_per_group, N_SSM))
         for g in range(G_SSM)], axis=0)
    cfull = jnp.concatenate(
        [jnp.broadcast_to(crow[:, g * N_SSM:(g + 1) * N_SSM], (rows_per_group, N_SSM))
         for g in range(G_SSM)], axis=0)
    hn = st_ref[0] * dcol + xcol * bfull
    so_ref[0] = hn
    ycol = jnp.sum(hn * cfull, axis=-1, keepdims=True)
    yt_ref[...] = jnp.where(sel, ycol, yt_ref[...])


def _dec_state(state_all, layer, xdt_t, dec_t, bs, cs, carry):
    nb = xdt_t.shape[1]
    const = lambda b: (0, 0)
    slab = lambda b: (layer * nb + b, 0, 0)
    extra = [] if carry is None else [carry]
    return pl.pallas_call(
        _dec_state_kernel,
        grid=(nb,),
        in_specs=[pl.BlockSpec((1, D_SSM, N_SSM), slab),
                  pl.BlockSpec(xdt_t.shape, const), pl.BlockSpec(dec_t.shape, const),
                  pl.BlockSpec(bs.shape, const), pl.BlockSpec(cs.shape, const)]
                 + [pl.BlockSpec(memory_space=pl.ANY)] * len(extra),
        out_specs=[pl.BlockSpec((1, D_SSM, N_SSM), slab),
                   pl.BlockSpec((D_SSM, nb), const)],
        out_shape=[jax.ShapeDtypeStruct(state_all.shape, F32),
                   jax.ShapeDtypeStruct((D_SSM, nb), F32)],
        input_output_aliases={5: 0} if extra else {},
        compiler_params=_cparams(1),
        name="dec_state",
    )(state_all, xdt_t, dec_t, bs, cs, *extra)


def _dec_post_kernel(yt_ref, xsd_ref, zs_ref, nrm_ref, o_ref):
    y = yt_ref[...].T + xsd_ref[...]
    y = y * _silu(zs_ref[...].astype(F32))
    o_ref[...] = _rms(y, nrm_ref[...]).astype(o_ref.dtype)


def _dec_post(yt, xsd, z, p):
    m = xsd.shape[0]
    const = lambda i: (0, 0)
    return pl.pallas_call(
        _dec_post_kernel,
        grid=(1,),
        in_specs=[pl.BlockSpec(yt.shape, const), pl.BlockSpec(xsd.shape, const),
                  pl.BlockSpec((m, D_SSM), lambda i: (0, OFF_ZS // D_SSM)),
                  pl.BlockSpec((1, D_SSM), const)],
        out_specs=pl.BlockSpec((m, D_SSM), const),
        out_shape=jax.ShapeDtypeStruct((m, D_SSM), BF16),
        compiler_params=_cparams(1),
        name="dec_post",
    )(yt, xsd, z, p["ssm_norm"])


def _dec_swa_kernel(qe_ref, kn_ref, vn_ref, ck_ref, cv_ref, c_ref, s1_ref, s2_ref, sink_ref, *rest):
    nk_ref, nv_ref, oe_ref = rest[-3:]
    bt = ck_ref.shape[0]
    w = ck_ref.shape[1]
    c, s1, s2 = c_ref[0:1, :], s1_ref[0:1, :], s2_ref[0:1, :]
    kn = _rope(kn_ref[...].astype(F32), c, s1, s2)
    vn = vn_ref[...].astype(F32)
    last = lax.broadcasted_iota(jnp.int32, (w, KV_ATT * HD_ATT), 0) == w - 1
    sink = sink_ref[:, 0:1]
    for i in range(bt):
        nk = jnp.where(last, kn[i:i + 1, :], pltpu.roll(ck_ref[i], w - 1, 0))
        nv = jnp.where(last, vn[i:i + 1, :], pltpu.roll(cv_ref[i], w - 1, 0))
        nk_ref[i] = nk
        nv_ref[i] = nv
        qe = _rope(qe_ref[i].astype(F32), c, s1, s2) * (HD_ATT ** -0.5)
        s = _dot_nt(qe.astype(BF16), nk.astype(BF16))
        m = jnp.maximum(jnp.max(s, axis=-1, keepdims=True), sink)
        e = jnp.exp(s - m)
        den = jnp.sum(e, axis=-1, keepdims=True) + jnp.exp(sink - m)
        pb = (e * (1.0 / den)).astype(BF16)
        oe_ref[i] = _dot(pb, nv.astype(BF16)).astype(oe_ref.dtype)


def _dec_swa(qe, z, ck_all, cv_all, layer, p, bt, carry):
    nb = qe.shape[0]
    _, w, kvw = ck_all.shape
    const = lambda i: (0, 0)
    slab = lambda i: (layer * (nb // bt) + i, 0, 0)
    tab = pl.BlockSpec((SUBLANES, LANES), const)
    extra = [] if carry is None else list(carry)
    return pl.pallas_call(
        _dec_swa_kernel,
        grid=(nb // bt,),
        in_specs=[pl.BlockSpec((bt, H_ATT, kvw), lambda i: (i, 0, 0)),
                  pl.BlockSpec((bt, kvw), lambda i: (i, OFF_K // kvw)),
                  pl.BlockSpec((bt, kvw), lambda i: (i, OFF_V // kvw)),
                  pl.BlockSpec((bt, w, kvw), slab),
                  pl.BlockSpec((bt, w, kvw), slab),
                  tab, tab, tab,
                  pl.BlockSpec((H_ATT, LANES), const)]
                 + [pl.BlockSpec(memory_space=pl.ANY)] * len(extra),
        out_specs=[pl.BlockSpec((bt, w, kvw), slab),
                   pl.BlockSpec((bt, w, kvw), slab),
                   pl.BlockSpec((bt, H_ATT, kvw), lambda i: (i, 0, 0))],
        out_shape=[jax.ShapeDtypeStruct(ck_all.shape, F32), jax.ShapeDtypeStruct(cv_all.shape, F32),
                   jax.ShapeDtypeStruct((nb, H_ATT, kvw), BF16)],
        input_output_aliases={9: 0, 10: 1} if extra else {},
        compiler_params=_cparams(1),
        name="dec_swa",
    )(qe, z, z, ck_all, cv_all, p["rope_c_s"], p["rope_s1_s"], p["rope_s2_s"], p["sinks_x"], *extra)


def _dec_xattn_kernel(q_ref, k_ref, v_ref, o_ref):
    bt = k_ref.shape[0]
    rowi = lax.broadcasted_iota(jnp.int32, (SUBLANES, X_H * X_HD), 0)
    lane = lax.broadcasted_iota(jnp.int32, (SUBLANES, X_H * X_HD), 1)
    own = (lane // X_HD) == rowi
    qv = q_ref[...].astype(F32) * (X_HD ** -0.5)
    for i in range(bt):
        qe = jnp.where(own, jnp.broadcast_to(qv[i:i + 1, :], own.shape), 0.0).astype(BF16)
        s = _dot_nt(qe, k_ref[i].astype(BF16))
        m = jnp.max(s, axis=-1, keepdims=True)
        e = jnp.exp(s - m)
        pb = (e * (1.0 / jnp.sum(e, axis=-1, keepdims=True))).astype(BF16)
        oe = _dot(pb, v_ref[i].astype(BF16))
        o_ref[i:i + 1, :] = jnp.sum(jnp.where(own, oe, 0.0), axis=0, keepdims=True).astype(o_ref.dtype)


def _dec_xattn(qv, mk_all, mv_all, layer, bt):
    nb = qv.shape[0]
    _, mem_len, d = mk_all.shape
    slab = lambda i: (layer * (nb // bt) + i, 0, 0)
    return pl.pallas_call(
        _dec_xattn_kernel,
        grid=(nb // bt,),
        in_specs=[pl.BlockSpec((bt, d), lambda i: (i, 0)),
                  pl.BlockSpec((bt, mem_len, d), slab),
                  pl.BlockSpec((bt, mem_len, d), slab)],
        out_specs=pl.BlockSpec((bt, d), lambda i: (i, 0)),
        out_shape=jax.ShapeDtypeStruct((nb, d), F32),
        compiler_params=_cparams(1),
        name="dec_xattn",
    )(qv, mk_all, mv_all)


def _pad_rows(w, rows=SUBLANES):
    return jnp.pad(w, ((0, rows - w.shape[0]), (0, 0)))


def _pad_lanes(v, lanes=LANES):
    return jnp.pad(v, (0, lanes - v.shape[0])).reshape(1, lanes)


def _rope_tables(pos):
    half = ROT_DIM // 2
    inv = ROPE_THETA ** (-jnp.arange(half, dtype=F32) / half)
    ang = pos.astype(F32)[:, None] * inv[None, :]
    cos, sin = jnp.cos(ang), jnp.sin(ang)
    n = pos.shape[0]
    ones = jnp.ones((n, HD_ATT - ROT_DIM), F32)
    zeros = jnp.zeros((n, HD_ATT - ROT_DIM), F32)
    zh = jnp.zeros((n, half), F32)
    c = jnp.concatenate([cos, cos, ones], axis=1)
    s1 = jnp.concatenate([-sin, zh, zeros], axis=1)
    s2 = jnp.concatenate([zh, sin, zeros], axis=1)
    rep = LANES // HD_ATT
    return tuple(jnp.tile(t, (1, rep)) for t in (c, s1, s2))


def _prep_layer(l, prm, seq):
    w_in = prm["w_in"][l]
    o = np.cumsum([0, D_A, D_A, D_A, D_SSM, SSM_CONV_DIM, H_SSM, H_ATT * HD_ATT, KV_ATT * HD_ATT,
                   KV_ATT * HD_ATT, 3 * D_MODEL])
    col = lambda i: w_in[:, o[i]:o[i + 1]]
    w_main = jnp.concatenate([col(4), col(3), col(0), col(1), col(2), col(9), col(6), col(7), col(8)],
                             axis=1).astype(BF16)
    w_dt = jnp.pad(col(5), ((0, 0), (0, LANES - H_SSM))).astype(BF16)
    kvw = KV_ATT * HD_ATT
    wq = col(6).reshape(D_MODEL, H_ATT, HD_ATT)
    grp = np.arange(H_ATT) // (H_ATT // KV_ATT)
    place = jnp.asarray(np.eye(KV_ATT, dtype=np.float32)[grp])
    wq_exp = (wq[:, :, None, :] * place[None, :, :, None]).reshape(D_MODEL, H_ATT * kvw).astype(BF16)
    wc = prm["w_attn_out"][l].reshape(H_ATT, HD_ATT, D_MODEL)
    wc_exp = (wc[:, None, :, :] * place[:, :, None, None]).reshape(H_ATT * kvw, D_MODEL).astype(BF16)
    expand = jnp.asarray(np.kron(np.eye(LANES, H_SSM, dtype=np.float32),
                                 np.ones((1, SSM_HEADDIM), np.float32))).astype(BF16)
    rope_p = _rope_tables(jnp.arange(seq, dtype=jnp.int32))
    rope_s = _rope_tables(jnp.full((SUBLANES,), PAST_LEN, jnp.int32))
    return {
        "norm_mix_pre": prm["norm_mix_pre"][l].reshape(1, -1),
        "norm_mix_post": prm["norm_mix_post"][l].reshape(1, -1),
        "w_main": w_main, "w_dt": w_dt, "wq_exp": wq_exp,
        "conv_a_w": _pad_rows(prm["conv_a_w"][l]),
        "w_a_out": prm["w_a_out"][l].astype(BF16),
        "ssm_conv_w": _pad_rows(prm["ssm_conv_w"][l]),
        "ssm_conv_b": prm["ssm_conv_b"][l].reshape(1, -1),
        "dt_bias": _pad_lanes(prm["ssm_dt_bias"][l]),
        "a_log": _pad_lanes(prm["ssm_a_log"][l]),
        "d_x": jnp.repeat(prm["ssm_d"][l], SSM_HEADDIM).reshape(1, -1),
        "ssm_norm": prm["ssm_norm"][l].reshape(1, -1),
        "expand": expand,
        "w_ssm_out": prm["w_ssm_out"][l].astype(BF16),
        "sinks": prm["attn_sinks"][l],
        "sinks_x": jnp.broadcast_to(prm["attn_sinks"][l][:, None], (H_ATT, LANES)),
        "w_attn_out": prm["w_attn_out"][l].astype(BF16),
        "w_attn_out_exp": wc_exp,
        "w_out": prm["w_out"][l].astype(BF16),
        "rope_c": rope_p[0], "rope_s1": rope_p[1], "rope_s2": rope_p[2],
        "rope_c_s": rope_s[0], "rope_s1_s": rope_s[1], "rope_s2_s": rope_s[2],
        "norm_x_pre": prm["norm_x_pre"][l].reshape(1, -1),
        "norm_x_post": prm["norm_x_post"][l].reshape(1, -1),
        "norm_mem": prm["norm_mem"][l].reshape(1, -1),
        "w_xq": prm["w_xq"][l].astype(BF16),
        "w_xk": prm["w_xk"][l].astype(BF16),
        "w_xv": prm["w_xv"][l].astype(BF16),
        "w_xo": prm["w_xo"][l].astype(BF16),
        "norm_ffn_pre": prm["norm_ffn_pre"][l].reshape(1, -1),
        "norm_ffn_post": prm["norm_ffn_post"][l].reshape(1, -1),
        "w_ffn_in": prm["w_ffn_in"][l].astype(BF16),
        "ffn_conv_w": _pad_rows(prm["ffn_conv_w"][l]),
        "ffn_conv_b": prm["ffn_conv_b"][l].reshape(1, -1),
        "w_ffn_out": prm["w_ffn_out"][l].astype(BF16),
    }


def _pick_tile(n, pref):
    t = min(n, pref)
    while n % t:
        t //= 2
    return t


def _prompt_layer(x, mem, p, nb, seq, mem_len):
    rows = nb * seq
    z, dtr = _in_proj(x, p["norm_mix_pre"], p["w_main"], p["w_dt"], _pick_tile(rows, 1024), 1536)
    ys, h_last, cbuf = _ssd_prompt(z, dtr, p, nb, seq)
    oc, k_new, v_new = _swa_prompt(z, p, nb, seq)
    x, bufa = _tail_prompt(z, ys, oc, x, p, nb, seq, _pick_tile(seq, 256))
    tmem = _pick_tile(mem.shape[0], 512)
    mk = _norm_matmul(mem, p["norm_mem"], p["w_xk"], F32, tmem, X_H * X_HD)
    mv = _norm_matmul(mem, p["norm_mem"], p["w_xv"], F32, tmem, X_H * X_HD)
    x = _xattn_prompt(x, mk, mv, p, nb, seq, mem_len, _pick_tile(seq, 512))
    x, buff0, buff1 = _ffn_prompt(x, p, nb, seq, _pick_tile(seq, 512))
    buff = jnp.concatenate([buff0, buff1], axis=-1)
    state = (bufa[:, SUBLANES - (CONV_A_W - 1):],
             cbuf[:, SUBLANES - (SSM_CONV_W - 1):],
             h_last.reshape(nb, H_SSM, SSM_HEADDIM, N_SSM),
             k_new.reshape(nb, WINDOW, KV_ATT, HD_ATT),
             v_new.reshape(nb, WINDOW, KV_ATT, HD_ATT),
             buff[:, SUBLANES - (FFN_CONV_W - 1):],
             mk.reshape(nb, mem_len, X_H, X_HD),
             mv.reshape(nb, mem_len, X_H, X_HD))
    return x, state


def _sample_layer(x, layer, st, big, carry, p):
    buf_a, buf_ssm, buf_ffn = st
    ssm_all, swa_k_all, swa_v_all, mem_k_all, mem_v_all = big
    nb = x.shape[0]
    kvw = KV_ATT * HD_ATT
    z, dtr = _in_proj(x, p["norm_mix_pre"], p["w_main"], p["w_dt"], nb, 1536)
    qe = _norm_matmul(x, p["norm_mix_pre"], p["wq_exp"], BF16, nb, 1024).reshape(nb, H_ATT, kvw)
    xdt_t, dec_t, bs, cs, xsd = _dec_prep(z, dtr, buf_ssm[:, 0], buf_ssm[:, 1], buf_ssm[:, 2], p)
    new_ssm, yt = _dec_state(ssm_all, layer, xdt_t, dec_t, bs, cs, None if carry is None else carry[0])
    ys = _dec_post(yt, xsd, z, p)
    new_k, new_v, oe = _dec_swa(qe, z, swa_k_all, swa_v_all, layer, p, _pick_tile(nb, 16),
                                None if carry is None else carry[1:3])
    x, cv = _tail_sample(z, ys, oe.reshape(nb, H_ATT * kvw), x, buf_a[:, 0], buf_a[:, 1], p)
    qx = _norm_matmul(x, p["norm_x_pre"], p["w_xq"], F32, nb, 1024)
    ox = _dec_xattn(qx, mem_k_all, mem_v_all, layer, _pick_tile(nb, 8))
    x = _mm_norm_res(ox, p["w_xo"], p["norm_x_post"], x)
    x, a_up = _ffn_sample(x, buf_ffn[:, 0], buf_ffn[:, 1], p)
    x_raw = z[:, OFF_XBC:OFF_XBC + SSM_CONV_DIM].astype(F32)
    small = (jnp.stack([buf_a[:, 1], cv], axis=1),
             jnp.concatenate([buf_ssm[:, 1:], x_raw[:, None, :]], axis=1),
             jnp.stack([buf_ffn[:, 1], a_up], axis=1))
    return x, small, (new_ssm, new_k, new_v)


def kernel(x_prompt, x_sample, mem_prompt, state_conv_a, state_ssm_conv, state_ssm, cache_swa_k, cache_swa_v, cache_mem_k, cache_mem_v, state_ffn_conv, norm_mix_pre, norm_mix_post, w_in, conv_a_w, w_a_out, ssm_conv_w, ssm_conv_b, ssm_dt_bias, ssm_a_log, ssm_d, ssm_norm, w_ssm_out, attn_sinks, w_attn_out, w_out, norm_x_pre, norm_x_post, norm_mem, w_xq, w_xk, w_xv, w_xo, norm_ffn_pre, norm_ffn_post, w_ffn_in, ffn_conv_w, ffn_conv_b, w_ffn_out):
    prm = dict(norm_mix_pre=norm_mix_pre, norm_mix_post=norm_mix_post, w_in=w_in, conv_a_w=conv_a_w,
               w_a_out=w_a_out, ssm_conv_w=ssm_conv_w, ssm_conv_b=ssm_conv_b, ssm_dt_bias=ssm_dt_bias,
               ssm_a_log=ssm_a_log, ssm_d=ssm_d, ssm_norm=ssm_norm, w_ssm_out=w_ssm_out,
               attn_sinks=attn_sinks, w_attn_out=w_attn_out, w_out=w_out, norm_x_pre=norm_x_pre,
               norm_x_post=norm_x_post, norm_mem=norm_mem, w_xq=w_xq, w_xk=w_xk, w_xv=w_xv, w_xo=w_xo,
               norm_ffn_pre=norm_ffn_pre, norm_ffn_post=norm_ffn_post, w_ffn_in=w_ffn_in,
               ffn_conv_w=ffn_conv_w, ffn_conv_b=ffn_conv_b, w_ffn_out=w_ffn_out)
    nb, seq, d = x_prompt.shape
    ns = x_sample.shape[0]
    mem_len = mem_prompt.shape[1]
    depth = w_in.shape[0]
    assert x_sample.shape[1] == 1 and seq % WINDOW == 0 and seq % SSD_CHUNK == 0
    assert cache_swa_k.shape[2] == WINDOW and PAST_LEN >= WINDOW

    yp = x_prompt.reshape(nb * seq, d)
    ys = x_sample.reshape(ns, d)
    mem = mem_prompt.reshape(nb * mem_len, d)
    flat = lambda a, *tail: a.reshape((depth * ns,) + tail)
    kvw = KV_ATT * HD_ATT
    big = (flat(state_ssm, D_SSM, N_SSM),
           flat(cache_swa_k, cache_swa_k.shape[2], kvw), flat(cache_swa_v, cache_swa_v.shape[2], kvw),
           flat(cache_mem_k, cache_mem_k.shape[2], X_H * X_HD),
           flat(cache_mem_v, cache_mem_v.shape[2], X_H * X_HD))
    new_p, new_s, carry = [], [], None
    for l in range(depth):
        p = _prep_layer(l, prm, seq)
        yp, st_p = _prompt_layer(yp, mem, p, nb, seq, mem_len)
        new_p.append(st_p)
        ys, st_s, carry = _sample_layer(ys, l, (state_conv_a[l], state_ssm_conv[l], state_ffn_conv[l]),
                                        big, carry, p)
        new_s.append(st_s)
    stack = lambda lst, i: jnp.stack([s[i] for s in lst])
    s_ssm, s_swa_k, s_swa_v = (carry[0].reshape(state_ssm.shape), carry[1].reshape(cache_swa_k.shape),
                               carry[2].reshape(cache_swa_v.shape))
    return ((yp.reshape(nb, seq, d), ys.reshape(ns, 1, d))
            + tuple(stack(new_p, i) for i in range(8))
            + (stack(new_s, 0), stack(new_s, 1), s_ssm, s_swa_k, s_swa_v, stack(new_s, 2)))
```

```python
import functools

import numpy as np
import jax
import jax.numpy as jnp
from jax import lax
from jax.experimental import pallas as pl
from jax.experimental.pallas import tpu as pltpu

F32 = jnp.float32
BF16 = jnp.bfloat16

EPS = 1e-6
D_MODEL = 1024
D_A = D_MODEL
CONV_A_W = 3
D_SSM = 2 * D_MODEL
SSM_HEADDIM = 64
H_SSM = D_SSM // SSM_HEADDIM
G_SSM = 8
N_SSM = 128
SSM_CONV_W = 4
SSM_CONV_DIM = D_SSM + 2 * G_SSM * N_SSM
SSD_CHUNK = 128
H_ATT = 16
KV_ATT = 4
HD_ATT = 64
ROT_DIM = HD_ATT // 4
ROPE_THETA = 500000.0
WINDOW = 128
X_H = 4
X_HD = D_MODEL // X_H
D_FF = 2816
FFN_CONV_W = 3
PAST_LEN = 8192
LOG2E = 1.4426950408889634

LANES = 128
SUBLANES = 8
VMEM_LIMIT = 48 * 1024 * 1024
VMEM_LIMIT_BIG = 56 * 1024 * 1024

OFF_XBC = 0
OFF_ZS = OFF_XBC + SSM_CONV_DIM
OFF_ZA = OFF_ZS + D_SSM
OFF_GATES = OFF_ZA + 3 * D_A
OFF_Q = OFF_GATES + 3 * D_MODEL
OFF_K = OFF_Q + H_ATT * HD_ATT
OFF_V = OFF_K + KV_ATT * HD_ATT
N_MAIN = OFF_V + KV_ATT * HD_ATT
FF_CHUNK = D_FF // 2
FFN_SUB = 4 * LANES


def _cparams(n_axes, vmem=VMEM_LIMIT):
    return pltpu.CompilerParams(dimension_semantics=("arbitrary",) * n_axes,
                                vmem_limit_bytes=vmem)


def _resident(shape, index_map):
    return pl.BlockSpec(shape, index_map, pipeline_mode=pl.Buffered(1))


def _rms(x, g):
    return x * lax.rsqrt(jnp.mean(x * x, axis=-1, keepdims=True) + EPS) * g


def _sigmoid(x):
    return 0.5 + 0.5 * jnp.tanh(0.5 * x)


def _silu(x):
    hx = 0.5 * x
    return hx + hx * jnp.tanh(hx)


def _softplus(x):
    return jnp.maximum(x, 0.0) + jnp.log(1.0 + jnp.exp(-jnp.abs(x)))


def _dot(a, b):
    return jnp.dot(a, b, preferred_element_type=F32)


def _dot_nt(a, b):
    return lax.dot_general(a, b, (((1,), (1,)), ((), ())), preferred_element_type=F32)


def _dot_exact_rhs(a01, v):
    hi = v.astype(BF16)
    r1 = v - hi.astype(F32)
    mid = r1.astype(BF16)
    lo = (r1 - mid.astype(F32)).astype(BF16)
    return _dot(a01, hi) + _dot(a01, mid) + _dot(a01, lo)


def _dot_exact_lhs(v, b01):
    hi = v.astype(BF16)
    r1 = v - hi.astype(F32)
    mid = r1.astype(BF16)
    lo = (r1 - mid.astype(F32)).astype(BF16)
    return _dot(hi, b01) + _dot(mid, b01) + _dot(lo, b01)


def _rope(x, c, s1, s2):
    outs = []
    for i in range(x.shape[1] // LANES):
        xi = x[:, i * LANES:(i + 1) * LANES]
        outs.append(xi * c + pltpu.roll(xi, LANES - ROT_DIM // 2, 1) * s1
                    + pltpu.roll(xi, ROT_DIM // 2, 1) * s2)
    return outs[0] if len(outs) == 1 else jnp.concatenate(outs, axis=1)


def _norm_mm_kernel(x_ref, g_ref, w_ref, o_ref, h_scr):
    @pl.when(pl.program_id(1) == 0)
    def _():
        h_scr[...] = _rms(x_ref[...], g_ref[...]).astype(BF16)

    o_ref[...] = _dot(h_scr[...], w_ref[...]).astype(o_ref.dtype)


def _norm_matmul(x, g, w, out_dtype, tm, tn):
    m, k = x.shape
    n = w.shape[1]
    return pl.pallas_call(
        _norm_mm_kernel,
        grid=(m // tm, n // tn),
        in_specs=[pl.BlockSpec((tm, k), lambda i, j: (i, 0)),
                  pl.BlockSpec((1, k), lambda i, j: (0, 0)),
                  pl.BlockSpec((k, tn), lambda i, j: (0, j))],
        out_specs=pl.BlockSpec((tm, tn), lambda i, j: (i, j)),
        out_shape=jax.ShapeDtypeStruct((m, n), out_dtype),
        scratch_shapes=[pltpu.VMEM((tm, k), BF16)],
        compiler_params=_cparams(2),
        name="norm_matmul",
    )(x, g, w)


def _inproj_kernel(x_ref, g_ref, w_ref, wdt_ref, z_ref, dt_ref, h_scr):
    @pl.when(pl.program_id(1) == 0)
    def _():
        h = _rms(x_ref[...], g_ref[...]).astype(BF16)
        h_scr[...] = h
        dt_ref[...] = _dot(h, wdt_ref[...])

    z_ref[...] = _dot(h_scr[...], w_ref[...]).astype(z_ref.dtype)


def _in_proj(x, g, w, wdt, tm, tn):
    m, k = x.shape
    n = w.shape[1]
    return pl.pallas_call(
        _inproj_kernel,
        grid=(m // tm, n // tn),
        in_specs=[pl.BlockSpec((tm, k), lambda i, j: (i, 0)),
                  pl.BlockSpec((1, k), lambda i, j: (0, 0)),
                  pl.BlockSpec((k, tn), lambda i, j: (0, j)),
                  pl.BlockSpec((k, LANES), lambda i, j: (0, 0))],
        out_specs=[pl.BlockSpec((tm, tn), lambda i, j: (i, j)),
                   pl.BlockSpec((tm, LANES), lambda i, j: (i, 0))],
        out_shape=[jax.ShapeDtypeStruct((m, n), BF16),
                   jax.ShapeDtypeStruct((m, LANES), F32)],
        scratch_shapes=[pltpu.VMEM((tm, k), BF16)],
        compiler_params=_cparams(2),
        name="in_proj",
    )(x, g, w, wdt)


def _mm_norm_res_kernel(a_ref, w_ref, g_ref, x_ref, o_ref):
    y = _dot(a_ref[...].astype(BF16), w_ref[...])
    o_ref[...] = x_ref[...] + _rms(y, g_ref[...])


def _mm_norm_res(a, w, g, x):
    m, k = a.shape
    n = w.shape[1]
    return pl.pallas_call(
        _mm_norm_res_kernel,
        grid=(1,),
        in_specs=[pl.BlockSpec((m, k), lambda i: (0, 0)),
                  pl.BlockSpec((k, n), lambda i: (0, 0)),
                  pl.BlockSpec((1, n), lambda i: (0, 0)),
                  pl.BlockSpec((m, n), lambda i: (0, 0))],
        out_specs=pl.BlockSpec((m, n), lambda i: (0, 0)),
        out_shape=jax.ShapeDtypeStruct((m, n), F32),
        compiler_params=_cparams(1),
        name="mm_norm_res",
    )(a, w, g, x)


def _ssd_kernel(xbc_ref, zs_ref, dtr_ref, cw_ref, cb_ref, dtb_ref, alog_ref, dx_ref, nrm_ref, e_ref,
                y_ref, hout_ref, cbuf_ref, x_scr, ht_scr):
    c = pl.program_id(1)
    q = xbc_ref.shape[0]
    pad = SUBLANES

    halo = x_scr.shape[0]

    @pl.when(c == 0)
    def _():
        x_scr[...] = jnp.zeros(x_scr.shape, BF16)
        ht_scr[...] = jnp.zeros(ht_scr.shape, F32)

    xb = xbc_ref[...]
    xin = xb.astype(F32)
    ext = jnp.concatenate([x_scr[...], xb], axis=0)
    n_sh = SSM_CONV_W - 1
    ri = lax.broadcasted_iota(jnp.int32, (n_sh * q, halo + q), 0)
    ci = lax.broadcasted_iota(jnp.int32, (n_sh * q, halo + q), 1)
    shift_mat = jnp.where(ci == (ri % q) + halo - n_sh + ri // q, 1.0, 0.0).astype(BF16)
    shifted = _dot(shift_mat, ext)
    cw = cw_ref[...]
    xc = cb_ref[...] + xin * cw[n_sh:n_sh + 1, :]
    for k in range(n_sh):
        xc = xc + shifted[k * q:(k + 1) * q, :] * cw[k:k + 1, :]
    xc = _silu(xc)
    x_scr[...] = xb[q - halo:q, :]
    cbuf_ref[0] = xin[q - pad:q, :]

    xs = xc[:, :D_SSM]
    bs = xc[:, D_SSM:D_SSM + G_SSM * N_SSM]
    cs_in = xc[:, D_SSM + G_SSM * N_SSM:]

    dt = _softplus(dtr_ref[...] + dtb_ref[...])
    a = dt * (-jnp.exp(alog_ref[...]))
    row = lax.broadcasted_iota(jnp.int32, (q, q), 0)
    col = lax.broadcasted_iota(jnp.int32, (q, q), 1)
    tri = row >= col
    cs = _dot_exact_rhs(jnp.where(tri, 1.0, 0.0).astype(BF16), a)
    cs_last = cs[q - 1:q, :]
    cs2 = cs * LOG2E
    src_t = (cs2 - jnp.log2(dt)).T
    wdd = dt * jnp.exp(cs_last - cs)
    cdec = jnp.broadcast_to(jnp.exp(cs_last), (SUBLANES, LANES))
    cdec_x = _dot_exact_lhs(cdec, e_ref[...])[0:1, :]

    lane = lax.broadcasted_iota(jnp.int32, (q, LANES), 1)
    heads_per_group = H_SSM // G_SSM
    y_tiles = []
    for g in range(G_SSM):
        bg = bs[:, g * N_SSM:(g + 1) * N_SSM]
        cg = cs_in[:, g * N_SSM:(g + 1) * N_SSM]
        cb = _dot_nt(cg.astype(BF16), bg.astype(BF16))
        xdd_tiles = []
        for pr in range(heads_per_group // 2):
            hp = g * (heads_per_group // 2) + pr
            xpair = xs[:, hp * LANES:(hp + 1) * LANES]
            wp = jnp.concatenate([xpair, ht_scr[:, hp * LANES:(hp + 1) * LANES]], axis=0).astype(BF16)
            parts = []
            wsel = []
            for r2 in range(2):
                h = 2 * hp + r2
                csb = jnp.broadcast_to(cs2[:, h:h + 1], (q, LANES))
                mmat = cb * jnp.exp2(jnp.where(tri, csb - src_t[h:h + 1, :], -jnp.inf))
                csc = cg * jnp.exp2(csb)
                lhs = jnp.concatenate([mmat, csc], axis=1).astype(BF16)
                parts.append(_dot(lhs, wp))
                wsel.append(jnp.broadcast_to(wdd[:, h:h + 1], (q, LANES)))
            y_tiles.append(jnp.where(lane < SSM_HEADDIM, parts[0], parts[1]))
            xdd_tiles.append(xpair * jnp.where(lane < SSM_HEADDIM, wsel[0], wsel[1]))
        xdd = jnp.concatenate(xdd_tiles, axis=1).astype(BF16)
        st_t = _dot(bg.T.astype(BF16), xdd)
        lo, hi = g * heads_per_group * SSM_HEADDIM, (g + 1) * heads_per_group * SSM_HEADDIM
        ht_scr[:, lo:hi] = ht_scr[:, lo:hi] * cdec_x[:, lo:hi] + st_t

    y = jnp.concatenate(y_tiles, axis=1) + xs * dx_ref[...]
    y = y * _silu(zs_ref[...].astype(F32))
    y_ref[...] = _rms(y, nrm_ref[...]).astype(y_ref.dtype)

    @pl.when(c == pl.num_programs(1) - 1)
    def _():
        hout_ref[0] = ht_scr[...].T


def _ssd_prompt(z, dtr, p, nb, seq):
    q = SSD_CHUNK
    nc = seq // q
    row = lambda b, c: b * nc + c
    const = lambda b, c: (0, 0)
    return pl.pallas_call(
        _ssd_kernel,
        grid=(nb, nc),
        in_specs=[pl.BlockSpec((q, SSM_CONV_DIM), lambda b, c: (row(b, c), OFF_XBC // SSM_CONV_DIM)),
                  pl.BlockSpec((q, D_SSM), lambda b, c: (row(b, c), OFF_ZS // D_SSM)),
                  pl.BlockSpec((q, LANES), lambda b, c: (row(b, c), 0)),
                  pl.BlockSpec((SUBLANES, SSM_CONV_DIM), const),
                  pl.BlockSpec((1, SSM_CONV_DIM), const),
                  pl.BlockSpec((1, LANES), const),
                  pl.BlockSpec((1, LANES), const),
                  pl.BlockSpec((1, D_SSM), const),
                  pl.BlockSpec((1, D_SSM), const),
                  pl.BlockSpec((LANES, D_SSM), const)],
        out_specs=[pl.BlockSpec((q, D_SSM), lambda b, c: (row(b, c), 0)),
                   pl.BlockSpec((1, D_SSM, N_SSM), lambda b, c: (b, 0, 0)),
                   pl.BlockSpec((1, SUBLANES, SSM_CONV_DIM), lambda b, c: (b, 0, 0))],
        out_shape=[jax.ShapeDtypeStruct((nb * seq, D_SSM), BF16),
                   jax.ShapeDtypeStruct((nb, D_SSM, N_SSM), F32),
                   jax.ShapeDtypeStruct((nb, SUBLANES, SSM_CONV_DIM), F32)],
        scratch_shapes=[pltpu.VMEM((2 * SUBLANES, SSM_CONV_DIM), BF16),
                        pltpu.VMEM((N_SSM, D_SSM), F32)],
        compiler_params=_cparams(2),
        name="ssd_prompt",
    )(z, z, dtr, p["ssm_conv_w"], p["ssm_conv_b"], p["dt_bias"], p["a_log"], p["d_x"], p["ssm_norm"],
      p["expand"])


def _swa_kernel(sink_ref, q_ref, k_ref, v_ref, c_ref, s1_ref, s2_ref, o_ref, kn_ref, vn_ref,
                kp_scr, vp_scr):
    qb = pl.program_id(1)
    w = WINDOW
    kvw = KV_ATT * HD_ATT

    @pl.when(qb == 0)
    def _():
        kp_scr[...] = jnp.zeros((w, kvw), F32)
        vp_scr[...] = jnp.zeros((w, kvw), F32)

    c, s1, s2 = c_ref[...], s1_ref[...], s2_ref[...]
    qv = _rope(q_ref[...].astype(F32), c, s1, s2) * (HD_ATT ** -0.5)
    ko = _rope(k_ref[...].astype(F32), c, s1, s2)
    vo = v_ref[...].astype(F32)
    kn_ref[0] = ko
    vn_ref[0] = vo
    kcat = jnp.concatenate([kp_scr[...], ko], axis=0)
    vcat = jnp.concatenate([vp_scr[...], vo], axis=0)
    kp_scr[...] = ko
    vp_scr[...] = vo
    vcat_t = vcat.T.astype(BF16)

    rep = H_ATT // KV_ATT
    keyj = lax.broadcasted_iota(jnp.int32, (2 * w, w), 0)
    qryl = lax.broadcasted_iota(jnp.int32, (2 * w, w), 1)
    valid = (keyj > qryl) & (keyj <= qryl + w) & ((qb > 0) | (keyj >= w))
    valid = jnp.concatenate([valid] * rep, axis=1)
    lane_kv = lax.broadcasted_iota(jnp.int32, (2 * w, LANES), 1)
    lane_q = lax.broadcasted_iota(jnp.int32, (w, LANES), 1)
    lo_kv, lo_q = lane_kv < HD_ATT, lane_q < HD_ATT

    o_t = []
    for slab in range(kvw // LANES):
        a_k = kcat[:, slab * LANES:(slab + 1) * LANES]
        b_k = pltpu.roll(a_k, HD_ATT, 1)
        for gi in range(2):
            g = 2 * slab + gi
            kdup = jnp.where(lo_kv, a_k, b_k) if gi == 0 else jnp.where(lo_kv, b_k, a_k)
            q_tiles, sink_tiles = [], []
            for pr in range(rep // 2):
                qp = qv[:, (rep // 2 * g + pr) * LANES:(rep // 2 * g + pr + 1) * LANES]
                q_tiles.append(jnp.where(lo_q, qp, 0.0))
                q_tiles.append(jnp.where(lo_q, 0.0, qp))
                for r2 in range(2):
                    sink_tiles.append(jnp.full((1, w), sink_ref[rep * g + 2 * pr + r2], F32))
            qs = jnp.concatenate(q_tiles, axis=0).astype(BF16)
            sink = jnp.concatenate(sink_tiles, axis=1)
            s = _dot_nt(kdup.astype(BF16), qs)
            s = jnp.where(valid, s, -jnp.inf)
            m = jnp.maximum(jnp.max(s, axis=0, keepdims=True), sink)
            e = jnp.exp(s - m)
            den = jnp.sum(e, axis=0, keepdims=True) + jnp.exp(sink - m)
            o_g = _dot(vcat_t[g * HD_ATT:(g + 1) * HD_ATT, :], e.astype(BF16))
            o_g = o_g * (1.0 / den)
            for r in range(rep):
                o_t.append(o_g[:, r * w:(r + 1) * w])
    o_ref[...] = jnp.concatenate(o_t, axis=0).T.astype(o_ref.dtype)


def _swa_prompt(z, p, nb, seq):
    w = WINDOW
    nq = seq // w
    kvw = KV_ATT * HD_ATT
    row = lambda b, i: b * nq + i
    tab = pl.BlockSpec((w, LANES), lambda b, i: (i, 0))
    return pl.pallas_call(
        _swa_kernel,
        grid=(nb, nq),
        in_specs=[pl.BlockSpec(memory_space=pltpu.SMEM),
                  pl.BlockSpec((w, H_ATT * HD_ATT), lambda b, i: (row(b, i), OFF_Q // (H_ATT * HD_ATT))),
                  pl.BlockSpec((w, kvw), lambda b, i: (row(b, i), OFF_K // kvw)),
                  pl.BlockSpec((w, kvw), lambda b, i: (row(b, i), OFF_V // kvw)),
                  tab, tab, tab],
        out_specs=[pl.BlockSpec((w, H_ATT * HD_ATT), lambda b, i: (row(b, i), 0)),
                   pl.BlockSpec((1, w, kvw), lambda b, i: (b, 0, 0)),
                   pl.BlockSpec((1, w, kvw), lambda b, i: (b, 0, 0))],
        out_shape=[jax.ShapeDtypeStruct((nb * seq, H_ATT * HD_ATT), BF16),
                   jax.ShapeDtypeStruct((nb, w, kvw), F32),
                   jax.ShapeDtypeStruct((nb, w, kvw), F32)],
        scratch_shapes=[pltpu.VMEM((w, kvw), F32), pltpu.VMEM((w, kvw), F32)],
        compiler_params=_cparams(2),
        name="swa_prompt",
    )(p["sinks"], z, z, z, p["rope_c"], p["rope_s1"], p["rope_s2"])


def _tail_body(u, gb, ys, oc, ga, gbg, gcg, x, wa_ref, ws_ref, wc_ref, wo_ref, gpost_ref):
    ta = (gb * u).astype(BF16)
    out = None
    half = D_MODEL // 2
    for lo in range(0, D_MODEL, half):
        hi = lo + half
        y_a = _dot(ta, wa_ref[:, lo:hi])
        y_b = _dot(ys, ws_ref[:, lo:hi])
        y_c = _dot(oc, wc_ref[:, lo:hi])
        merged = (_sigmoid(ga[:, lo:hi]) * y_a + _sigmoid(gbg[:, lo:hi]) * y_b
                  + _sigmoid(gcg[:, lo:hi]) * y_c)
        d = _dot(merged.astype(BF16), wo_ref[lo:hi, :])
        out = d if out is None else out + d
    return x + _rms(out, gpost_ref[...])


def _tail_prompt_kernel(va_ref, gb_ref, gc_ref, ys_ref, oc_ref, ga_ref, gbg_ref, gcg_ref, x_ref,
                        cw_ref, wa_ref, ws_ref, wc_ref, wo_ref, gpost_ref, xo_ref, bufo_ref, cv_scr):
    s = pl.program_id(1)
    tm = x_ref.shape[0]
    pad = SUBLANES

    @pl.when(s == 0)
    def _():
        cv_scr[0:pad, :] = jnp.zeros((pad, D_A), F32)

    cv = gc_ref[...].astype(F32) * va_ref[...].astype(F32)
    cv_scr[pad:pad + tm, :] = cv
    cw = cw_ref[...]
    u = cv * cw[CONV_A_W - 1:CONV_A_W, :]
    for k in range(CONV_A_W - 1):
        off = pad - (CONV_A_W - 1) + k
        u = u + cv_scr[off:off + tm, :] * cw[k:k + 1, :]
    tail = cv_scr[tm:tm + pad, :]
    cv_scr[0:pad, :] = tail
    bufo_ref[0] = tail
    xo_ref[...] = _tail_body(u, gb_ref[...].astype(F32), ys_ref[...], oc_ref[...],
                             ga_ref[...].astype(F32), gbg_ref[...].astype(F32), gcg_ref[...].astype(F32),
                             x_ref[...], wa_ref, ws_ref, wc_ref, wo_ref, gpost_ref)


def _tail_prompt(z, ys, oc, x, p, nb, seq, tm):
    ns = seq // tm
    row = lambda b, s: b * ns + s
    const = lambda b, s: (0, 0)
    zblk = lambda off: pl.BlockSpec((tm, D_MODEL), lambda b, s: (row(b, s), off // D_MODEL))
    return pl.pallas_call(
        _tail_prompt_kernel,
        grid=(nb, ns),
        in_specs=[zblk(OFF_ZA), zblk(OFF_ZA + D_A), zblk(OFF_ZA + 2 * D_A),
                  pl.BlockSpec((tm, D_SSM), lambda b, s: (row(b, s), 0)),
                  pl.BlockSpec((tm, H_ATT * HD_ATT), lambda b, s: (row(b, s), 0)),
                  zblk(OFF_GATES), zblk(OFF_GATES + D_MODEL), zblk(OFF_GATES + 2 * D_MODEL),
                  pl.BlockSpec((tm, D_MODEL), lambda b, s: (row(b, s), 0)),
                  pl.BlockSpec((SUBLANES, D_A), const),
                  _resident((D_A, D_MODEL), const),
                  _resident((D_SSM, D_MODEL), const),
                  _resident((H_ATT * HD_ATT, D_MODEL), const),
                  _resident((D_MODEL, D_MODEL), const),
                  pl.BlockSpec((1, D_MODEL), const)],
        out_specs=[pl.BlockSpec((tm, D_MODEL), lambda b, s: (row(b, s), 0)),
                   pl.BlockSpec((1, SUBLANES, D_A), lambda b, s: (b, 0, 0))],
        out_shape=[jax.ShapeDtypeStruct((nb * seq, D_MODEL), F32),
                   jax.ShapeDtypeStruct((nb, SUBLANES, D_A), F32)],
        scratch_shapes=[pltpu.VMEM((tm + SUBLANES, D_A), F32)],
        compiler_params=_cparams(2, VMEM_LIMIT_BIG),
        name="tail_prompt",
    )(z, z, z, ys, oc, z, z, z, x, p["conv_a_w"], p["w_a_out"], p["w_ssm_out"], p["w_attn_out"],
      p["w_out"], p["norm_mix_post"])


def _tail_sample_kernel(va_ref, gb_ref, gc_ref, ys_ref, oc_ref, ga_ref, gbg_ref, gcg_ref, x_ref,
                        b0_ref, b1_ref, cw_ref, wa_ref, ws_ref, wc_ref, wo_ref, gpost_ref,
                        xo_ref, cvo_ref):
    cv = gc_ref[...].astype(F32) * va_ref[...].astype(F32)
    cw = cw_ref[...]
    u = b0_ref[...] * cw[0:1, :] + b1_ref[...] * cw[1:2, :] + cv * cw[2:3, :]
    cvo_ref[...] = cv
    xo_ref[...] = _tail_body(u, gb_ref[...].astype(F32), ys_ref[...], oc_ref[...],
                             ga_ref[...].astype(F32), gbg_ref[...].astype(F32), gcg_ref[...].astype(F32),
                             x_ref[...], wa_ref, ws_ref, wc_ref, wo_ref, gpost_ref)


def _tail_sample(z, ys, oc, x, b0, b1, p):
    m = x.shape[0]
    const = lambda i: (0, 0)
    zblk = lambda off: pl.BlockSpec((m, D_MODEL), lambda i: (0, off // D_MODEL))
    full = lambda a: pl.BlockSpec(a.shape, const)
    w_c = p["w_attn_out_exp"]
    return pl.pallas_call(
        _tail_sample_kernel,
        grid=(1,),
        in_specs=[zblk(OFF_ZA), zblk(OFF_ZA + D_A), zblk(OFF_ZA + 2 * D_A),
                  full(ys), full(oc),
                  zblk(OFF_GATES), zblk(OFF_GATES + D_MODEL), zblk(OFF_GATES + 2 * D_MODEL),
                  full(x), full(b0), full(b1),
                  full(p["conv_a_w"]), full(p["w_a_out"]), full(p["w_ssm_out"]), full(w_c),
                  full(p["w_out"]), full(p["norm_mix_post"])],
        out_specs=[pl.BlockSpec((m, D_MODEL), const), pl.BlockSpec((m, D_A), const)],
        out_shape=[jax.ShapeDtypeStruct((m, D_MODEL), F32), jax.ShapeDtypeStruct((m, D_A), F32)],
        compiler_params=_cparams(1, VMEM_LIMIT_BIG),
        name="tail_sample",
    )(z, z, z, ys, oc, z, z, z, x, b0, b1, p["conv_a_w"], p["w_a_out"], p["w_ssm_out"], w_c,
      p["w_out"], p["norm_mix_post"])


def _xattn_prompt_kernel(x_ref, k_ref, v_ref, gpre_ref, wq_ref, wo_ref, gpost_ref, xo_ref):
    x = x_ref[...]
    h = _rms(x, gpre_ref[...]).astype(BF16)
    qv = (_dot(h, wq_ref[...]) * (X_HD ** -0.5)).astype(BF16)
    o_tiles = []
    for hh in range(X_H):
        kh = k_ref[:, hh * X_HD:(hh + 1) * X_HD].astype(BF16)
        vh = v_ref[:, hh * X_HD:(hh + 1) * X_HD].astype(BF16)
        s = _dot_nt(qv[:, hh * X_HD:(hh + 1) * X_HD], kh)
        m = jnp.max(s, axis=-1, keepdims=True)
        e = jnp.exp(s - m)
        pb = (e * (1.0 / jnp.sum(e, axis=-1, keepdims=True))).astype(BF16)
        o_tiles.append(_dot(pb, vh))
    o = jnp.concatenate(o_tiles, axis=1).astype(BF16)
    xo_ref[...] = x + _rms(_dot(o, wo_ref[...]), gpost_ref[...])


def _xattn_prompt(x, mk, mv, p, nb, seq, mem_len, tm):
    ns = seq // tm
    const = lambda b, s: (0, 0)
    return pl.pallas_call(
        _xattn_prompt_kernel,
        grid=(nb, ns),
        in_specs=[pl.BlockSpec((tm, D_MODEL), lambda b, s: (b * ns + s, 0)),
                  pl.BlockSpec((mem_len, X_H * X_HD), lambda b, s: (b, 0)),
                  pl.BlockSpec((mem_len, X_H * X_HD), lambda b, s: (b, 0)),
                  pl.BlockSpec((1, D_MODEL), const),
                  pl.BlockSpec((D_MODEL, X_H * X_HD), const),
                  pl.BlockSpec((X_H * X_HD, D_MODEL), const),
                  pl.BlockSpec((1, D_MODEL), const)],
        out_specs=pl.BlockSpec((tm, D_MODEL), lambda b, s: (b * ns + s, 0)),
        out_shape=jax.ShapeDtypeStruct((nb * seq, D_MODEL), F32),
        compiler_params=_cparams(2),
        name="xattn_prompt",
    )(x, mk, mv, p["norm_x_pre"], p["w_xq"], p["w_xo"], p["norm_x_post"])


def _ffn_finish(kf, part, x_ref, gpost_ref, xo_ref, acc_scr):
    @pl.when(kf == 0)
    def _():
        acc_scr[...] = part

    @pl.when(kf > 0)
    def _():
        acc_scr[...] = acc_scr[...] + part

    @pl.when(kf == pl.num_programs(2) - 1)
    def _():
        xo_ref[...] = x_ref[...] + _rms(acc_scr[...], gpost_ref[...])


def _ffn_prompt_kernel(x_ref, gpre_ref, wa_ref, wg_ref, cw_ref, cb_ref, wo_ref, gpost_ref,
                       xo_ref, bufo0_ref, bufo1_ref, h_scr, acc_scr, a_scr):
    s = pl.program_id(1)
    kf = pl.program_id(2)
    tm = x_ref.shape[0]
    pad = SUBLANES

    @pl.when(kf == 0)
    def _():
        h_scr[...] = _rms(x_ref[...], gpre_ref[...]).astype(BF16)

    @pl.when(s == 0)
    def _():
        a_scr[kf, 0:pad, :] = jnp.zeros((pad, FF_CHUNK), F32)

    h = h_scr[...]
    cw = cw_ref[...]
    cb = cb_ref[...]

    def up(lo, hi):
        return _dot(h, wa_ref[:, lo:hi]), _dot(h, wg_ref[:, lo:hi])

    bounds = [(lo, min(lo + FFN_SUB, FF_CHUNK)) for lo in range(0, FF_CHUNK, FFN_SUB)]
    part = None
    nxt = up(*bounds[0])
    for i, (lo, hi) in enumerate(bounds):
        a, gate = nxt
        if i + 1 < len(bounds):
            nxt = up(*bounds[i + 1])
        a_scr[kf, pad:pad + tm, lo:hi] = a
        ac = cb[:, lo:hi] + a * cw[FFN_CONV_W - 1:FFN_CONV_W, lo:hi]
        for k in range(FFN_CONV_W - 1):
            off = pad - (FFN_CONV_W - 1) + k
            ac = ac + a_scr[kf, off:off + tm, lo:hi] * cw[k:k + 1, lo:hi]
        d = _dot((_silu(ac) * gate).astype(BF16), wo_ref[lo:hi, :])
        part = d if part is None else part + d
    tail = a_scr[kf, tm:tm + pad, :]
    a_scr[kf, 0:pad, :] = tail

    @pl.when(kf == 0)
    def _():
        bufo0_ref[0] = tail

    @pl.when(kf == 1)
    def _():
        bufo1_ref[0] = tail

    _ffn_finish(kf, part, x_ref, gpost_ref, xo_ref, acc_scr)


def _ffn_prompt(x, p, nb, seq, tm):
    ns = seq // tm
    nk = D_FF // FF_CHUNK
    const = lambda b, s, k: (0, 0)
    return pl.pallas_call(
        _ffn_prompt_kernel,
        grid=(nb, ns, nk),
        in_specs=[pl.BlockSpec((tm, D_MODEL), lambda b, s, k: (b * ns + s, 0)),
                  pl.BlockSpec((1, D_MODEL), const),
                  pl.BlockSpec((D_MODEL, FF_CHUNK), lambda b, s, k: (0, k)),
                  pl.BlockSpec((D_MODEL, FF_CHUNK), lambda b, s, k: (0, nk + k)),
                  pl.BlockSpec((SUBLANES, FF_CHUNK), lambda b, s, k: (0, k)),
                  pl.BlockSpec((1, FF_CHUNK), lambda b, s, k: (0, k)),
                  pl.BlockSpec((FF_CHUNK, D_MODEL), lambda b, s, k: (k, 0)),
                  pl.BlockSpec((1, D_MODEL), const)],
        out_specs=[pl.BlockSpec((tm, D_MODEL), lambda b, s, k: (b * ns + s, 0)),
                   pl.BlockSpec((1, SUBLANES, FF_CHUNK), lambda b, s, k: (b, 0, 0)),
                   pl.BlockSpec((1, SUBLANES, FF_CHUNK), lambda b, s, k: (b, 0, 0))],
        out_shape=[jax.ShapeDtypeStruct((nb * seq, D_MODEL), F32),
                   jax.ShapeDtypeStruct((nb, SUBLANES, FF_CHUNK), F32),
                   jax.ShapeDtypeStruct((nb, SUBLANES, FF_CHUNK), F32)],
        scratch_shapes=[pltpu.VMEM((tm, D_MODEL), BF16),
                        pltpu.VMEM((tm, D_MODEL), F32),
                        pltpu.VMEM((nk, tm + SUBLANES, FF_CHUNK), F32)],
        compiler_params=_cparams(3, VMEM_LIMIT_BIG),
        name="ffn_prompt",
    )(x, p["norm_ffn_pre"], p["w_ffn_in"], p["w_ffn_in"], p["ffn_conv_w"], p["ffn_conv_b"],
      p["w_ffn_out"], p["norm_ffn_post"])


def _ffn_sample_kernel(x_ref, gpre_ref, wa_ref, wg_ref, cw_ref, cb_ref, b0_ref, b1_ref, wo_ref,
                       gpost_ref, xo_ref, ao_ref, h_scr, acc_scr):
    kf = pl.program_id(2)

    @pl.when(kf == 0)
    def _():
        h_scr[...] = _rms(x_ref[...], gpre_ref[...]).astype(BF16)

    h = h_scr[...]
    a = _dot(h, wa_ref[...])
    gate = _dot(h, wg_ref[...])
    ao_ref[...] = a
    cw = cw_ref[...]
    ac = cb_ref[...] + b0_ref[...] * cw[0:1, :] + b1_ref[...] * cw[1:2, :] + a * cw[2:3, :]
    part = _dot((_silu(ac) * gate).astype(BF16), wo_ref[...])
    _ffn_finish(kf, part, x_ref, gpost_ref, xo_ref, acc_scr)


def _ffn_sample(x, b0, b1, p):
    m = x.shape[0]
    nk = D_FF // FF_CHUNK
    const = lambda b, s, k: (0, 0)
    chunk = lambda b, s, k: (0, k)
    return pl.pallas_call(
        _ffn_sample_kernel,
        grid=(1, 1, nk),
        in_specs=[pl.BlockSpec((m, D_MODEL), const),
                  pl.BlockSpec((1, D_MODEL), const),
                  pl.BlockSpec((D_MODEL, FF_CHUNK), chunk),
                  pl.BlockSpec((D_MODEL, FF_CHUNK), lambda b, s, k: (0, nk + k)),
                  pl.BlockSpec((SUBLANES, FF_CHUNK), chunk),
                  pl.BlockSpec((1, FF_CHUNK), chunk),
                  pl.BlockSpec((m, FF_CHUNK), chunk),
                  pl.BlockSpec((m, FF_CHUNK), chunk),
                  pl.BlockSpec((FF_CHUNK, D_MODEL), lambda b, s, k: (k, 0)),
                  pl.BlockSpec((1, D_MODEL), const)],
        out_specs=[pl.BlockSpec((m, D_MODEL), const),
                   pl.BlockSpec((m, FF_CHUNK), chunk)],
        out_shape=[jax.ShapeDtypeStruct((m, D_MODEL), F32),
                   jax.ShapeDtypeStruct((m, D_FF), F32)],
        scratch_shapes=[pltpu.VMEM((m, D_MODEL), BF16),
                        pltpu.VMEM((m, D_MODEL), F32)],
        compiler_params=_cparams(3),
        name="ffn_sample",
    )(x, p["norm_ffn_pre"], p["w_ffn_in"], p["w_ffn_in"], p["ffn_conv_w"], p["ffn_conv_b"], b0, b1,
      p["w_ffn_out"], p["norm_ffn_post"])


def _dec_prep_kernel(xbc_ref, dtr_ref, b0_ref, b1_ref, b2_ref, cw_ref, cb_ref, dtb_ref, alog_ref,
                     dx_ref, e_ref, xdt_t_ref, dec_t_ref, bs_ref, cs_ref, xsd_ref):
    cw = cw_ref[...]
    xc = (cb_ref[...] + b0_ref[...] * cw[0:1, :] + b1_ref[...] * cw[1:2, :] + b2_ref[...] * cw[2:3, :]
          + xbc_ref[...].astype(F32) * cw[3:4, :])
    xc = _silu(xc)
    xs = xc[:, :D_SSM]
    bs_ref[...] = xc[:, D_SSM:D_SSM + G_SSM * N_SSM]
    cs_ref[...] = xc[:, D_SSM + G_SSM * N_SSM:]
    dt = _softplus(dtr_ref[...] + dtb_ref[...])
    dec = jnp.exp(dt * (-jnp.exp(alog_ref[...])))
    e = e_ref[...]
    xdt_t_ref[...] = (xs * _dot_exact_lhs(dt, e)).T
    dec_t_ref[...] = _dot_exact_lhs(dec, e).T
    xsd_ref[...] = xs * dx_ref[...]


def _dec_prep(z, dtr, b0, b1, b2, p):
    m = dtr.shape[0]
    const = lambda i: (0, 0)
    full = lambda a: pl.BlockSpec(a.shape, const)
    return pl.pallas_call(
        _dec_prep_kernel,
        grid=(1,),
        in_specs=[pl.BlockSpec((m, SSM_CONV_DIM), lambda i: (0, OFF_XBC // SSM_CONV_DIM)),
                  full(dtr), full(b0), full(b1), full(b2), full(p["ssm_conv_w"]), full(p["ssm_conv_b"]),
                  full(p["dt_bias"]), full(p["a_log"]), full(p["d_x"]), full(p["expand"])],
        out_specs=[pl.BlockSpec((D_SSM, m), const), pl.BlockSpec((D_SSM, m), const),
                   pl.BlockSpec((m, G_SSM * N_SSM), const), pl.BlockSpec((m, G_SSM * N_SSM), const),
                   pl.BlockSpec((m, D_SSM), const)],
        out_shape=[jax.ShapeDtypeStruct((D_SSM, m), F32), jax.ShapeDtypeStruct((D_SSM, m), F32),
                   jax.ShapeDtypeStruct((m, G_SSM * N_SSM), F32),
                   jax.ShapeDtypeStruct((m, G_SSM * N_SSM), F32),
                   jax.ShapeDtypeStruct((m, D_SSM), F32)],
        compiler_params=_cparams(1),
        name="dec_prep",
    )(z, dtr, b0, b1, b2, p["ssm_conv_w"], p["ssm_conv_b"], p["dt_bias"], p["a_log"], p["d_x"],
      p["expand"])


def _dec_state_kernel(st_ref, xdt_t_ref, dec_t_ref, bs_ref, cs_ref, *rest):
    so_ref, yt_ref = rest[-2:]
    b = pl.program_id(0)
    nb = xdt_t_ref.shape[1]

    @pl.when(b == 0)
    def _():
        yt_ref[...] = jnp.zeros(yt_ref.shape, F32)

    sel = lax.broadcasted_iota(jnp.int32, (D_SSM, nb), 1) == b
    xcol = jnp.sum(jnp.where(sel, xdt_t_ref[...], 0.0), axis=-1, keepdims=True)
    dcol = jnp.sum(jnp.where(sel, dec_t_ref[...], 0.0), axis=-1, keepdims=True)
    brow = bs_ref[pl.ds(b, 1), :]
    crow = cs_ref[pl.ds(b, 1), :]
    rows_per_group = D_SSM // G_SSM
    bfull = jnp.concatenate(
        [jnp.broadcast_to(brow[:, g * N_SSM:(g + 1) * N_SSM], (rows_per_group, N_SSM))
         for g in range(G_SSM)], axis=0)
    cfull = jnp.concatenate(
        [jnp.broadcast_to(crow[:, g * N_SSM:(g + 1) * N_SSM], (rows_per_group, N_SSM))
         for g in range(G_SSM)], axis=0)
    hn = st_ref[0] * dcol + xcol * bfull
    so_ref[0] = hn
    ycol = jnp.sum(hn * cfull, axis=-1, keepdims=True)
    yt_ref[...] = jnp.where(sel, ycol, yt_ref[...])


def _dec_state(state_all, layer, xdt_t, dec_t, bs, cs, carry):
    nb = xdt_t.shape[1]
    const = lambda b: (0, 0)
    slab = lambda b: (layer * nb + b, 0, 0)
    extra = [] if carry is None else [carry]
    return pl.pallas_call(
        _dec_state_kernel,
        grid=(nb,),
        in_specs=[pl.BlockSpec((1, D_SSM, N_SSM), slab),
                  pl.BlockSpec(xdt_t.shape, const), pl.BlockSpec(dec_t.shape, const),
                  pl.BlockSpec(bs.shape, const), pl.BlockSpec(cs.shape, const)]
                 + [pl.BlockSpec(memory_space=pl.ANY)] * len(extra),
        out_specs=[pl.BlockSpec((1, D_SSM, N_SSM), slab),
                   pl.BlockSpec((D_SSM, nb), const)],
        out_shape=[jax.ShapeDtypeStruct(state_all.shape, F32),
                   jax.ShapeDtypeStruct((D_SSM, nb), F32)],
        input_output_aliases={5: 0} if extra else {},
        compiler_params=_cparams(1),
        name="dec_state",
    )(state_all, xdt_t, dec_t, bs, cs, *extra)


def _dec_post_kernel(yt_ref, xsd_ref, zs_ref, nrm_ref, o_ref):
    y = yt_ref[...].T + xsd_ref[...]
    y = y * _silu(zs_ref[...].astype(F32))
    o_ref[...] = _rms(y, nrm_ref[...]).astype(o_ref.dtype)


def _dec_post(yt, xsd, z, p):
    m = xsd.shape[0]
    const = lambda i: (0, 0)
    return pl.pallas_call(
        _dec_post_kernel,
        grid=(1,),
        in_specs=[pl.BlockSpec(yt.shape, const), pl.BlockSpec(xsd.shape, const),
                  pl.BlockSpec((m, D_SSM), lambda i: (0, OFF_ZS // D_SSM)),
                  pl.BlockSpec((1, D_SSM), const)],
        out_specs=pl.BlockSpec((m, D_SSM), const),
        out_shape=jax.ShapeDtypeStruct((m, D_SSM), BF16),
        compiler_params=_cparams(1),
        name="dec_post",
    )(yt, xsd, z, p["ssm_norm"])


def _dec_swa_kernel(qe_ref, kn_ref, vn_ref, ck_ref, cv_ref, c_ref, s1_ref, s2_ref, sink_ref, *rest):
    nk_ref, nv_ref, oe_ref = rest[-3:]
    bt = ck_ref.shape[0]
    w = ck_ref.shape[1]
    c, s1, s2 = c_ref[0:1, :], s1_ref[0:1, :], s2_ref[0:1, :]
    kn = _rope(kn_ref[...].astype(F32), c, s1, s2)
    vn = vn_ref[...].astype(F32)
    last = lax.broadcasted_iota(jnp.int32, (w, KV_ATT * HD_ATT), 0) == w - 1
    sink = sink_ref[:, 0:1]
    for i in range(bt):
        nk = jnp.where(last, kn[i:i + 1, :], pltpu.roll(ck_ref[i], w - 1, 0))
        nv = jnp.where(last, vn[i:i + 1, :], pltpu.roll(cv_ref[i], w - 1, 0))
        nk_ref[i] = nk
        nv_ref[i] = nv
        qe = _rope(qe_ref[i].astype(F32), c, s1, s2) * (HD_ATT ** -0.5)
        s = _dot_nt(qe.astype(BF16), nk.astype(BF16))
        m = jnp.maximum(jnp.max(s, axis=-1, keepdims=True), sink)
        e = jnp.exp(s - m)
        den = jnp.sum(e, axis=-1, keepdims=True) + jnp.exp(sink - m)
        pb = (e * (1.0 / den)).astype(BF16)
        oe_ref[i] = _dot(pb, nv.astype(BF16)).astype(oe_ref.dtype)


def _dec_swa(qe, z, ck_all, cv_all, layer, p, bt, carry):
    nb = qe.shape[0]
    _, w, kvw = ck_all.shape
    const = lambda i: (0, 0)
    slab = lambda i: (layer * (nb // bt) + i, 0, 0)
    tab = pl.BlockSpec((SUBLANES, LANES), const)
    extra = [] if carry is None else list(carry)
    return pl.pallas_call(
        _dec_swa_kernel,
        grid=(nb // bt,),
        in_specs=[pl.BlockSpec((bt, H_ATT, kvw), lambda i: (i, 0, 0)),
                  pl.BlockSpec((bt, kvw), lambda i: (i, OFF_K // kvw)),
                  pl.BlockSpec((bt, kvw), lambda i: (i, OFF_V // kvw)),
                  pl.BlockSpec((bt, w, kvw), slab),
                  pl.BlockSpec((bt, w, kvw), slab),
                  tab, tab, tab,
                  pl.BlockSpec((H_ATT, LANES), const)]
                 + [pl.BlockSpec(memory_space=pl.ANY)] * len(extra),
        out_specs=[pl.BlockSpec((bt, w, kvw), slab),
                   pl.BlockSpec((bt, w, kvw), slab),
                   pl.BlockSpec((bt, H_ATT, kvw), lambda i: (i, 0, 0))],
        out_shape=[jax.ShapeDtypeStruct(ck_all.shape, F32), jax.ShapeDtypeStruct(cv_all.shape, F32),
                   jax.ShapeDtypeStruct((nb, H_ATT, kvw), BF16)],
        input_output_aliases={9: 0, 10: 1} if extra else {},
        compiler_params=_cparams(1),
        name="dec_swa",
    )(qe, z, z, ck_all, cv_all, p["rope_c_s"], p["rope_s1_s"], p["rope_s2_s"], p["sinks_x"], *extra)


def _dec_xattn_kernel(q_ref, k_ref, v_ref, o_ref):
    bt = k_ref.shape[1]
    for i in range(bt):
        qh = q_ref[i] * (X_HD ** -0.5)
        s = jnp.sum(k_ref[0, i] * qh[None], axis=-1, keepdims=True)
        m = jnp.max(s, axis=0, keepdims=True)
        e = jnp.exp(s - m)
        pr = e * (1.0 / jnp.sum(e, axis=0, keepdims=True))
        o_ref[i] = jnp.sum(pr * v_ref[0, i], axis=0)


def _dec_xattn(qv, mem_k, mem_v, layer, bt):
    nb = qv.shape[0]
    _, _, mem_len, nh, hd = mem_k.shape
    slab = lambda i: (layer, i, 0, 0, 0)
    return pl.pallas_call(
        _dec_xattn_kernel,
        grid=(nb // bt,),
        in_specs=[pl.BlockSpec((bt, nh, hd), lambda i: (i, 0, 0)),
                  pl.BlockSpec((1, bt, mem_len, nh, hd), slab),
                  pl.BlockSpec((1, bt, mem_len, nh, hd), slab)],
        out_specs=pl.BlockSpec((bt, nh, hd), lambda i: (i, 0, 0)),
        out_shape=jax.ShapeDtypeStruct((nb, nh, hd), F32),
        compiler_params=_cparams(1),
        name="dec_xattn",
    )(qv, mem_k, mem_v)


def _pad_rows(w, rows=SUBLANES):
    return jnp.pad(w, ((0, rows - w.shape[0]), (0, 0)))


def _pad_lanes(v, lanes=LANES):
    return jnp.pad(v, (0, lanes - v.shape[0])).reshape(1, lanes)


def _rope_tables(pos):
    half = ROT_DIM // 2
    inv = ROPE_THETA ** (-jnp.arange(half, dtype=F32) / half)
    ang = pos.astype(F32)[:, None] * inv[None, :]
    cos, sin = jnp.cos(ang), jnp.sin(ang)
    n = pos.shape[0]
    ones = jnp.ones((n, HD_ATT - ROT_DIM), F32)
    zeros = jnp.zeros((n, HD_ATT - ROT_DIM), F32)
    zh = jnp.zeros((n, half), F32)
    c = jnp.concatenate([cos, cos, ones], axis=1)
    s1 = jnp.concatenate([-sin, zh, zeros], axis=1)
    s2 = jnp.concatenate([zh, sin, zeros], axis=1)
    rep = LANES // HD_ATT
    return tuple(jnp.tile(t, (1, rep)) for t in (c, s1, s2))


_BF16_WEIGHTS = ("w_in", "w_a_out", "w_ssm_out", "w_attn_out", "w_out", "w_xq", "w_xk", "w_xv", "w_xo",
                 "w_ffn_in", "w_ffn_out")


def _prep_layer(l, prm, wb, seq):
    w_in = wb["w_in"][l]
    o = np.cumsum([0, D_A, D_A, D_A, D_SSM, SSM_CONV_DIM, H_SSM, H_ATT * HD_ATT, KV_ATT * HD_ATT,
                   KV_ATT * HD_ATT, 3 * D_MODEL])
    col = lambda i: w_in[:, o[i]:o[i + 1]]
    w_main = jnp.concatenate([col(4), col(3), col(0), col(1), col(2), col(9), col(6), col(7), col(8)],
                             axis=1)
    w_dt = jnp.pad(col(5), ((0, 0), (0, LANES - H_SSM)))
    rep = H_ATT // KV_ATT
    wq, wc = col(6), wb["w_attn_out"][l]
    q_blocks, c_blocks = [], []
    for h in range(H_ATT):
        g = h // rep
        wq_h = wq[:, h * HD_ATT:(h + 1) * HD_ATT]
        wc_h = wc[h * HD_ATT:(h + 1) * HD_ATT, :]
        q_blocks += [jnp.zeros((D_MODEL, g * HD_ATT), BF16), wq_h,
                     jnp.zeros((D_MODEL, (KV_ATT - 1 - g) * HD_ATT), BF16)]
        c_blocks += [jnp.zeros((g * HD_ATT, D_MODEL), BF16), wc_h,
                     jnp.zeros(((KV_ATT - 1 - g) * HD_ATT, D_MODEL), BF16)]
    wq_exp = jnp.concatenate(q_blocks, axis=1)
    wc_exp = jnp.concatenate(c_blocks, axis=0)
    expand = jnp.asarray(np.kron(np.eye(LANES, H_SSM, dtype=np.float32),
                                 np.ones((1, SSM_HEADDIM), np.float32))).astype(BF16)
    rope_p = _rope_tables(jnp.arange(seq, dtype=jnp.int32))
    rope_s = _rope_tables(jnp.full((SUBLANES,), PAST_LEN, jnp.int32))
    return {
        "norm_mix_pre": prm["norm_mix_pre"][l].reshape(1, -1),
        "norm_mix_post": prm["norm_mix_post"][l].reshape(1, -1),
        "w_main": w_main, "w_dt": w_dt, "wq_exp": wq_exp,
        "conv_a_w": _pad_rows(prm["conv_a_w"][l]),
        "w_a_out": wb["w_a_out"][l],
        "ssm_conv_w": _pad_rows(prm["ssm_conv_w"][l]),
        "ssm_conv_b": prm["ssm_conv_b"][l].reshape(1, -1),
        "dt_bias": _pad_lanes(prm["ssm_dt_bias"][l]),
        "a_log": _pad_lanes(prm["ssm_a_log"][l]),
        "d_x": jnp.repeat(prm["ssm_d"][l], SSM_HEADDIM).reshape(1, -1),
        "ssm_norm": prm["ssm_norm"][l].reshape(1, -1),
        "expand": expand,
        "w_ssm_out": wb["w_ssm_out"][l],
        "sinks": prm["attn_sinks"][l],
        "sinks_x": jnp.broadcast_to(prm["attn_sinks"][l][:, None], (H_ATT, LANES)),
        "w_attn_out": wb["w_attn_out"][l],
        "w_attn_out_exp": wc_exp,
        "w_out": wb["w_out"][l],
        "rope_c": rope_p[0], "rope_s1": rope_p[1], "rope_s2": rope_p[2],
        "rope_c_s": rope_s[0], "rope_s1_s": rope_s[1], "rope_s2_s": rope_s[2],
        "norm_x_pre": prm["norm_x_pre"][l].reshape(1, -1),
        "norm_x_post": prm["norm_x_post"][l].reshape(1, -1),
        "norm_mem": prm["norm_mem"][l].reshape(1, -1),
        "w_xq": wb["w_xq"][l],
        "w_xk": wb["w_xk"][l],
        "w_xv": wb["w_xv"][l],
        "w_xo": wb["w_xo"][l],
        "norm_ffn_pre": prm["norm_ffn_pre"][l].reshape(1, -1),
        "norm_ffn_post": prm["norm_ffn_post"][l].reshape(1, -1),
        "w_ffn_in": wb["w_ffn_in"][l],
        "ffn_conv_w": _pad_rows(prm["ffn_conv_w"][l]),
        "ffn_conv_b": prm["ffn_conv_b"][l].reshape(1, -1),
        "w_ffn_out": wb["w_ffn_out"][l],
    }


def _pick_tile(n, pref):
    t = min(n, pref)
    while n % t:
        t //= 2
    return t


def _prompt_layer(x, mem, p, nb, seq, mem_len):
    rows = nb * seq
    z, dtr = _in_proj(x, p["norm_mix_pre"], p["w_main"], p["w_dt"], _pick_tile(rows, 1024), 1536)
    ys, h_last, cbuf = _ssd_prompt(z, dtr, p, nb, seq)
    oc, k_new, v_new = _swa_prompt(z, p, nb, seq)
    x, bufa = _tail_prompt(z, ys, oc, x, p, nb, seq, _pick_tile(seq, 512))
    tmem = _pick_tile(mem.shape[0], 512)
    mk = _norm_matmul(mem, p["norm_mem"], p["w_xk"], F32, tmem, X_H * X_HD)
    mv = _norm_matmul(mem, p["norm_mem"], p["w_xv"], F32, tmem, X_H * X_HD)
    x = _xattn_prompt(x, mk, mv, p, nb, seq, mem_len, _pick_tile(seq, 512))
    x, buff0, buff1 = _ffn_prompt(x, p, nb, seq, _pick_tile(seq, 512))
    buff = jnp.concatenate([buff0, buff1], axis=-1)
    state = (bufa[:, SUBLANES - (CONV_A_W - 1):],
             cbuf[:, SUBLANES - (SSM_CONV_W - 1):],
             h_last.reshape(nb, H_SSM, SSM_HEADDIM, N_SSM),
             k_new.reshape(nb, WINDOW, KV_ATT, HD_ATT),
             v_new.reshape(nb, WINDOW, KV_ATT, HD_ATT),
             buff[:, SUBLANES - (FFN_CONV_W - 1):],
             mk.reshape(nb, mem_len, X_H, X_HD),
             mv.reshape(nb, mem_len, X_H, X_HD))
    return x, state


def _sample_layer(x, layer, st, big, carry, p):
    buf_a, buf_ssm, buf_ffn = st
    ssm_all, swa_k_all, swa_v_all, mem_k_all, mem_v_all = big
    nb = x.shape[0]
    kvw = KV_ATT * HD_ATT
    z, dtr = _in_proj(x, p["norm_mix_pre"], p["w_main"], p["w_dt"], nb, 1536)
    qe = _norm_matmul(x, p["norm_mix_pre"], p["wq_exp"], BF16, nb, 1024).reshape(nb, H_ATT, kvw)
    xdt_t, dec_t, bs, cs, xsd = _dec_prep(z, dtr, buf_ssm[:, 0], buf_ssm[:, 1], buf_ssm[:, 2], p)
    new_ssm, yt = _dec_state(ssm_all, layer, xdt_t, dec_t, bs, cs, None if carry is None else carry[0])
    ys = _dec_post(yt, xsd, z, p)
    new_k, new_v, oe = _dec_swa(qe, z, swa_k_all, swa_v_all, layer, p, _pick_tile(nb, 16),
                                None if carry is None else carry[1:3])
    x, cv = _tail_sample(z, ys, oe.reshape(nb, H_ATT * kvw), x, buf_a[:, 0], buf_a[:, 1], p)
    qx = _norm_matmul(x, p["norm_x_pre"], p["w_xq"], F32, nb, 1024)
    ox = _dec_xattn(qx.reshape(nb, X_H, X_HD), mem_k_all, mem_v_all, layer, _pick_tile(nb, 4))
    x = _mm_norm_res(ox.reshape(nb, X_H * X_HD), p["w_xo"], p["norm_x_post"], x)
    x, a_up = _ffn_sample(x, buf_ffn[:, 0], buf_ffn[:, 1], p)
    x_raw = z[:, OFF_XBC:OFF_XBC + SSM_CONV_DIM].astype(F32)
    small = (jnp.stack([buf_a[:, 1], cv], axis=1),
             jnp.concatenate([buf_ssm[:, 1:], x_raw[:, None, :]], axis=1),
             jnp.stack([buf_ffn[:, 1], a_up], axis=1))
    return x, small, (new_ssm, new_k, new_v)


def kernel(x_prompt, x_sample, mem_prompt, state_conv_a, state_ssm_conv, state_ssm, cache_swa_k, cache_swa_v, cache_mem_k, cache_mem_v, state_ffn_conv, norm_mix_pre, norm_mix_post, w_in, conv_a_w, w_a_out, ssm_conv_w, ssm_conv_b, ssm_dt_bias, ssm_a_log, ssm_d, ssm_norm, w_ssm_out, attn_sinks, w_attn_out, w_out, norm_x_pre, norm_x_post, norm_mem, w_xq, w_xk, w_xv, w_xo, norm_ffn_pre, norm_ffn_post, w_ffn_in, ffn_conv_w, ffn_conv_b, w_ffn_out):
    prm = dict(norm_mix_pre=norm_mix_pre, norm_mix_post=norm_mix_post, w_in=w_in, conv_a_w=conv_a_w,
               w_a_out=w_a_out, ssm_conv_w=ssm_conv_w, ssm_conv_b=ssm_conv_b, ssm_dt_bias=ssm_dt_bias,
               ssm_a_log=ssm_a_log, ssm_d=ssm_d, ssm_norm=ssm_norm, w_ssm_out=w_ssm_out,
               attn_sinks=attn_sinks, w_attn_out=w_attn_out, w_out=w_out, norm_x_pre=norm_x_pre,
               norm_x_post=norm_x_post, norm_mem=norm_mem, w_xq=w_xq, w_xk=w_xk, w_xv=w_xv, w_xo=w_xo,
               norm_ffn_pre=norm_ffn_pre, norm_ffn_post=norm_ffn_post, w_ffn_in=w_ffn_in,
               ffn_conv_w=ffn_conv_w, ffn_conv_b=ffn_conv_b, w_ffn_out=w_ffn_out)
    nb, seq, d = x_prompt.shape
    ns = x_sample.shape[0]
    mem_len = mem_prompt.shape[1]
    depth = w_in.shape[0]
    assert x_sample.shape[1] == 1 and seq % WINDOW == 0 and seq % SSD_CHUNK == 0
    assert cache_swa_k.shape[2] == WINDOW and PAST_LEN >= WINDOW

    yp = x_prompt.reshape(nb * seq, d)
    ys = x_sample.reshape(ns, d)
    mem = mem_prompt.reshape(nb * mem_len, d)
    flat = lambda a, *tail: a.reshape((depth * ns,) + tail)
    kvw = KV_ATT * HD_ATT
    big = (flat(state_ssm, D_SSM, N_SSM),
           flat(cache_swa_k, cache_swa_k.shape[2], kvw), flat(cache_swa_v, cache_swa_v.shape[2], kvw),
           cache_mem_k, cache_mem_v)
    wb = {name: prm[name].astype(BF16) for name in _BF16_WEIGHTS}
    new_p, new_s, carry = [], [], None
    for l in range(depth):
        p = _prep_layer(l, prm, wb, seq)
        yp, st_p = _prompt_layer(yp, mem, p, nb, seq, mem_len)
        new_p.append(st_p)
        ys, st_s, carry = _sample_layer(ys, l, (state_conv_a[l], state_ssm_conv[l], state_ffn_conv[l]),
                                        big, carry, p)
        new_s.append(st_s)
    stack = lambda lst, i: jnp.stack([s[i] for s in lst])
    s_ssm, s_swa_k, s_swa_v = (carry[0].reshape(state_ssm.shape), carry[1].reshape(cache_swa_k.shape),
                               carry[2].reshape(cache_swa_v.shape))
    return ((yp.reshape(nb, seq, d), ys.reshape(ns, 1, d))
            + tuple(stack(new_p, i) for i in range(8))
            + (stack(new_s, 0), stack(new_s, 1), s_ssm, s_swa_k, s_swa_v, stack(new_s, 2)))
```

```python
import functools

import numpy as np
import jax
import jax.numpy as jnp
from jax import lax
from jax.experimental import pallas as pl
from jax.experimental.pallas import tpu as pltpu

F32 = jnp.float32
BF16 = jnp.bfloat16

EPS = 1e-6
D_MODEL = 1024
D_A = D_MODEL
CONV_A_W = 3
D_SSM = 2 * D_MODEL
SSM_HEADDIM = 64
H_SSM = D_SSM // SSM_HEADDIM
G_SSM = 8
N_SSM = 128
SSM_CONV_W = 4
SSM_CONV_DIM = D_SSM + 2 * G_SSM * N_SSM
SSD_CHUNK = 128
H_ATT = 16
KV_ATT = 4
HD_ATT = 64
ROT_DIM = HD_ATT // 4
ROPE_THETA = 500000.0
WINDOW = 128
X_H = 4
X_HD = D_MODEL // X_H
D_FF = 2816
FFN_CONV_W = 3
PAST_LEN = 8192
LOG2E = 1.4426950408889634

LANES = 128
SUBLANES = 8
VMEM_LIMIT = 48 * 1024 * 1024
VMEM_LIMIT_BIG = 56 * 1024 * 1024

OFF_XBC = 0
OFF_ZS = OFF_XBC + SSM_CONV_DIM
OFF_ZA = OFF_ZS + D_SSM
OFF_GATES = OFF_ZA + 3 * D_A
OFF_Q = OFF_GATES + 3 * D_MODEL
OFF_K = OFF_Q + H_ATT * HD_ATT
OFF_V = OFF_K + KV_ATT * HD_ATT
N_MAIN = OFF_V + KV_ATT * HD_ATT
FF_CHUNK = D_FF // 2
FFN_SUB = 4 * LANES


def _cparams(n_axes, vmem=VMEM_LIMIT):
    return pltpu.CompilerParams(dimension_semantics=("arbitrary",) * n_axes,
                                vmem_limit_bytes=vmem)


def _resident(shape, index_map):
    return pl.BlockSpec(shape, index_map, pipeline_mode=pl.Buffered(1))


def _rms(x, g):
    return x * lax.rsqrt(jnp.mean(x * x, axis=-1, keepdims=True) + EPS) * g


def _sigmoid(x):
    return 0.5 + 0.5 * jnp.tanh(0.5 * x)


def _silu(x):
    hx = 0.5 * x
    return hx + hx * jnp.tanh(hx)


def _softplus(x):
    return jnp.maximum(x, 0.0) + jnp.log(1.0 + jnp.exp(-jnp.abs(x)))


def _dot(a, b):
    return jnp.dot(a, b, preferred_element_type=F32)


def _dot_nt(a, b):
    return lax.dot_general(a, b, (((1,), (1,)), ((), ())), preferred_element_type=F32)


def _dot_exact_rhs(a01, v):
    hi = v.astype(BF16)
    r1 = v - hi.astype(F32)
    mid = r1.astype(BF16)
    lo = (r1 - mid.astype(F32)).astype(BF16)
    return _dot(a01, hi) + _dot(a01, mid) + _dot(a01, lo)


def _dot_exact_lhs(v, b01):
    hi = v.astype(BF16)
    r1 = v - hi.astype(F32)
    mid = r1.astype(BF16)
    lo = (r1 - mid.astype(F32)).astype(BF16)
    return _dot(hi, b01) + _dot(mid, b01) + _dot(lo, b01)


def _rope(x, c, s1, s2):
    outs = []
    for i in range(x.shape[1] // LANES):
        xi = x[:, i * LANES:(i + 1) * LANES]
        outs.append(xi * c + pltpu.roll(xi, LANES - ROT_DIM // 2, 1) * s1
                    + pltpu.roll(xi, ROT_DIM // 2, 1) * s2)
    return outs[0] if len(outs) == 1 else jnp.concatenate(outs, axis=1)


def _norm_mm_kernel(x_ref, g_ref, w_ref, o_ref, h_scr):
    @pl.when(pl.program_id(1) == 0)
    def _():
        h_scr[...] = _rms(x_ref[...], g_ref[...]).astype(BF16)

    o_ref[...] = _dot(h_scr[...], w_ref[...]).astype(o_ref.dtype)


def _norm_matmul(x, g, w, out_dtype, tm, tn):
    m, k = x.shape
    n = w.shape[1]
    return pl.pallas_call(
        _norm_mm_kernel,
        grid=(m // tm, n // tn),
        in_specs=[pl.BlockSpec((tm, k), lambda i, j: (i, 0)),
                  pl.BlockSpec((1, k), lambda i, j: (0, 0)),
                  pl.BlockSpec((k, tn), lambda i, j: (0, j))],
        out_specs=pl.BlockSpec((tm, tn), lambda i, j: (i, j)),
        out_shape=jax.ShapeDtypeStruct((m, n), out_dtype),
        scratch_shapes=[pltpu.VMEM((tm, k), BF16)],
        compiler_params=_cparams(2),
        name="norm_matmul",
    )(x, g, w)


def _inproj_kernel(x_ref, g_ref, w_ref, wdt_ref, z_ref, dt_ref, h_scr):
    @pl.when(pl.program_id(1) == 0)
    def _():
        h = _rms(x_ref[...], g_ref[...]).astype(BF16)
        h_scr[...] = h
        dt_ref[...] = _dot(h, wdt_ref[...])

    z_ref[...] = _dot(h_scr[...], w_ref[...]).astype(z_ref.dtype)


def _in_proj(x, g, w, wdt, tm, tn):
    m, k = x.shape
    n = w.shape[1]
    return pl.pallas_call(
        _inproj_kernel,
        grid=(m // tm, n // tn),
        in_specs=[pl.BlockSpec((tm, k), lambda i, j: (i, 0)),
                  pl.BlockSpec((1, k), lambda i, j: (0, 0)),
                  pl.BlockSpec((k, tn), lambda i, j: (0, j)),
                  pl.BlockSpec((k, LANES), lambda i, j: (0, 0))],
        out_specs=[pl.BlockSpec((tm, tn), lambda i, j: (i, j)),
                   pl.BlockSpec((tm, LANES), lambda i, j: (i, 0))],
        out_shape=[jax.ShapeDtypeStruct((m, n), BF16),
                   jax.ShapeDtypeStruct((m, LANES), F32)],
        scratch_shapes=[pltpu.VMEM((tm, k), BF16)],
        compiler_params=_cparams(2),
        name="in_proj",
    )(x, g, w, wdt)


def _mm_norm_res_kernel(a_ref, w_ref, g_ref, x_ref, o_ref):
    y = _dot(a_ref[...].astype(BF16), w_ref[...])
    o_ref[...] = x_ref[...] + _rms(y, g_ref[...])


def _mm_norm_res(a, w, g, x):
    m, k = a.shape
    n = w.shape[1]
    return pl.pallas_call(
        _mm_norm_res_kernel,
        grid=(1,),
        in_specs=[pl.BlockSpec((m, k), lambda i: (0, 0)),
                  pl.BlockSpec((k, n), lambda i: (0, 0)),
                  pl.BlockSpec((1, n), lambda i: (0, 0)),
                  pl.BlockSpec((m, n), lambda i: (0, 0))],
        out_specs=pl.BlockSpec((m, n), lambda i: (0, 0)),
        out_shape=jax.ShapeDtypeStruct((m, n), F32),
        compiler_params=_cparams(1),
        name="mm_norm_res",
    )(a, w, g, x)


def _ssd_kernel(xbc_ref, zs_ref, dtr_ref, cw_ref, cb_ref, dtb_ref, alog_ref, dx_ref, nrm_ref, e_ref,
                y_ref, hout_ref, cbuf_ref, x_scr, ht_scr):
    c = pl.program_id(1)
    q = xbc_ref.shape[0]
    pad = SUBLANES

    halo = x_scr.shape[0]

    @pl.when(c == 0)
    def _():
        x_scr[...] = jnp.zeros(x_scr.shape, BF16)
        ht_scr[...] = jnp.zeros(ht_scr.shape, F32)

    xb = xbc_ref[...]
    xin = xb.astype(F32)
    ext = jnp.concatenate([x_scr[...], xb], axis=0)
    n_sh = SSM_CONV_W - 1
    ri = lax.broadcasted_iota(jnp.int32, (n_sh * q, halo + q), 0)
    ci = lax.broadcasted_iota(jnp.int32, (n_sh * q, halo + q), 1)
    shift_mat = jnp.where(ci == (ri % q) + halo - n_sh + ri // q, 1.0, 0.0).astype(BF16)
    shifted = _dot(shift_mat, ext)
    cw = cw_ref[...]
    xc = cb_ref[...] + xin * cw[n_sh:n_sh + 1, :]
    for k in range(n_sh):
        xc = xc + shifted[k * q:(k + 1) * q, :] * cw[k:k + 1, :]
    xc = _silu(xc)
    x_scr[...] = xb[q - halo:q, :]
    cbuf_ref[0] = xin[q - pad:q, :]

    xs = xc[:, :D_SSM]
    bs = xc[:, D_SSM:D_SSM + G_SSM * N_SSM]
    cs_in = xc[:, D_SSM + G_SSM * N_SSM:]

    dt = _softplus(dtr_ref[...] + dtb_ref[...])
    a = dt * (-jnp.exp(alog_ref[...]))
    row = lax.broadcasted_iota(jnp.int32, (q, q), 0)
    col = lax.broadcasted_iota(jnp.int32, (q, q), 1)
    tri = row >= col
    cs = _dot_exact_rhs(jnp.where(tri, 1.0, 0.0).astype(BF16), a)
    cs_last = cs[q - 1:q, :]
    cs2 = cs * LOG2E
    src_t = (cs2 - jnp.log2(dt)).T
    wdd = dt * jnp.exp(cs_last - cs)
    cdec = jnp.broadcast_to(jnp.exp(cs_last), (SUBLANES, LANES))
    cdec_x = _dot_exact_lhs(cdec, e_ref[...])[0:1, :]

    lane = lax.broadcasted_iota(jnp.int32, (q, LANES), 1)
    heads_per_group = H_SSM // G_SSM
    y_tiles = []
    for g in range(G_SSM):
        bg = bs[:, g * N_SSM:(g + 1) * N_SSM]
        cg = cs_in[:, g * N_SSM:(g + 1) * N_SSM]
        cb = _dot_nt(cg.astype(BF16), bg.astype(BF16))
        xdd_tiles = []
        for pr in range(heads_per_group // 2):
            hp = g * (heads_per_group // 2) + pr
            xpair = xs[:, hp * LANES:(hp + 1) * LANES]
            wp = jnp.concatenate([xpair, ht_scr[:, hp * LANES:(hp + 1) * LANES]], axis=0).astype(BF16)
            parts = []
            wsel = []
            for r2 in range(2):
                h = 2 * hp + r2
                csb = jnp.broadcast_to(cs2[:, h:h + 1], (q, LANES))
                mmat = cb * jnp.exp2(jnp.where(tri, csb - src_t[h:h + 1, :], -jnp.inf))
                csc = cg * jnp.exp2(csb)
                lhs = jnp.concatenate([mmat, csc], axis=1).astype(BF16)
                parts.append(_dot(lhs, wp))
                wsel.append(jnp.broadcast_to(wdd[:, h:h + 1], (q, LANES)))
            y_tiles.append(jnp.where(lane < SSM_HEADDIM, parts[0], parts[1]))
            xdd_tiles.append(xpair * jnp.where(lane < SSM_HEADDIM, wsel[0], wsel[1]))
        xdd = jnp.concatenate(xdd_tiles, axis=1).astype(BF16)
        st_t = _dot(bg.T.astype(BF16), xdd)
        lo, hi = g * heads_per_group * SSM_HEADDIM, (g + 1) * heads_per_group * SSM_HEADDIM
        ht_scr[:, lo:hi] = ht_scr[:, lo:hi] * cdec_x[:, lo:hi] + st_t

    y = jnp.concatenate(y_tiles, axis=1) + xs * dx_ref[...]
    y = y * _silu(zs_ref[...].astype(F32))
    y_ref[...] = _rms(y, nrm_ref[...]).astype(y_ref.dtype)

    @pl.when(c == pl.num_programs(1) - 1)
    def _():
        hout_ref[0] = ht_scr[...].T


def _ssd_prompt(z, dtr, p, nb, seq):
    q = SSD_CHUNK
    nc = seq // q
    row = lambda b, c: b * nc + c
    const = lambda b, c: (0, 0)
    return pl.pallas_call(
        _ssd_kernel,
        grid=(nb, nc),
        in_specs=[pl.BlockSpec((q, SSM_CONV_DIM), lambda b, c: (row(b, c), OFF_XBC // SSM_CONV_DIM)),
                  pl.BlockSpec((q, D_SSM), lambda b, c: (row(b, c), OFF_ZS // D_SSM)),
                  pl.BlockSpec((q, LANES), lambda b, c: (row(b, c), 0)),
                  pl.BlockSpec((SUBLANES, SSM_CONV_DIM), const),
                  pl.BlockSpec((1, SSM_CONV_DIM), const),
                  pl.BlockSpec((1, LANES), const),
                  pl.BlockSpec((1, LANES), const),
                  pl.BlockSpec((1, D_SSM), const),
                  pl.BlockSpec((1, D_SSM), const),
                  pl.BlockSpec((LANES, D_SSM), const)],
        out_specs=[pl.BlockSpec((q, D_SSM), lambda b, c: (row(b, c), 0)),
                   pl.BlockSpec((1, D_SSM, N_SSM), lambda b, c: (b, 0, 0)),
                   pl.BlockSpec((1, SUBLANES, SSM_CONV_DIM), lambda b, c: (b, 0, 0))],
        out_shape=[jax.ShapeDtypeStruct((nb * seq, D_SSM), BF16),
                   jax.ShapeDtypeStruct((nb, D_SSM, N_SSM), F32),
                   jax.ShapeDtypeStruct((nb, SUBLANES, SSM_CONV_DIM), F32)],
        scratch_shapes=[pltpu.VMEM((2 * SUBLANES, SSM_CONV_DIM), BF16),
                        pltpu.VMEM((N_SSM, D_SSM), F32)],
        compiler_params=_cparams(2),
        name="ssd_prompt",
    )(z, z, dtr, p["ssm_conv_w"], p["ssm_conv_b"], p["dt_bias"], p["a_log"], p["d_x"], p["ssm_norm"],
      p["expand"])


def _swa_kernel(sink_ref, q_ref, k_ref, v_ref, c_ref, s1_ref, s2_ref, o_ref, kn_ref, vn_ref,
                kp_scr, vp_scr):
    qb = pl.program_id(1)
    w = WINDOW
    kvw = KV_ATT * HD_ATT

    @pl.when(qb == 0)
    def _():
        kp_scr[...] = jnp.zeros((w, kvw), F32)
        vp_scr[...] = jnp.zeros((w, kvw), F32)

    c, s1, s2 = c_ref[...], s1_ref[...], s2_ref[...]
    qv = _rope(q_ref[...].astype(F32), c, s1, s2) * (HD_ATT ** -0.5)
    ko = _rope(k_ref[...].astype(F32), c, s1, s2)
    vo = v_ref[...].astype(F32)
    kn_ref[0] = ko
    vn_ref[0] = vo
    kcat = jnp.concatenate([kp_scr[...], ko], axis=0)
    vcat = jnp.concatenate([vp_scr[...], vo], axis=0)
    kp_scr[...] = ko
    vp_scr[...] = vo
    vcat_t = vcat.T.astype(BF16)

    rep = H_ATT // KV_ATT
    keyj = lax.broadcasted_iota(jnp.int32, (2 * w, w), 0)
    qryl = lax.broadcasted_iota(jnp.int32, (2 * w, w), 1)
    valid = (keyj > qryl) & (keyj <= qryl + w) & ((qb > 0) | (keyj >= w))
    valid = jnp.concatenate([valid] * rep, axis=1)
    lane_kv = lax.broadcasted_iota(jnp.int32, (2 * w, LANES), 1)
    lane_q = lax.broadcasted_iota(jnp.int32, (w, LANES), 1)
    lo_kv, lo_q = lane_kv < HD_ATT, lane_q < HD_ATT

    o_t = []
    for slab in range(kvw // LANES):
        a_k = kcat[:, slab * LANES:(slab + 1) * LANES]
        b_k = pltpu.roll(a_k, HD_ATT, 1)
        for gi in range(2):
            g = 2 * slab + gi
            kdup = jnp.where(lo_kv, a_k, b_k) if gi == 0 else jnp.where(lo_kv, b_k, a_k)
            q_tiles, sink_tiles = [], []
            for pr in range(rep // 2):
                qp = qv[:, (rep // 2 * g + pr) * LANES:(rep // 2 * g + pr + 1) * LANES]
                q_tiles.append(jnp.where(lo_q, qp, 0.0))
                q_tiles.append(jnp.where(lo_q, 0.0, qp))
                for r2 in range(2):
                    sink_tiles.append(jnp.full((1, w), sink_ref[rep * g + 2 * pr + r2], F32))
            qs = jnp.concatenate(q_tiles, axis=0).astype(BF16)
            sink = jnp.concatenate(sink_tiles, axis=1)
            s = _dot_nt(kdup.astype(BF16), qs)
            s = jnp.where(valid, s, -jnp.inf)
            m = jnp.maximum(jnp.max(s, axis=0, keepdims=True), sink)
            e = jnp.exp(s - m)
            den = jnp.sum(e, axis=0, keepdims=True) + jnp.exp(sink - m)
            o_g = _dot(vcat_t[g * HD_ATT:(g + 1) * HD_ATT, :], e.astype(BF16))
            o_g = o_g * (1.0 / den)
            for r in range(rep):
                o_t.append(o_g[:, r * w:(r + 1) * w])
    o_ref[...] = jnp.concatenate(o_t, axis=0).T.astype(o_ref.dtype)


def _swa_prompt(z, p, nb, seq):
    w = WINDOW
    nq = seq // w
    kvw = KV_ATT * HD_ATT
    row = lambda b, i: b * nq + i
    tab = pl.BlockSpec((w, LANES), lambda b, i: (i, 0))
    return pl.pallas_call(
        _swa_kernel,
        grid=(nb, nq),
        in_specs=[pl.BlockSpec(memory_space=pltpu.SMEM),
                  pl.BlockSpec((w, H_ATT * HD_ATT), lambda b, i: (row(b, i), OFF_Q // (H_ATT * HD_ATT))),
                  pl.BlockSpec((w, kvw), lambda b, i: (row(b, i), OFF_K // kvw)),
                  pl.BlockSpec((w, kvw), lambda b, i: (row(b, i), OFF_V // kvw)),
                  tab, tab, tab],
        out_specs=[pl.BlockSpec((w, H_ATT * HD_ATT), lambda b, i: (row(b, i), 0)),
                   pl.BlockSpec((1, w, kvw), lambda b, i: (b, 0, 0)),
                   pl.BlockSpec((1, w, kvw), lambda b, i: (b, 0, 0))],
        out_shape=[jax.ShapeDtypeStruct((nb * seq, H_ATT * HD_ATT), BF16),
                   jax.ShapeDtypeStruct((nb, w, kvw), F32),
                   jax.ShapeDtypeStruct((nb, w, kvw), F32)],
        scratch_shapes=[pltpu.VMEM((w, kvw), F32), pltpu.VMEM((w, kvw), F32)],
        compiler_params=_cparams(2),
        name="swa_prompt",
    )(p["sinks"], z, z, z, p["rope_c"], p["rope_s1"], p["rope_s2"])


def _tail_body(u, gb, ys, oc, ga, gbg, gcg, x, wa_ref, ws_ref, wc_ref, wo_ref, gpost_ref):
    ta = (gb * u).astype(BF16)
    out = None
    half = D_MODEL // 2
    for lo in range(0, D_MODEL, half):
        hi = lo + half
        y_a = _dot(ta, wa_ref[:, lo:hi])
        y_b = _dot(ys, ws_ref[:, lo:hi])
        y_c = _dot(oc, wc_ref[:, lo:hi])
        merged = (_sigmoid(ga[:, lo:hi]) * y_a + _sigmoid(gbg[:, lo:hi]) * y_b
                  + _sigmoid(gcg[:, lo:hi]) * y_c)
        d = _dot(merged.astype(BF16), wo_ref[lo:hi, :])
        out = d if out is None else out + d
    return x + _rms(out, gpost_ref[...])


def _tail_prompt_kernel(va_ref, gb_ref, gc_ref, ys_ref, oc_ref, ga_ref, gbg_ref, gcg_ref, x_ref,
                        cw_ref, wa_ref, ws_ref, wc_ref, wo_ref, gpost_ref, xo_ref, bufo_ref, cv_scr):
    s = pl.program_id(1)
    tm = x_ref.shape[0]
    pad = SUBLANES

    @pl.when(s == 0)
    def _():
        cv_scr[0:pad, :] = jnp.zeros((pad, D_A), F32)

    cv = gc_ref[...].astype(F32) * va_ref[...].astype(F32)
    cv_scr[pad:pad + tm, :] = cv
    cw = cw_ref[...]
    u = cv * cw[CONV_A_W - 1:CONV_A_W, :]
    for k in range(CONV_A_W - 1):
        off = pad - (CONV_A_W - 1) + k
        u = u + cv_scr[off:off + tm, :] * cw[k:k + 1, :]
    tail = cv_scr[tm:tm + pad, :]
    cv_scr[0:pad, :] = tail
    bufo_ref[0] = tail
    xo_ref[...] = _tail_body(u, gb_ref[...].astype(F32), ys_ref[...], oc_ref[...],
                             ga_ref[...].astype(F32), gbg_ref[...].astype(F32), gcg_ref[...].astype(F32),
                             x_ref[...], wa_ref, ws_ref, wc_ref, wo_ref, gpost_ref)


def _tail_prompt(z, ys, oc, x, p, nb, seq, tm):
    ns = seq // tm
    row = lambda b, s: b * ns + s
    const = lambda b, s: (0, 0)
    zblk = lambda off: pl.BlockSpec((tm, D_MODEL), lambda b, s: (row(b, s), off // D_MODEL))
    return pl.pallas_call(
        _tail_prompt_kernel,
        grid=(nb, ns),
        in_specs=[zblk(OFF_ZA), zblk(OFF_ZA + D_A), zblk(OFF_ZA + 2 * D_A),
                  pl.BlockSpec((tm, D_SSM), lambda b, s: (row(b, s), 0)),
                  pl.BlockSpec((tm, H_ATT * HD_ATT), lambda b, s: (row(b, s), 0)),
                  zblk(OFF_GATES), zblk(OFF_GATES + D_MODEL), zblk(OFF_GATES + 2 * D_MODEL),
                  pl.BlockSpec((tm, D_MODEL), lambda b, s: (row(b, s), 0)),
                  pl.BlockSpec((SUBLANES, D_A), const),
                  _resident((D_A, D_MODEL), const),
                  _resident((D_SSM, D_MODEL), const),
                  _resident((H_ATT * HD_ATT, D_MODEL), const),
                  _resident((D_MODEL, D_MODEL), const),
                  pl.BlockSpec((1, D_MODEL), const)],
        out_specs=[pl.BlockSpec((tm, D_MODEL), lambda b, s: (row(b, s), 0)),
                   pl.BlockSpec((1, SUBLANES, D_A), lambda b, s: (b, 0, 0))],
        out_shape=[jax.ShapeDtypeStruct((nb * seq, D_MODEL), F32),
                   jax.ShapeDtypeStruct((nb, SUBLANES, D_A), F32)],
        scratch_shapes=[pltpu.VMEM((tm + SUBLANES, D_A), F32)],
        compiler_params=_cparams(2, VMEM_LIMIT_BIG),
        name="tail_prompt",
    )(z, z, z, ys, oc, z, z, z, x, p["conv_a_w"], p["w_a_out"], p["w_ssm_out"], p["w_attn_out"],
      p["w_out"], p["norm_mix_post"])


def _tail_sample_kernel(va_ref, gb_ref, gc_ref, ys_ref, oc_ref, ga_ref, gbg_ref, gcg_ref, x_ref,
                        b0_ref, b1_ref, cw_ref, wa_ref, ws_ref, wc_ref, wo_ref, gpost_ref,
                        xo_ref, cvo_ref):
    cv = gc_ref[...].astype(F32) * va_ref[...].astype(F32)
    cw = cw_ref[...]
    u = b0_ref[...] * cw[0:1, :] + b1_ref[...] * cw[1:2, :] + cv * cw[2:3, :]
    cvo_ref[...] = cv
    xo_ref[...] = _tail_body(u, gb_ref[...].astype(F32), ys_ref[...], oc_ref[...],
                             ga_ref[...].astype(F32), gbg_ref[...].astype(F32), gcg_ref[...].astype(F32),
                             x_ref[...], wa_ref, ws_ref, wc_ref, wo_ref, gpost_ref)


def _tail_sample(z, ys, oc, x, b0, b1, p):
    m = x.shape[0]
    const = lambda i: (0, 0)
    zblk = lambda off: pl.BlockSpec((m, D_MODEL), lambda i: (0, off // D_MODEL))
    full = lambda a: pl.BlockSpec(a.shape, const)
    w_c = p["w_attn_out_exp"]
    return pl.pallas_call(
        _tail_sample_kernel,
        grid=(1,),
        in_specs=[zblk(OFF_ZA), zblk(OFF_ZA + D_A), zblk(OFF_ZA + 2 * D_A),
                  full(ys), full(oc),
                  zblk(OFF_GATES), zblk(OFF_GATES + D_MODEL), zblk(OFF_GATES + 2 * D_MODEL),
                  full(x), full(b0), full(b1),
                  full(p["conv_a_w"]), full(p["w_a_out"]), full(p["w_ssm_out"]), full(w_c),
                  full(p["w_out"]), full(p["norm_mix_post"])],
        out_specs=[pl.BlockSpec((m, D_MODEL), const), pl.BlockSpec((m, D_A), const)],
        out_shape=[jax.ShapeDtypeStruct((m, D_MODEL), F32), jax.ShapeDtypeStruct((m, D_A), F32)],
        compiler_params=_cparams(1, VMEM_LIMIT_BIG),
        name="tail_sample",
    )(z, z, z, ys, oc, z, z, z, x, b0, b1, p["conv_a_w"], p["w_a_out"], p["w_ssm_out"], w_c,
      p["w_out"], p["norm_mix_post"])


def _xattn_prompt_kernel(x_ref, k_ref, v_ref, gpre_ref, wq_ref, wo_ref, gpost_ref, xo_ref):
    x = x_ref[...]
    h = _rms(x, gpre_ref[...]).astype(BF16)
    qv = (_dot(h, wq_ref[...]) * (X_HD ** -0.5)).astype(BF16)
    o_tiles = []
    for hh in range(X_H):
        kh = k_ref[:, hh * X_HD:(hh + 1) * X_HD].astype(BF16)
        vh = v_ref[:, hh * X_HD:(hh + 1) * X_HD].astype(BF16)
        s = _dot_nt(qv[:, hh * X_HD:(hh + 1) * X_HD], kh)
        m = jnp.max(s, axis=-1, keepdims=True)
        e = jnp.exp(s - m)
        pb = (e * (1.0 / jnp.sum(e, axis=-1, keepdims=True))).astype(BF16)
        o_tiles.append(_dot(pb, vh))
    o = jnp.concatenate(o_tiles, axis=1).astype(BF16)
    xo_ref[...] = x + _rms(_dot(o, wo_ref[...]), gpost_ref[...])


def _xattn_prompt(x, mk, mv, p, nb, seq, mem_len, tm):
    ns = seq // tm
    const = lambda b, s: (0, 0)
    return pl.pallas_call(
        _xattn_prompt_kernel,
        grid=(nb, ns),
        in_specs=[pl.BlockSpec((tm, D_MODEL), lambda b, s: (b * ns + s, 0)),
                  pl.BlockSpec((mem_len, X_H * X_HD), lambda b, s: (b, 0)),
                  pl.BlockSpec((mem_len, X_H * X_HD), lambda b, s: (b, 0)),
                  pl.BlockSpec((1, D_MODEL), const),
                  pl.BlockSpec((D_MODEL, X_H * X_HD), const),
                  pl.BlockSpec((X_H * X_HD, D_MODEL), const),
                  pl.BlockSpec((1, D_MODEL), const)],
        out_specs=pl.BlockSpec((tm, D_MODEL), lambda b, s: (b * ns + s, 0)),
        out_shape=jax.ShapeDtypeStruct((nb * seq, D_MODEL), F32),
        compiler_params=_cparams(2),
        name="xattn_prompt",
    )(x, mk, mv, p["norm_x_pre"], p["w_xq"], p["w_xo"], p["norm_x_post"])


def _ffn_finish(kf, part, x_ref, gpost_ref, xo_ref, acc_scr):
    @pl.when(kf == 0)
    def _():
        acc_scr[...] = part

    @pl.when(kf > 0)
    def _():
        acc_scr[...] = acc_scr[...] + part

    @pl.when(kf == pl.num_programs(2) - 1)
    def _():
        xo_ref[...] = x_ref[...] + _rms(acc_scr[...], gpost_ref[...])


def _ffn_prompt_kernel(x_ref, gpre_ref, win_ref, cw_ref, cb_ref, wo_ref, gpost_ref,
                       xo_ref, bufo_ref, a_scr):
    s = pl.program_id(1)
    tm = x_ref.shape[0]
    pad = SUBLANES

    @pl.when(s == 0)
    def _():
        a_scr[0:pad, :] = jnp.zeros((pad, D_FF), F32)

    x = x_ref[...]
    h = _rms(x, gpre_ref[...]).astype(BF16)
    cw = cw_ref[...]
    cb = cb_ref[...]

    def up(lo, hi):
        return _dot(h, win_ref[:, lo:hi]), _dot(h, win_ref[:, D_FF + lo:D_FF + hi])

    bounds = [(lo, min(lo + FFN_SUB, D_FF)) for lo in range(0, D_FF, FFN_SUB)]
    out = None
    nxt = up(*bounds[0])
    for i, (lo, hi) in enumerate(bounds):
        a, gate = nxt
        if i + 1 < len(bounds):
            nxt = up(*bounds[i + 1])
        a_scr[pad:pad + tm, lo:hi] = a
        ac = cb[:, lo:hi] + a * cw[FFN_CONV_W - 1:FFN_CONV_W, lo:hi]
        for k in range(FFN_CONV_W - 1):
            off = pad - (FFN_CONV_W - 1) + k
            ac = ac + a_scr[off:off + tm, lo:hi] * cw[k:k + 1, lo:hi]
        d = _dot((_silu(ac) * gate).astype(BF16), wo_ref[lo:hi, :])
        out = d if out is None else out + d
    tail = a_scr[tm:tm + pad, :]
    a_scr[0:pad, :] = tail
    bufo_ref[0] = tail
    xo_ref[...] = x + _rms(out, gpost_ref[...])


def _ffn_prompt(x, p, nb, seq, tm):
    ns = seq // tm
    const = lambda b, s: (0, 0)
    return pl.pallas_call(
        _ffn_prompt_kernel,
        grid=(nb, ns),
        in_specs=[pl.BlockSpec((tm, D_MODEL), lambda b, s: (b * ns + s, 0)),
                  pl.BlockSpec((1, D_MODEL), const),
                  _resident((D_MODEL, 2 * D_FF), const),
                  pl.BlockSpec((SUBLANES, D_FF), const),
                  pl.BlockSpec((1, D_FF), const),
                  _resident((D_FF, D_MODEL), const),
                  pl.BlockSpec((1, D_MODEL), const)],
        out_specs=[pl.BlockSpec((tm, D_MODEL), lambda b, s: (b * ns + s, 0)),
                   pl.BlockSpec((1, SUBLANES, D_FF), lambda b, s: (b, 0, 0))],
        out_shape=[jax.ShapeDtypeStruct((nb * seq, D_MODEL), F32),
                   jax.ShapeDtypeStruct((nb, SUBLANES, D_FF), F32)],
        scratch_shapes=[pltpu.VMEM((tm + SUBLANES, D_FF), F32)],
        compiler_params=_cparams(2, VMEM_LIMIT_BIG),
        name="ffn_prompt",
    )(x, p["norm_ffn_pre"], p["w_ffn_in"], p["ffn_conv_w"], p["ffn_conv_b"],
      p["w_ffn_out"], p["norm_ffn_post"])


def _ffn_sample_kernel(x_ref, gpre_ref, wa_ref, wg_ref, cw_ref, cb_ref, b0_ref, b1_ref, wo_ref,
                       gpost_ref, xo_ref, ao_ref, h_scr, acc_scr):
    kf = pl.program_id(2)

    @pl.when(kf == 0)
    def _():
        h_scr[...] = _rms(x_ref[...], gpre_ref[...]).astype(BF16)

    h = h_scr[...]
    a = _dot(h, wa_ref[...])
    gate = _dot(h, wg_ref[...])
    ao_ref[...] = a
    cw = cw_ref[...]
    ac = cb_ref[...] + b0_ref[...] * cw[0:1, :] + b1_ref[...] * cw[1:2, :] + a * cw[2:3, :]
    part = _dot((_silu(ac) * gate).astype(BF16), wo_ref[...])
    _ffn_finish(kf, part, x_ref, gpost_ref, xo_ref, acc_scr)


def _ffn_sample(x, b0, b1, p):
    m = x.shape[0]
    nk = D_FF // FF_CHUNK
    const = lambda b, s, k: (0, 0)
    chunk = lambda b, s, k: (0, k)
    return pl.pallas_call(
        _ffn_sample_kernel,
        grid=(1, 1, nk),
        in_specs=[pl.BlockSpec((m, D_MODEL), const),
                  pl.BlockSpec((1, D_MODEL), const),
                  pl.BlockSpec((D_MODEL, FF_CHUNK), chunk),
                  pl.BlockSpec((D_MODEL, FF_CHUNK), lambda b, s, k: (0, nk + k)),
                  pl.BlockSpec((SUBLANES, FF_CHUNK), chunk),
                  pl.BlockSpec((1, FF_CHUNK), chunk),
                  pl.BlockSpec((m, FF_CHUNK), chunk),
                  pl.BlockSpec((m, FF_CHUNK), chunk),
                  pl.BlockSpec((FF_CHUNK, D_MODEL), lambda b, s, k: (k, 0)),
                  pl.BlockSpec((1, D_MODEL), const)],
        out_specs=[pl.BlockSpec((m, D_MODEL), const),
                   pl.BlockSpec((m, FF_CHUNK), chunk)],
        out_shape=[jax.ShapeDtypeStruct((m, D_MODEL), F32),
                   jax.ShapeDtypeStruct((m, D_FF), F32)],
        scratch_shapes=[pltpu.VMEM((m, D_MODEL), BF16),
                        pltpu.VMEM((m, D_MODEL), F32)],
        compiler_params=_cparams(3),
        name="ffn_sample",
    )(x, p["norm_ffn_pre"], p["w_ffn_in"], p["w_ffn_in"], p["ffn_conv_w"], p["ffn_conv_b"], b0, b1,
      p["w_ffn_out"], p["norm_ffn_post"])


def _dec_prep_kernel(xbc_ref, dtr_ref, b0_ref, b1_ref, b2_ref, cw_ref, cb_ref, dtb_ref, alog_ref,
                     dx_ref, e_ref, xdt_ref, dec_ref, bs_ref, cs_ref, xsd_ref):
    cw = cw_ref[...]
    xc = (cb_ref[...] + b0_ref[...] * cw[0:1, :] + b1_ref[...] * cw[1:2, :] + b2_ref[...] * cw[2:3, :]
          + xbc_ref[...].astype(F32) * cw[3:4, :])
    xc = _silu(xc)
    xs = xc[:, :D_SSM]
    bs_ref[...] = xc[:, D_SSM:D_SSM + G_SSM * N_SSM]
    cs_ref[...] = xc[:, D_SSM + G_SSM * N_SSM:]
    dt = _softplus(dtr_ref[...] + dtb_ref[...])
    dec_ref[...] = jnp.exp(dt * (-jnp.exp(alog_ref[...])))
    xdt_ref[...] = xs * _dot_exact_lhs(dt, e_ref[...])
    xsd_ref[...] = xs * dx_ref[...]


def _dec_prep(z, dtr, b0, b1, b2, p):
    m = dtr.shape[0]
    const = lambda i: (0, 0)
    full = lambda a: pl.BlockSpec(a.shape, const)
    return pl.pallas_call(
        _dec_prep_kernel,
        grid=(1,),
        in_specs=[pl.BlockSpec((m, SSM_CONV_DIM), lambda i: (0, OFF_XBC // SSM_CONV_DIM)),
                  full(dtr), full(b0), full(b1), full(b2), full(p["ssm_conv_w"]), full(p["ssm_conv_b"]),
                  full(p["dt_bias"]), full(p["a_log"]), full(p["d_x"]), full(p["expand"])],
        out_specs=[pl.BlockSpec((m, D_SSM), const), pl.BlockSpec((m, LANES), const),
                   pl.BlockSpec((m, G_SSM * N_SSM), const), pl.BlockSpec((m, G_SSM * N_SSM), const),
                   pl.BlockSpec((m, D_SSM), const)],
        out_shape=[jax.ShapeDtypeStruct((m, D_SSM), F32), jax.ShapeDtypeStruct((m, LANES), F32),
                   jax.ShapeDtypeStruct((m, G_SSM * N_SSM), F32),
                   jax.ShapeDtypeStruct((m, G_SSM * N_SSM), F32),
                   jax.ShapeDtypeStruct((m, D_SSM), F32)],
        compiler_params=_cparams(1),
        name="dec_prep",
    )(z, dtr, b0, b1, b2, p["ssm_conv_w"], p["ssm_conv_b"], p["dt_bias"], p["a_log"], p["d_x"],
      p["expand"])


def _dec_state_kernel(dec_ref, st_ref, xdt_ref, bs_ref, cs_ref, *rest):
    so_ref, y_ref = rest[-2:]
    b = pl.program_id(0)
    rows_per_group = D_SSM // G_SSM
    grp = lax.broadcasted_iota(jnp.int32, (G_SSM, D_SSM), 0)
    own = (lax.broadcasted_iota(jnp.int32, (G_SSM, D_SSM), 1) // rows_per_group) == grp
    x = xdt_ref[pl.ds(b, 1), :]
    x8 = jnp.where(own, jnp.broadcast_to(x, own.shape), 0.0).astype(BF16)
    outer = lax.dot_general(x8, bs_ref[b].astype(BF16), (((0,), (0,)), ((), ())),
                            preferred_element_type=F32)
    for h in range(H_SSM):
        lo, hi = h * SSM_HEADDIM, (h + 1) * SSM_HEADDIM
        so_ref[0, lo:hi, :] = st_ref[0, lo:hi, :] * dec_ref[b, h] + outer[lo:hi, :]
    y8 = _dot_nt(cs_ref[b].astype(BF16), so_ref[0].astype(BF16))
    y_ref[pl.ds(b, 1), :] = jnp.sum(jnp.where(own, y8, 0.0), axis=0, keepdims=True)


def _dec_state(state_all, layer, dec, xdt, bs3, cs3, carry):
    nb = xdt.shape[0]
    const2 = lambda b: (0, 0)
    const3 = lambda b: (0, 0, 0)
    slab = lambda b: (layer * nb + b, 0, 0)
    extra = [] if carry is None else [carry]
    return pl.pallas_call(
        _dec_state_kernel,
        grid=(nb,),
        in_specs=[pl.BlockSpec(memory_space=pltpu.SMEM),
                  pl.BlockSpec((1, D_SSM, N_SSM), slab),
                  pl.BlockSpec(xdt.shape, const2),
                  pl.BlockSpec(bs3.shape, const3), pl.BlockSpec(cs3.shape, const3)]
                 + [pl.BlockSpec(memory_space=pl.ANY)] * len(extra),
        out_specs=[pl.BlockSpec((1, D_SSM, N_SSM), slab),
                   pl.BlockSpec((nb, D_SSM), const2)],
        out_shape=[jax.ShapeDtypeStruct(state_all.shape, F32),
                   jax.ShapeDtypeStruct((nb, D_SSM), F32)],
        input_output_aliases={5: 0} if extra else {},
        compiler_params=_cparams(1),
        name="dec_state",
    )(dec, state_all, xdt, bs3, cs3, *extra)


def _dec_post_kernel(y_ref, xsd_ref, zs_ref, nrm_ref, o_ref):
    y = y_ref[...] + xsd_ref[...]
    y = y * _silu(zs_ref[...].astype(F32))
    o_ref[...] = _rms(y, nrm_ref[...]).astype(o_ref.dtype)


def _dec_post(yt, xsd, z, p):
    m = xsd.shape[0]
    const = lambda i: (0, 0)
    return pl.pallas_call(
        _dec_post_kernel,
        grid=(1,),
        in_specs=[pl.BlockSpec(yt.shape, const), pl.BlockSpec(xsd.shape, const),
                  pl.BlockSpec((m, D_SSM), lambda i: (0, OFF_ZS // D_SSM)),
                  pl.BlockSpec((1, D_SSM), const)],
        out_specs=pl.BlockSpec((m, D_SSM), const),
        out_shape=jax.ShapeDtypeStruct((m, D_SSM), BF16),
        compiler_params=_cparams(1),
        name="dec_post",
    )(yt, xsd, z, p["ssm_norm"])


def _dec_swa_kernel(qe_ref, kn_ref, vn_ref, ck_ref, cv_ref, c_ref, s1_ref, s2_ref, sink_ref, *rest):
    nk_ref, nv_ref, oe_ref = rest[-3:]
    bt = ck_ref.shape[0]
    w = ck_ref.shape[1]
    c, s1, s2 = c_ref[0:1, :], s1_ref[0:1, :], s2_ref[0:1, :]
    kn = _rope(kn_ref[...].astype(F32), c, s1, s2)
    vn = vn_ref[...].astype(F32)
    last = lax.broadcasted_iota(jnp.int32, (w, KV_ATT * HD_ATT), 0) == w - 1
    sink = sink_ref[:, 0:1]
    for i in range(bt):
        nk = jnp.where(last, kn[i:i + 1, :], pltpu.roll(ck_ref[i], w - 1, 0))
        nv = jnp.where(last, vn[i:i + 1, :], pltpu.roll(cv_ref[i], w - 1, 0))
        nk_ref[i] = nk
        nv_ref[i] = nv
        qe = _rope(qe_ref[i].astype(F32), c, s1, s2) * (HD_ATT ** -0.5)
        s = _dot_nt(qe.astype(BF16), nk.astype(BF16))
        m = jnp.maximum(jnp.max(s, axis=-1, keepdims=True), sink)
        e = jnp.exp(s - m)
        den = jnp.sum(e, axis=-1, keepdims=True) + jnp.exp(sink - m)
        pb = (e * (1.0 / den)).astype(BF16)
        oe_ref[i] = _dot(pb, nv.astype(BF16)).astype(oe_ref.dtype)


def _dec_swa(qe, z, ck_all, cv_all, layer, p, bt, carry):
    nb = qe.shape[0]
    _, w, kvw = ck_all.shape
    const = lambda i: (0, 0)
    slab = lambda i: (layer * (nb // bt) + i, 0, 0)
    tab = pl.BlockSpec((SUBLANES, LANES), const)
    extra = [] if carry is None else list(carry)
    return pl.pallas_call(
        _dec_swa_kernel,
        grid=(nb // bt,),
        in_specs=[pl.BlockSpec((bt, H_ATT, kvw), lambda i: (i, 0, 0)),
                  pl.BlockSpec((bt, kvw), lambda i: (i, OFF_K // kvw)),
                  pl.BlockSpec((bt, kvw), lambda i: (i, OFF_V // kvw)),
                  pl.BlockSpec((bt, w, kvw), slab),
                  pl.BlockSpec((bt, w, kvw), slab),
                  tab, tab, tab,
                  pl.BlockSpec((H_ATT, LANES), const)]
                 + [pl.BlockSpec(memory_space=pl.ANY)] * len(extra),
        out_specs=[pl.BlockSpec((bt, w, kvw), slab),
                   pl.BlockSpec((bt, w, kvw), slab),
                   pl.BlockSpec((bt, H_ATT, kvw), lambda i: (i, 0, 0))],
        out_shape=[jax.ShapeDtypeStruct(ck_all.shape, F32), jax.ShapeDtypeStruct(cv_all.shape, F32),
                   jax.ShapeDtypeStruct((nb, H_ATT, kvw), BF16)],
        input_output_aliases={9: 0, 10: 1} if extra else {},
        compiler_params=_cparams(1),
        name="dec_swa",
    )(qe, z, z, ck_all, cv_all, p["rope_c_s"], p["rope_s1_s"], p["rope_s2_s"], p["sinks_x"], *extra)


def _dec_xattn_kernel(q_ref, k_ref, v_ref, o_ref):
    bt = k_ref.shape[1]
    for i in range(bt):
        qh = q_ref[i] * (X_HD ** -0.5)
        s = jnp.sum(k_ref[0, i] * qh[None], axis=-1, keepdims=True)
        m = jnp.max(s, axis=0, keepdims=True)
        e = jnp.exp(s - m)
        pr = e * (1.0 / jnp.sum(e, axis=0, keepdims=True))
        o_ref[i] = jnp.sum(pr * v_ref[0, i], axis=0)


def _dec_xattn(qv, mem_k, mem_v, layer, bt):
    nb = qv.shape[0]
    _, _, mem_len, nh, hd = mem_k.shape
    slab = lambda i: (layer, i, 0, 0, 0)
    return pl.pallas_call(
        _dec_xattn_kernel,
        grid=(nb // bt,),
        in_specs=[pl.BlockSpec((bt, nh, hd), lambda i: (i, 0, 0)),
                  pl.BlockSpec((1, bt, mem_len, nh, hd), slab),
                  pl.BlockSpec((1, bt, mem_len, nh, hd), slab)],
        out_specs=pl.BlockSpec((bt, nh, hd), lambda i: (i, 0, 0)),
        out_shape=jax.ShapeDtypeStruct((nb, nh, hd), F32),
        compiler_params=_cparams(1),
        name="dec_xattn",
    )(qv, mem_k, mem_v)


def _pad_rows(w, rows=SUBLANES):
    return jnp.pad(w, ((0, rows - w.shape[0]), (0, 0)))


def _pad_lanes(v, lanes=LANES):
    return jnp.pad(v, (0, lanes - v.shape[0])).reshape(1, lanes)


def _rope_tables(pos):
    half = ROT_DIM // 2
    inv = ROPE_THETA ** (-jnp.arange(half, dtype=F32) / half)
    ang = pos.astype(F32)[:, None] * inv[None, :]
    cos, sin = jnp.cos(ang), jnp.sin(ang)
    n = pos.shape[0]
    ones = jnp.ones((n, HD_ATT - ROT_DIM), F32)
    zeros = jnp.zeros((n, HD_ATT - ROT_DIM), F32)
    zh = jnp.zeros((n, half), F32)
    c = jnp.concatenate([cos, cos, ones], axis=1)
    s1 = jnp.concatenate([-sin, zh, zeros], axis=1)
    s2 = jnp.concatenate([zh, sin, zeros], axis=1)
    rep = LANES // HD_ATT
    return tuple(jnp.tile(t, (1, rep)) for t in (c, s1, s2))


_BF16_WEIGHTS = ("w_in", "w_a_out", "w_ssm_out", "w_attn_out", "w_out", "w_xq", "w_xk", "w_xv", "w_xo",
                 "w_ffn_in", "w_ffn_out")


def _prep_layer(l, prm, wb, seq):
    w_in = wb["w_in"][l]
    o = np.cumsum([0, D_A, D_A, D_A, D_SSM, SSM_CONV_DIM, H_SSM, H_ATT * HD_ATT, KV_ATT * HD_ATT,
                   KV_ATT * HD_ATT, 3 * D_MODEL])
    col = lambda i: w_in[:, o[i]:o[i + 1]]
    w_main = jnp.concatenate([col(4), col(3), col(0), col(1), col(2), col(9), col(6), col(7), col(8)],
                             axis=1)
    w_dt = jnp.pad(col(5), ((0, 0), (0, LANES - H_SSM)))
    rep = H_ATT // KV_ATT
    wq, wc = col(6), wb["w_attn_out"][l]
    q_blocks, c_blocks = [], []
    for h in range(H_ATT):
        g = h // rep
        wq_h = wq[:, h * HD_ATT:(h + 1) * HD_ATT]
        wc_h = wc[h * HD_ATT:(h + 1) * HD_ATT, :]
        q_blocks += [jnp.zeros((D_MODEL, g * HD_ATT), BF16), wq_h,
                     jnp.zeros((D_MODEL, (KV_ATT - 1 - g) * HD_ATT), BF16)]
        c_blocks += [jnp.zeros((g * HD_ATT, D_MODEL), BF16), wc_h,
                     jnp.zeros(((KV_ATT - 1 - g) * HD_ATT, D_MODEL), BF16)]
    wq_exp = jnp.concatenate(q_blocks, axis=1)
    wc_exp = jnp.concatenate(c_blocks, axis=0)
    expand = jnp.asarray(np.kron(np.eye(LANES, H_SSM, dtype=np.float32),
                                 np.ones((1, SSM_HEADDIM), np.float32))).astype(BF16)
    rope_p = _rope_tables(jnp.arange(seq, dtype=jnp.int32))
    rope_s = _rope_tables(jnp.full((SUBLANES,), PAST_LEN, jnp.int32))
    return {
        "norm_mix_pre": prm["norm_mix_pre"][l].reshape(1, -1),
        "norm_mix_post": prm["norm_mix_post"][l].reshape(1, -1),
        "w_main": w_main, "w_dt": w_dt, "wq_exp": wq_exp,
        "conv_a_w": _pad_rows(prm["conv_a_w"][l]),
        "w_a_out": wb["w_a_out"][l],
        "ssm_conv_w": _pad_rows(prm["ssm_conv_w"][l]),
        "ssm_conv_b": prm["ssm_conv_b"][l].reshape(1, -1),
        "dt_bias": _pad_lanes(prm["ssm_dt_bias"][l]),
        "a_log": _pad_lanes(prm["ssm_a_log"][l]),
        "d_x": jnp.repeat(prm["ssm_d"][l], SSM_HEADDIM).reshape(1, -1),
        "ssm_norm": prm["ssm_norm"][l].reshape(1, -1),
        "expand": expand,
        "w_ssm_out": wb["w_ssm_out"][l],
        "sinks": prm["attn_sinks"][l],
        "sinks_x": jnp.broadcast_to(prm["attn_sinks"][l][:, None], (H_ATT, LANES)),
        "w_attn_out": wb["w_attn_out"][l],
        "w_attn_out_exp": wc_exp,
        "w_out": wb["w_out"][l],
        "rope_c": rope_p[0], "rope_s1": rope_p[1], "rope_s2": rope_p[2],
        "rope_c_s": rope_s[0], "rope_s1_s": rope_s[1], "rope_s2_s": rope_s[2],
        "norm_x_pre": prm["norm_x_pre"][l].reshape(1, -1),
        "norm_x_post": prm["norm_x_post"][l].reshape(1, -1),
        "norm_mem": prm["norm_mem"][l].reshape(1, -1),
        "w_xq": wb["w_xq"][l],
        "w_xk": wb["w_xk"][l],
        "w_xv": wb["w_xv"][l],
        "w_xo": wb["w_xo"][l],
        "norm_ffn_pre": prm["norm_ffn_pre"][l].reshape(1, -1),
        "norm_ffn_post": prm["norm_ffn_post"][l].reshape(1, -1),
        "w_ffn_in": wb["w_ffn_in"][l],
        "ffn_conv_w": _pad_rows(prm["ffn_conv_w"][l]),
        "ffn_conv_b": prm["ffn_conv_b"][l].reshape(1, -1),
        "w_ffn_out": wb["w_ffn_out"][l],
    }


def _pick_tile(n, pref):
    t = min(n, pref)
    while n % t:
        t //= 2
    return t


def _prompt_layer(x, mem, p, nb, seq, mem_len):
    rows = nb * seq
    z, dtr = _in_proj(x, p["norm_mix_pre"], p["w_main"], p["w_dt"], _pick_tile(rows, 1024), 2304)
    ys, h_last, cbuf = _ssd_prompt(z, dtr, p, nb, seq)
    oc, k_new, v_new = _swa_prompt(z, p, nb, seq)
    x, bufa = _tail_prompt(z, ys, oc, x, p, nb, seq, _pick_tile(seq, 512))
    tmem = _pick_tile(mem.shape[0], 512)
    mk = _norm_matmul(mem, p["norm_mem"], p["w_xk"], F32, tmem, X_H * X_HD)
    mv = _norm_matmul(mem, p["norm_mem"], p["w_xv"], F32, tmem, X_H * X_HD)
    x = _xattn_prompt(x, mk, mv, p, nb, seq, mem_len, _pick_tile(seq, 512))
    x, buff = _ffn_prompt(x, p, nb, seq, _pick_tile(seq, 512))
    state = (bufa[:, SUBLANES - (CONV_A_W - 1):],
             cbuf[:, SUBLANES - (SSM_CONV_W - 1):],
             h_last.reshape(nb, H_SSM, SSM_HEADDIM, N_SSM),
             k_new.reshape(nb, WINDOW, KV_ATT, HD_ATT),
             v_new.reshape(nb, WINDOW, KV_ATT, HD_ATT),
             buff[:, SUBLANES - (FFN_CONV_W - 1):],
             mk.reshape(nb, mem_len, X_H, X_HD),
             mv.reshape(nb, mem_len, X_H, X_HD))
    return x, state


def _sample_layer(x, layer, st, big, carry, p):
    buf_a, buf_ssm, buf_ffn = st
    ssm_all, swa_k_all, swa_v_all, mem_k_all, mem_v_all = big
    nb = x.shape[0]
    kvw = KV_ATT * HD_ATT
    z, dtr = _in_proj(x, p["norm_mix_pre"], p["w_main"], p["w_dt"], nb, 1536)
    qe = _norm_matmul(x, p["norm_mix_pre"], p["wq_exp"], BF16, nb, 1024).reshape(nb, H_ATT, kvw)
    xdt, dec, bs, cs, xsd = _dec_prep(z, dtr, buf_ssm[:, 0], buf_ssm[:, 1], buf_ssm[:, 2], p)
    new_ssm, y_ssd = _dec_state(ssm_all, layer, dec[:, :H_SSM], xdt, bs.reshape(nb, G_SSM, N_SSM),
                                cs.reshape(nb, G_SSM, N_SSM), None if carry is None else carry[0])
    ys = _dec_post(y_ssd, xsd, z, p)
    new_k, new_v, oe = _dec_swa(qe, z, swa_k_all, swa_v_all, layer, p, _pick_tile(nb, 16),
                                None if carry is None else carry[1:3])
    x, cv = _tail_sample(z, ys, oe.reshape(nb, H_ATT * kvw), x, buf_a[:, 0], buf_a[:, 1], p)
    qx = _norm_matmul(x, p["norm_x_pre"], p["w_xq"], F32, nb, 1024)
    ox = _dec_xattn(qx.reshape(nb, X_H, X_HD), mem_k_all, mem_v_all, layer, _pick_tile(nb, 4))
    x = _mm_norm_res(ox.reshape(nb, X_H * X_HD), p["w_xo"], p["norm_x_post"], x)
    x, a_up = _ffn_sample(x, buf_ffn[:, 0], buf_ffn[:, 1], p)
    x_raw = z[:, OFF_XBC:OFF_XBC + SSM_CONV_DIM].astype(F32)
    small = (jnp.stack([buf_a[:, 1], cv], axis=1),
             jnp.concatenate([buf_ssm[:, 1:], x_raw[:, None, :]], axis=1),
             jnp.stack([buf_ffn[:, 1], a_up], axis=1))
    return x, small, (new_ssm, new_k, new_v)


def kernel(x_prompt, x_sample, mem_prompt, state_conv_a, state_ssm_conv, state_ssm, cache_swa_k, cache_swa_v, cache_mem_k, cache_mem_v, state_ffn_conv, norm_mix_pre, norm_mix_post, w_in, conv_a_w, w_a_out, ssm_conv_w, ssm_conv_b, ssm_dt_bias, ssm_a_log, ssm_d, ssm_norm, w_ssm_out, attn_sinks, w_attn_out, w_out, norm_x_pre, norm_x_post, norm_mem, w_xq, w_xk, w_xv, w_xo, norm_ffn_pre, norm_ffn_post, w_ffn_in, ffn_conv_w, ffn_conv_b, w_ffn_out):
    prm = dict(norm_mix_pre=norm_mix_pre, norm_mix_post=norm_mix_post, w_in=w_in, conv_a_w=conv_a_w,
               w_a_out=w_a_out, ssm_conv_w=ssm_conv_w, ssm_conv_b=ssm_conv_b, ssm_dt_bias=ssm_dt_bias,
               ssm_a_log=ssm_a_log, ssm_d=ssm_d, ssm_norm=ssm_norm, w_ssm_out=w_ssm_out,
               attn_sinks=attn_sinks, w_attn_out=w_attn_out, w_out=w_out, norm_x_pre=norm_x_pre,
               norm_x_post=norm_x_post, norm_mem=norm_mem, w_xq=w_xq, w_xk=w_xk, w_xv=w_xv, w_xo=w_xo,
               norm_ffn_pre=norm_ffn_pre, norm_ffn_post=norm_ffn_post, w_ffn_in=w_ffn_in,
               ffn_conv_w=ffn_conv_w, ffn_conv_b=ffn_conv_b, w_ffn_out=w_ffn_out)
    nb, seq, d = x_prompt.shape
    ns = x_sample.shape[0]
    mem_len = mem_prompt.shape[1]
    depth = w_in.shape[0]
    assert x_sample.shape[1] == 1 and seq % WINDOW == 0 and seq % SSD_CHUNK == 0
    assert cache_swa_k.shape[2] == WINDOW and PAST_LEN >= WINDOW

    yp = x_prompt.reshape(nb * seq, d)
    ys = x_sample.reshape(ns, d)
    mem = mem_prompt.reshape(nb * mem_len, d)
    flat = lambda a, *tail: a.reshape((depth * ns,) + tail)
    kvw = KV_ATT * HD_ATT
    big = (flat(state_ssm, D_SSM, N_SSM),
           flat(cache_swa_k, cache_swa_k.shape[2], kvw), flat(cache_swa_v, cache_swa_v.shape[2], kvw),
           cache_mem_k, cache_mem_v)
    wb = {name: prm[name].astype(BF16) for name in _BF16_WEIGHTS}
    new_p, new_s, carry = [], [], None
    for l in range(depth):
        p = _prep_layer(l, prm, wb, seq)
        yp, st_p = _prompt_layer(yp, mem, p, nb, seq, mem_len)
        new_p.append(st_p)
        ys, st_s, carry = _sample_layer(ys, l, (state_conv_a[l], state_ssm_conv[l], state_ffn_conv[l]),
                                        big, carry, p)
        new_s.append(st_s)
    stack = lambda lst, i: jnp.stack([s[i] for s in lst])
    s_ssm, s_swa_k, s_swa_v = (carry[0].reshape(state_ssm.shape), carry[1].reshape(cache_swa_k.shape),
                               carry[2].reshape(cache_swa_v.shape))
    return ((yp.reshape(nb, seq, d), ys.reshape(ns, 1, d))
            + tuple(stack(new_p, i) for i in range(8))
            + (stack(new_s, 0), stack(new_s, 1), s_ssm, s_swa_k, s_swa_v, stack(new_s, 2)))
```

```python
import functools

import numpy as np
import jax
import jax.numpy as jnp
from jax import lax
from jax.experimental import pallas as pl
from jax.experimental.pallas import tpu as pltpu

F32 = jnp.float32
BF16 = jnp.bfloat16

EPS = 1e-6
D_MODEL = 1024
D_A = D_MODEL
CONV_A_W = 3
D_SSM = 2 * D_MODEL
SSM_HEADDIM = 64
H_SSM = D_SSM // SSM_HEADDIM
G_SSM = 8
N_SSM = 128
SSM_CONV_W = 4
SSM_CONV_DIM = D_SSM + 2 * G_SSM * N_SSM
SSD_CHUNK = 128
H_ATT = 16
KV_ATT = 4
HD_ATT = 64
ROT_DIM = HD_ATT // 4
ROPE_THETA = 500000.0
WINDOW = 128
X_H = 4
X_HD = D_MODEL // X_H
D_FF = 2816
FFN_CONV_W = 3
PAST_LEN = 8192
LOG2E = 1.4426950408889634

LANES = 128
SUBLANES = 8
VMEM_LIMIT = 48 * 1024 * 1024
VMEM_LIMIT_BIG = 56 * 1024 * 1024

OFF_XBC = 0
OFF_ZS = OFF_XBC + SSM_CONV_DIM
OFF_ZA = OFF_ZS + D_SSM
OFF_GATES = OFF_ZA + 3 * D_A
OFF_Q = OFF_GATES + 3 * D_MODEL
OFF_K = OFF_Q + H_ATT * HD_ATT
OFF_V = OFF_K + KV_ATT * HD_ATT
N_MAIN = OFF_V + KV_ATT * HD_ATT
FF_CHUNK = D_FF // 2
FFN_SUB = 4 * LANES


def _cparams(n_axes, vmem=VMEM_LIMIT):
    return pltpu.CompilerParams(dimension_semantics=("arbitrary",) * n_axes,
                                vmem_limit_bytes=vmem)


def _resident(shape, index_map):
    return pl.BlockSpec(shape, index_map, pipeline_mode=pl.Buffered(1))


def _wspec(layer, shape, index_map, resident=False):
    imap = lambda *g: (layer,) + tuple(index_map(*g))
    if resident:
        return pl.BlockSpec((None,) + tuple(shape), imap, pipeline_mode=pl.Buffered(1))
    return pl.BlockSpec((None,) + tuple(shape), imap)


def _rms(x, g):
    return x * lax.rsqrt(jnp.mean(x * x, axis=-1, keepdims=True) + EPS) * g


def _sigmoid(x):
    return 0.5 + 0.5 * jnp.tanh(0.5 * x)


def _silu(x):
    hx = 0.5 * x
    return hx + hx * jnp.tanh(hx)


def _softplus(x):
    return jnp.maximum(x, 0.0) + jnp.log(1.0 + jnp.exp(-jnp.abs(x)))


def _dot(a, b):
    return jnp.dot(a, b, preferred_element_type=F32)


def _dot_nt(a, b):
    return lax.dot_general(a, b, (((1,), (1,)), ((), ())), preferred_element_type=F32)


def _dot_exact_rhs(a01, v):
    hi = v.astype(BF16)
    r1 = v - hi.astype(F32)
    mid = r1.astype(BF16)
    lo = (r1 - mid.astype(F32)).astype(BF16)
    return _dot(a01, hi) + _dot(a01, mid) + _dot(a01, lo)


def _dot_exact_lhs(v, b01):
    hi = v.astype(BF16)
    r1 = v - hi.astype(F32)
    mid = r1.astype(BF16)
    lo = (r1 - mid.astype(F32)).astype(BF16)
    return _dot(hi, b01) + _dot(mid, b01) + _dot(lo, b01)


def _rope(x, c, s1, s2):
    outs = []
    for i in range(x.shape[1] // LANES):
        xi = x[:, i * LANES:(i + 1) * LANES]
        outs.append(xi * c + pltpu.roll(xi, LANES - ROT_DIM // 2, 1) * s1
                    + pltpu.roll(xi, ROT_DIM // 2, 1) * s2)
    return outs[0] if len(outs) == 1 else jnp.concatenate(outs, axis=1)


def _norm_mm_kernel(x_ref, g_ref, w_ref, o_ref, h_scr):
    @pl.when(pl.program_id(1) == 0)
    def _():
        h_scr[...] = _rms(x_ref[...], g_ref[...]).astype(BF16)

    o_ref[...] = _dot(h_scr[...], w_ref[...]).astype(o_ref.dtype)


def _norm_matmul(x, g, w, layer, out_dtype, tm, tn):
    m, k = x.shape
    n = w.shape[2]
    return pl.pallas_call(
        _norm_mm_kernel,
        grid=(m // tm, n // tn),
        in_specs=[pl.BlockSpec((tm, k), lambda i, j: (i, 0)),
                  pl.BlockSpec((1, k), lambda i, j: (0, 0)),
                  _wspec(layer, (k, tn), lambda i, j: (0, j))],
        out_specs=pl.BlockSpec((tm, tn), lambda i, j: (i, j)),
        out_shape=jax.ShapeDtypeStruct((m, n), out_dtype),
        scratch_shapes=[pltpu.VMEM((tm, k), BF16)],
        compiler_params=_cparams(2),
        name="norm_matmul",
    )(x, g, w)


def _inproj_kernel(x_ref, g_ref, w_ref, wdt_ref, z_ref, dt_ref, h_scr):
    @pl.when(pl.program_id(1) == 0)
    def _():
        h = _rms(x_ref[...], g_ref[...]).astype(BF16)
        h_scr[...] = h
        dt_ref[...] = _dot(h, wdt_ref[...])

    z_ref[...] = _dot(h_scr[...], w_ref[...]).astype(z_ref.dtype)


def _in_proj(x, g, w, wdt, layer, tm, tn):
    m, k = x.shape
    n = w.shape[2]
    return pl.pallas_call(
        _inproj_kernel,
        grid=(m // tm, n // tn),
        in_specs=[pl.BlockSpec((tm, k), lambda i, j: (i, 0)),
                  pl.BlockSpec((1, k), lambda i, j: (0, 0)),
                  _wspec(layer, (k, tn), lambda i, j: (0, j)),
                  _wspec(layer, (k, LANES), lambda i, j: (0, 0))],
        out_specs=[pl.BlockSpec((tm, tn), lambda i, j: (i, j)),
                   pl.BlockSpec((tm, LANES), lambda i, j: (i, 0))],
        out_shape=[jax.ShapeDtypeStruct((m, n), BF16),
                   jax.ShapeDtypeStruct((m, LANES), F32)],
        scratch_shapes=[pltpu.VMEM((tm, k), BF16)],
        compiler_params=_cparams(2),
        name="in_proj",
    )(x, g, w, wdt)


def _mm_norm_res_kernel(a_ref, w_ref, g_ref, x_ref, o_ref):
    y = _dot(a_ref[...].astype(BF16), w_ref[...])
    o_ref[...] = x_ref[...] + _rms(y, g_ref[...])


def _mm_norm_res(a, w, layer, g, x):
    m, k = a.shape
    n = w.shape[2]
    return pl.pallas_call(
        _mm_norm_res_kernel,
        grid=(1,),
        in_specs=[pl.BlockSpec((m, k), lambda i: (0, 0)),
                  _wspec(layer, (k, n), lambda i: (0, 0)),
                  pl.BlockSpec((1, n), lambda i: (0, 0)),
                  pl.BlockSpec((m, n), lambda i: (0, 0))],
        out_specs=pl.BlockSpec((m, n), lambda i: (0, 0)),
        out_shape=jax.ShapeDtypeStruct((m, n), F32),
        compiler_params=_cparams(1),
        name="mm_norm_res",
    )(a, w, g, x)


def _ssd_kernel(xbc_ref, zs_ref, dtr_ref, cw_ref, cb_ref, dtb_ref, alog_ref, dx_ref, nrm_ref, e_ref,
                y_ref, hout_ref, cbuf_ref, x_scr, ht_scr):
    c = pl.program_id(1)
    q = xbc_ref.shape[0]
    pad = SUBLANES

    halo = x_scr.shape[0]

    @pl.when(c == 0)
    def _():
        x_scr[...] = jnp.zeros(x_scr.shape, BF16)
        ht_scr[...] = jnp.zeros(ht_scr.shape, F32)

    xb = xbc_ref[...]
    xin = xb.astype(F32)
    ext = jnp.concatenate([x_scr[...], xb], axis=0)
    n_sh = SSM_CONV_W - 1
    ri = lax.broadcasted_iota(jnp.int32, (n_sh * q, halo + q), 0)
    ci = lax.broadcasted_iota(jnp.int32, (n_sh * q, halo + q), 1)
    shift_mat = jnp.where(ci == (ri % q) + halo - n_sh + ri // q, 1.0, 0.0).astype(BF16)
    shifted = _dot(shift_mat, ext)
    cw = cw_ref[...]
    xc = cb_ref[...] + xin * cw[n_sh:n_sh + 1, :]
    for k in range(n_sh):
        xc = xc + shifted[k * q:(k + 1) * q, :] * cw[k:k + 1, :]
    xc = _silu(xc)
    x_scr[...] = xb[q - halo:q, :]
    cbuf_ref[0] = xin[q - pad:q, :]

    xs = xc[:, :D_SSM]
    bs = xc[:, D_SSM:D_SSM + G_SSM * N_SSM]
    cs_in = xc[:, D_SSM + G_SSM * N_SSM:]

    dt = _softplus(dtr_ref[...] + dtb_ref[...])
    a = dt * (-jnp.exp(alog_ref[...]))
    row = lax.broadcasted_iota(jnp.int32, (q, q), 0)
    col = lax.broadcasted_iota(jnp.int32, (q, q), 1)
    tri = row >= col
    cs = _dot_exact_rhs(jnp.where(tri, 1.0, 0.0).astype(BF16), a)
    cs_last = cs[q - 1:q, :]
    cs2 = cs * LOG2E
    src_t = (cs2 - jnp.log2(dt)).T
    wdd = dt * jnp.exp(cs_last - cs)
    cdec = jnp.broadcast_to(jnp.exp(cs_last), (SUBLANES, LANES))
    cdec_x = _dot_exact_lhs(cdec, e_ref[...])[0:1, :]

    lane = lax.broadcasted_iota(jnp.int32, (q, LANES), 1)
    heads_per_group = H_SSM // G_SSM
    y_tiles = []
    for g in range(G_SSM):
        bg = bs[:, g * N_SSM:(g + 1) * N_SSM]
        cg = cs_in[:, g * N_SSM:(g + 1) * N_SSM]
        cb = _dot_nt(cg.astype(BF16), bg.astype(BF16))
        xdd_tiles = []
        for pr in range(heads_per_group // 2):
            hp = g * (heads_per_group // 2) + pr
            xpair = xs[:, hp * LANES:(hp + 1) * LANES]
            wp = jnp.concatenate([xpair, ht_scr[:, hp * LANES:(hp + 1) * LANES]], axis=0).astype(BF16)
            parts = []
            wsel = []
            for r2 in range(2):
                h = 2 * hp + r2
                csb = jnp.broadcast_to(cs2[:, h:h + 1], (q, LANES))
                mmat = cb * jnp.exp2(jnp.where(tri, csb - src_t[h:h + 1, :], -jnp.inf))
                csc = cg * jnp.exp2(csb)
                lhs = jnp.concatenate([mmat, csc], axis=1).astype(BF16)
                parts.append(_dot(lhs, wp))
                wsel.append(jnp.broadcast_to(wdd[:, h:h + 1], (q, LANES)))
            y_tiles.append(jnp.where(lane < SSM_HEADDIM, parts[0], parts[1]))
            xdd_tiles.append(xpair * jnp.where(lane < SSM_HEADDIM, wsel[0], wsel[1]))
        xdd = jnp.concatenate(xdd_tiles, axis=1).astype(BF16)
        st_t = _dot(bg.T.astype(BF16), xdd)
        lo, hi = g * heads_per_group * SSM_HEADDIM, (g + 1) * heads_per_group * SSM_HEADDIM
        ht_scr[:, lo:hi] = ht_scr[:, lo:hi] * cdec_x[:, lo:hi] + st_t

    y = jnp.concatenate(y_tiles, axis=1) + xs * dx_ref[...]
    y = y * _silu(zs_ref[...].astype(F32))
    y_ref[...] = _rms(y, nrm_ref[...]).astype(y_ref.dtype)

    @pl.when(c == pl.num_programs(1) - 1)
    def _():
        hout_ref[0] = ht_scr[...].T


def _ssd_prompt(z, dtr, p, nb, seq):
    q = SSD_CHUNK
    nc = seq // q
    row = lambda b, c: b * nc + c
    const = lambda b, c: (0, 0)
    return pl.pallas_call(
        _ssd_kernel,
        grid=(nb, nc),
        in_specs=[pl.BlockSpec((q, SSM_CONV_DIM), lambda b, c: (row(b, c), OFF_XBC // SSM_CONV_DIM)),
                  pl.BlockSpec((q, D_SSM), lambda b, c: (row(b, c), OFF_ZS // D_SSM)),
                  pl.BlockSpec((q, LANES), lambda b, c: (row(b, c), 0)),
                  pl.BlockSpec((SUBLANES, SSM_CONV_DIM), const),
                  pl.BlockSpec((1, SSM_CONV_DIM), const),
                  pl.BlockSpec((1, LANES), const),
                  pl.BlockSpec((1, LANES), const),
                  pl.BlockSpec((1, D_SSM), const),
                  pl.BlockSpec((1, D_SSM), const),
                  pl.BlockSpec((LANES, D_SSM), const)],
        out_specs=[pl.BlockSpec((q, D_SSM), lambda b, c: (row(b, c), 0)),
                   pl.BlockSpec((1, D_SSM, N_SSM), lambda b, c: (b, 0, 0)),
                   pl.BlockSpec((1, SUBLANES, SSM_CONV_DIM), lambda b, c: (b, 0, 0))],
        out_shape=[jax.ShapeDtypeStruct((nb * seq, D_SSM), BF16),
                   jax.ShapeDtypeStruct((nb, D_SSM, N_SSM), F32),
                   jax.ShapeDtypeStruct((nb, SUBLANES, SSM_CONV_DIM), F32)],
        scratch_shapes=[pltpu.VMEM((2 * SUBLANES, SSM_CONV_DIM), BF16),
                        pltpu.VMEM((N_SSM, D_SSM), F32)],
        compiler_params=_cparams(2),
        name="ssd_prompt",
    )(z, z, dtr, p["ssm_conv_w"], p["ssm_conv_b"], p["dt_bias"], p["a_log"], p["d_x"], p["ssm_norm"],
      p["expand"])


def _swa_kernel(sink_ref, q_ref, k_ref, v_ref, c_ref, s1_ref, s2_ref, o_ref, kn_ref, vn_ref,
                kp_scr, vp_scr):
    qb = pl.program_id(1)
    w = WINDOW
    kvw = KV_ATT * HD_ATT

    @pl.when(qb == 0)
    def _():
        kp_scr[...] = jnp.zeros((w, kvw), F32)
        vp_scr[...] = jnp.zeros((w, kvw), F32)

    c, s1, s2 = c_ref[...], s1_ref[...], s2_ref[...]
    qv = _rope(q_ref[...].astype(F32), c, s1, s2) * (HD_ATT ** -0.5)
    ko = _rope(k_ref[...].astype(F32), c, s1, s2)
    vo = v_ref[...].astype(F32)
    kn_ref[0] = ko
    vn_ref[0] = vo
    kcat = jnp.concatenate([kp_scr[...], ko], axis=0)
    vcat = jnp.concatenate([vp_scr[...], vo], axis=0)
    kp_scr[...] = ko
    vp_scr[...] = vo
    vcat_t = vcat.T.astype(BF16)

    rep = H_ATT // KV_ATT
    keyj = lax.broadcasted_iota(jnp.int32, (2 * w, w), 0)
    qryl = lax.broadcasted_iota(jnp.int32, (2 * w, w), 1)
    valid = (keyj > qryl) & (keyj <= qryl + w) & ((qb > 0) | (keyj >= w))
    valid = jnp.concatenate([valid] * rep, axis=1)
    lane_kv = lax.broadcasted_iota(jnp.int32, (2 * w, LANES), 1)
    lane_q = lax.broadcasted_iota(jnp.int32, (w, LANES), 1)
    lo_kv, lo_q = lane_kv < HD_ATT, lane_q < HD_ATT

    o_t = []
    for slab in range(kvw // LANES):
        a_k = kcat[:, slab * LANES:(slab + 1) * LANES]
        b_k = pltpu.roll(a_k, HD_ATT, 1)
        for gi in range(2):
            g = 2 * slab + gi
            kdup = jnp.where(lo_kv, a_k, b_k) if gi == 0 else jnp.where(lo_kv, b_k, a_k)
            q_tiles, sink_tiles = [], []
            for pr in range(rep // 2):
                qp = qv[:, (rep // 2 * g + pr) * LANES:(rep // 2 * g + pr + 1) * LANES]
                q_tiles.append(jnp.where(lo_q, qp, 0.0))
                q_tiles.append(jnp.where(lo_q, 0.0, qp))
                for r2 in range(2):
                    sink_tiles.append(jnp.full((1, w), sink_ref[rep * g + 2 * pr + r2], F32))
            qs = jnp.concatenate(q_tiles, axis=0).astype(BF16)
            sink = jnp.concatenate(sink_tiles, axis=1)
            s = _dot_nt(kdup.astype(BF16), qs)
            s = jnp.where(valid, s, -jnp.inf)
            m = jnp.maximum(jnp.max(s, axis=0, keepdims=True), sink)
            e = jnp.exp(s - m)
            den = jnp.sum(e, axis=0, keepdims=True) + jnp.exp(sink - m)
            o_g = _dot(vcat_t[g * HD_ATT:(g + 1) * HD_ATT, :], e.astype(BF16))
            o_g = o_g * (1.0 / den)
            for r in range(rep):
                o_t.append(o_g[:, r * w:(r + 1) * w])
    o_ref[...] = jnp.concatenate(o_t, axis=0).T.astype(o_ref.dtype)


def _swa_prompt(z, p, nb, seq):
    w = WINDOW
    nq = seq // w
    kvw = KV_ATT * HD_ATT
    row = lambda b, i: b * nq + i
    tab = pl.BlockSpec((w, LANES), lambda b, i: (i, 0))
    return pl.pallas_call(
        _swa_kernel,
        grid=(nb, nq),
        in_specs=[pl.BlockSpec(memory_space=pltpu.SMEM),
                  pl.BlockSpec((w, H_ATT * HD_ATT), lambda b, i: (row(b, i), OFF_Q // (H_ATT * HD_ATT))),
                  pl.BlockSpec((w, kvw), lambda b, i: (row(b, i), OFF_K // kvw)),
                  pl.BlockSpec((w, kvw), lambda b, i: (row(b, i), OFF_V // kvw)),
                  tab, tab, tab],
        out_specs=[pl.BlockSpec((w, H_ATT * HD_ATT), lambda b, i: (row(b, i), 0)),
                   pl.BlockSpec((1, w, kvw), lambda b, i: (b, 0, 0)),
                   pl.BlockSpec((1, w, kvw), lambda b, i: (b, 0, 0))],
        out_shape=[jax.ShapeDtypeStruct((nb * seq, H_ATT * HD_ATT), BF16),
                   jax.ShapeDtypeStruct((nb, w, kvw), F32),
                   jax.ShapeDtypeStruct((nb, w, kvw), F32)],
        scratch_shapes=[pltpu.VMEM((w, kvw), F32), pltpu.VMEM((w, kvw), F32)],
        compiler_params=_cparams(2),
        name="swa_prompt",
    )(p["sinks"], z, z, z, p["rope_c"], p["rope_s1"], p["rope_s2"])


def _tail_body(u, gb, ys, oc, ga, gbg, gcg, x, wa_ref, ws_ref, wc_ref, wo_ref, gpost_ref):
    ta = (gb * u).astype(BF16)
    out = None
    half = D_MODEL // 2
    for lo in range(0, D_MODEL, half):
        hi = lo + half
        y_a = _dot(ta, wa_ref[:, lo:hi])
        y_b = _dot(ys, ws_ref[:, lo:hi])
        y_c = _dot(oc, wc_ref[:, lo:hi])
        merged = (_sigmoid(ga[:, lo:hi]) * y_a + _sigmoid(gbg[:, lo:hi]) * y_b
                  + _sigmoid(gcg[:, lo:hi]) * y_c)
        d = _dot(merged.astype(BF16), wo_ref[lo:hi, :])
        out = d if out is None else out + d
    return x + _rms(out, gpost_ref[...])


def _tail_prompt_kernel(va_ref, gb_ref, gc_ref, ys_ref, oc_ref, ga_ref, gbg_ref, gcg_ref, x_ref,
                        cw_ref, wa_ref, ws_ref, wc_ref, wo_ref, gpost_ref, xo_ref, bufo_ref, cv_scr):
    s = pl.program_id(1)
    tm = x_ref.shape[0]
    pad = SUBLANES

    @pl.when(s == 0)
    def _():
        cv_scr[0:pad, :] = jnp.zeros((pad, D_A), F32)

    cv = gc_ref[...].astype(F32) * va_ref[...].astype(F32)
    cv_scr[pad:pad + tm, :] = cv
    cw = cw_ref[...]
    u = cv * cw[CONV_A_W - 1:CONV_A_W, :]
    for k in range(CONV_A_W - 1):
        off = pad - (CONV_A_W - 1) + k
        u = u + cv_scr[off:off + tm, :] * cw[k:k + 1, :]
    tail = cv_scr[tm:tm + pad, :]
    cv_scr[0:pad, :] = tail
    bufo_ref[0] = tail
    xo_ref[...] = _tail_body(u, gb_ref[...].astype(F32), ys_ref[...], oc_ref[...],
                             ga_ref[...].astype(F32), gbg_ref[...].astype(F32), gcg_ref[...].astype(F32),
                             x_ref[...], wa_ref, ws_ref, wc_ref, wo_ref, gpost_ref)


def _tail_prompt(z, ys, oc, x, p, nb, seq, tm):
    ns = seq // tm
    row = lambda b, s: b * ns + s
    const = lambda b, s: (0, 0)
    zblk = lambda off: pl.BlockSpec((tm, D_MODEL), lambda b, s: (row(b, s), off // D_MODEL))
    return pl.pallas_call(
        _tail_prompt_kernel,
        grid=(nb, ns),
        in_specs=[zblk(OFF_ZA), zblk(OFF_ZA + D_A), zblk(OFF_ZA + 2 * D_A),
                  pl.BlockSpec((tm, D_SSM), lambda b, s: (row(b, s), 0)),
                  pl.BlockSpec((tm, H_ATT * HD_ATT), lambda b, s: (row(b, s), 0)),
                  zblk(OFF_GATES), zblk(OFF_GATES + D_MODEL), zblk(OFF_GATES + 2 * D_MODEL),
                  pl.BlockSpec((tm, D_MODEL), lambda b, s: (row(b, s), 0)),
                  pl.BlockSpec((SUBLANES, D_A), const),
                  _wspec(p["layer"], (D_A, D_MODEL), const, resident=True),
                  _wspec(p["layer"], (D_SSM, D_MODEL), const, resident=True),
                  _wspec(p["layer"], (H_ATT * HD_ATT, D_MODEL), const, resident=True),
                  _wspec(p["layer"], (D_MODEL, D_MODEL), const, resident=True),
                  pl.BlockSpec((1, D_MODEL), const)],
        out_specs=[pl.BlockSpec((tm, D_MODEL), lambda b, s: (row(b, s), 0)),
                   pl.BlockSpec((1, SUBLANES, D_A), lambda b, s: (b, 0, 0))],
        out_shape=[jax.ShapeDtypeStruct((nb * seq, D_MODEL), F32),
                   jax.ShapeDtypeStruct((nb, SUBLANES, D_A), F32)],
        scratch_shapes=[pltpu.VMEM((tm + SUBLANES, D_A), F32)],
        compiler_params=_cparams(2, VMEM_LIMIT_BIG),
        name="tail_prompt",
    )(z, z, z, ys, oc, z, z, z, x, p["conv_a_w"], p["w_a_out"], p["w_ssm_out"], p["w_attn_out"],
      p["w_out"], p["norm_mix_post"])


def _tail_sample_kernel(va_ref, gb_ref, gc_ref, ys_ref, oc_ref, ga_ref, gbg_ref, gcg_ref, x_ref,
                        b0_ref, b1_ref, cw_ref, wa_ref, ws_ref, wc_ref, wo_ref, gpost_ref,
                        xo_ref, cvo_ref):
    cv = gc_ref[...].astype(F32) * va_ref[...].astype(F32)
    cw = cw_ref[...]
    u = b0_ref[...] * cw[0:1, :] + b1_ref[...] * cw[1:2, :] + cv * cw[2:3, :]
    cvo_ref[...] = cv
    xo_ref[...] = _tail_body(u, gb_ref[...].astype(F32), ys_ref[...], oc_ref[...],
                             ga_ref[...].astype(F32), gbg_ref[...].astype(F32), gcg_ref[...].astype(F32),
                             x_ref[...], wa_ref, ws_ref, wc_ref, wo_ref, gpost_ref)


def _tail_sample(z, ys, oc, x, b0, b1, p):
    m = x.shape[0]
    const = lambda i: (0, 0)
    zblk = lambda off: pl.BlockSpec((m, D_MODEL), lambda i: (0, off // D_MODEL))
    full = lambda a: pl.BlockSpec(a.shape, const)
    wfull = lambda a: _wspec(p["layer"], a.shape[1:], const)
    w_c = p["w_attn_out_exp"]
    return pl.pallas_call(
        _tail_sample_kernel,
        grid=(1,),
        in_specs=[zblk(OFF_ZA), zblk(OFF_ZA + D_A), zblk(OFF_ZA + 2 * D_A),
                  full(ys), full(oc),
                  zblk(OFF_GATES), zblk(OFF_GATES + D_MODEL), zblk(OFF_GATES + 2 * D_MODEL),
                  full(x), full(b0), full(b1),
                  full(p["conv_a_w"]), wfull(p["w_a_out"]), wfull(p["w_ssm_out"]), wfull(w_c),
                  wfull(p["w_out"]), full(p["norm_mix_post"])],
        out_specs=[pl.BlockSpec((m, D_MODEL), const), pl.BlockSpec((m, D_A), const)],
        out_shape=[jax.ShapeDtypeStruct((m, D_MODEL), F32), jax.ShapeDtypeStruct((m, D_A), F32)],
        compiler_params=_cparams(1, VMEM_LIMIT_BIG),
        name="tail_sample",
    )(z, z, z, ys, oc, z, z, z, x, b0, b1, p["conv_a_w"], p["w_a_out"], p["w_ssm_out"], w_c,
      p["w_out"], p["norm_mix_post"])


def _xattn_prompt_kernel(x_ref, k_ref, v_ref, gpre_ref, wq_ref, wo_ref, gpost_ref, xo_ref):
    x = x_ref[...]
    h = _rms(x, gpre_ref[...]).astype(BF16)
    qv = (_dot(h, wq_ref[...]) * (X_HD ** -0.5)).astype(BF16)
    o_tiles = []
    for hh in range(X_H):
        kh = k_ref[:, hh * X_HD:(hh + 1) * X_HD].astype(BF16)
        vh = v_ref[:, hh * X_HD:(hh + 1) * X_HD].astype(BF16)
        s = _dot_nt(qv[:, hh * X_HD:(hh + 1) * X_HD], kh)
        m = jnp.max(s, axis=-1, keepdims=True)
        e = jnp.exp(s - m)
        pb = (e * (1.0 / jnp.sum(e, axis=-1, keepdims=True))).astype(BF16)
        o_tiles.append(_dot(pb, vh))
    o = jnp.concatenate(o_tiles, axis=1).astype(BF16)
    xo_ref[...] = x + _rms(_dot(o, wo_ref[...]), gpost_ref[...])


def _xattn_prompt(x, mk, mv, p, nb, seq, mem_len, tm):
    ns = seq // tm
    const = lambda b, s: (0, 0)
    return pl.pallas_call(
        _xattn_prompt_kernel,
        grid=(nb, ns),
        in_specs=[pl.BlockSpec((tm, D_MODEL), lambda b, s: (b * ns + s, 0)),
                  pl.BlockSpec((mem_len, X_H * X_HD), lambda b, s: (b, 0)),
                  pl.BlockSpec((mem_len, X_H * X_HD), lambda b, s: (b, 0)),
                  pl.BlockSpec((1, D_MODEL), const),
                  _wspec(p["layer"], (D_MODEL, X_H * X_HD), const, resident=True),
                  _wspec(p["layer"], (X_H * X_HD, D_MODEL), const, resident=True),
                  pl.BlockSpec((1, D_MODEL), const)],
        out_specs=pl.BlockSpec((tm, D_MODEL), lambda b, s: (b * ns + s, 0)),
        out_shape=jax.ShapeDtypeStruct((nb * seq, D_MODEL), F32),
        compiler_params=_cparams(2),
        name="xattn_prompt",
    )(x, mk, mv, p["norm_x_pre"], p["w_xq"], p["w_xo"], p["norm_x_post"])


def _ffn_finish(kf, part, x_ref, gpost_ref, xo_ref, acc_scr):
    @pl.when(kf == 0)
    def _():
        acc_scr[...] = part

    @pl.when(kf > 0)
    def _():
        acc_scr[...] = acc_scr[...] + part

    @pl.when(kf == pl.num_programs(2) - 1)
    def _():
        xo_ref[...] = x_ref[...] + _rms(acc_scr[...], gpost_ref[...])


def _ffn_prompt_kernel(x_ref, gpre_ref, win_ref, cw_ref, cb_ref, wo_ref, gpost_ref,
                       xo_ref, bufo_ref, a_scr):
    s = pl.program_id(1)
    tm = x_ref.shape[0]
    pad = SUBLANES

    @pl.when(s == 0)
    def _():
        a_scr[0:pad, :] = jnp.zeros((pad, D_FF), F32)

    x = x_ref[...]
    h = _rms(x, gpre_ref[...]).astype(BF16)
    cw = cw_ref[...]
    cb = cb_ref[...]

    def up(lo, hi):
        return _dot(h, win_ref[:, lo:hi]), _dot(h, win_ref[:, D_FF + lo:D_FF + hi])

    bounds = [(lo, min(lo + FFN_SUB, D_FF)) for lo in range(0, D_FF, FFN_SUB)]
    out = None
    nxt = up(*bounds[0])
    for i, (lo, hi) in enumerate(bounds):
        a, gate = nxt
        if i + 1 < len(bounds):
            nxt = up(*bounds[i + 1])
        a_scr[pad:pad + tm, lo:hi] = a
        ac = cb[:, lo:hi] + a * cw[FFN_CONV_W - 1:FFN_CONV_W, lo:hi]
        for k in range(FFN_CONV_W - 1):
            off = pad - (FFN_CONV_W - 1) + k
            ac = ac + a_scr[off:off + tm, lo:hi] * cw[k:k + 1, lo:hi]
        d = _dot((_silu(ac) * gate).astype(BF16), wo_ref[lo:hi, :])
        out = d if out is None else out + d
    tail = a_scr[tm:tm + pad, :]
    a_scr[0:pad, :] = tail
    bufo_ref[0] = tail
    xo_ref[...] = x + _rms(out, gpost_ref[...])


def _ffn_prompt(x, p, nb, seq, tm):
    ns = seq // tm
    const = lambda b, s: (0, 0)
    return pl.pallas_call(
        _ffn_prompt_kernel,
        grid=(nb, ns),
        in_specs=[pl.BlockSpec((tm, D_MODEL), lambda b, s: (b * ns + s, 0)),
                  pl.BlockSpec((1, D_MODEL), const),
                  _wspec(p["layer"], (D_MODEL, 2 * D_FF), const, resident=True),
                  pl.BlockSpec((SUBLANES, D_FF), const),
                  pl.BlockSpec((1, D_FF), const),
                  _wspec(p["layer"], (D_FF, D_MODEL), const, resident=True),
                  pl.BlockSpec((1, D_MODEL), const)],
        out_specs=[pl.BlockSpec((tm, D_MODEL), lambda b, s: (b * ns + s, 0)),
                   pl.BlockSpec((1, SUBLANES, D_FF), lambda b, s: (b, 0, 0))],
        out_shape=[jax.ShapeDtypeStruct((nb * seq, D_MODEL), F32),
                   jax.ShapeDtypeStruct((nb, SUBLANES, D_FF), F32)],
        scratch_shapes=[pltpu.VMEM((tm + SUBLANES, D_FF), F32)],
        compiler_params=_cparams(2, VMEM_LIMIT_BIG),
        name="ffn_prompt",
    )(x, p["norm_ffn_pre"], p["w_ffn_in"], p["ffn_conv_w"], p["ffn_conv_b"],
      p["w_ffn_out"], p["norm_ffn_post"])


def _ffn_sample_kernel(x_ref, gpre_ref, wa_ref, wg_ref, cw_ref, cb_ref, b0_ref, b1_ref, wo_ref,
                       gpost_ref, xo_ref, ao_ref, h_scr, acc_scr):
    kf = pl.program_id(2)

    @pl.when(kf == 0)
    def _():
        h_scr[...] = _rms(x_ref[...], gpre_ref[...]).astype(BF16)

    h = h_scr[...]
    a = _dot(h, wa_ref[...])
    gate = _dot(h, wg_ref[...])
    ao_ref[...] = a
    cw = cw_ref[...]
    ac = cb_ref[...] + b0_ref[...] * cw[0:1, :] + b1_ref[...] * cw[1:2, :] + a * cw[2:3, :]
    part = _dot((_silu(ac) * gate).astype(BF16), wo_ref[...])
    _ffn_finish(kf, part, x_ref, gpost_ref, xo_ref, acc_scr)


def _ffn_sample(x, b0, b1, p):
    m = x.shape[0]
    nk = D_FF // FF_CHUNK
    const = lambda b, s, k: (0, 0)
    chunk = lambda b, s, k: (0, k)
    return pl.pallas_call(
        _ffn_sample_kernel,
        grid=(1, 1, nk),
        in_specs=[pl.BlockSpec((m, D_MODEL), const),
                  pl.BlockSpec((1, D_MODEL), const),
                  _wspec(p["layer"], (D_MODEL, FF_CHUNK), chunk),
                  _wspec(p["layer"], (D_MODEL, FF_CHUNK), lambda b, s, k: (0, nk + k)),
                  pl.BlockSpec((SUBLANES, FF_CHUNK), chunk),
                  pl.BlockSpec((1, FF_CHUNK), chunk),
                  pl.BlockSpec((m, FF_CHUNK), chunk),
                  pl.BlockSpec((m, FF_CHUNK), chunk),
                  _wspec(p["layer"], (FF_CHUNK, D_MODEL), lambda b, s, k: (k, 0)),
                  pl.BlockSpec((1, D_MODEL), const)],
        out_specs=[pl.BlockSpec((m, D_MODEL), const),
                   pl.BlockSpec((m, FF_CHUNK), chunk)],
        out_shape=[jax.ShapeDtypeStruct((m, D_MODEL), F32),
                   jax.ShapeDtypeStruct((m, D_FF), F32)],
        scratch_shapes=[pltpu.VMEM((m, D_MODEL), BF16),
                        pltpu.VMEM((m, D_MODEL), F32)],
        compiler_params=_cparams(3),
        name="ffn_sample",
    )(x, p["norm_ffn_pre"], p["w_ffn_in"], p["w_ffn_in"], p["ffn_conv_w"], p["ffn_conv_b"], b0, b1,
      p["w_ffn_out"], p["norm_ffn_post"])


def _dec_prep_kernel(xbc_ref, dtr_ref, b0_ref, b1_ref, b2_ref, cw_ref, cb_ref, dtb_ref, alog_ref,
                     dx_ref, e_ref, xdt_ref, dec_ref, bs_ref, cs_ref, xsd_ref):
    cw = cw_ref[...]
    xc = (cb_ref[...] + b0_ref[...] * cw[0:1, :] + b1_ref[...] * cw[1:2, :] + b2_ref[...] * cw[2:3, :]
          + xbc_ref[...].astype(F32) * cw[3:4, :])
    xc = _silu(xc)
    xs = xc[:, :D_SSM]
    bs_ref[...] = xc[:, D_SSM:D_SSM + G_SSM * N_SSM]
    cs_ref[...] = xc[:, D_SSM + G_SSM * N_SSM:]
    dt = _softplus(dtr_ref[...] + dtb_ref[...])
    dec_ref[...] = jnp.exp(dt * (-jnp.exp(alog_ref[...])))
    xdt_ref[...] = xs * _dot_exact_lhs(dt, e_ref[...])
    xsd_ref[...] = xs * dx_ref[...]


def _dec_prep(z, dtr, b0, b1, b2, p):
    m = dtr.shape[0]
    const = lambda i: (0, 0)
    full = lambda a: pl.BlockSpec(a.shape, const)
    return pl.pallas_call(
        _dec_prep_kernel,
        grid=(1,),
        in_specs=[pl.BlockSpec((m, SSM_CONV_DIM), lambda i: (0, OFF_XBC // SSM_CONV_DIM)),
                  full(dtr), full(b0), full(b1), full(b2), full(p["ssm_conv_w"]), full(p["ssm_conv_b"]),
                  full(p["dt_bias"]), full(p["a_log"]), full(p["d_x"]), full(p["expand"])],
        out_specs=[pl.BlockSpec((m, D_SSM), const), pl.BlockSpec((m, LANES), const),
                   pl.BlockSpec((m, G_SSM * N_SSM), const), pl.BlockSpec((m, G_SSM * N_SSM), const),
                   pl.BlockSpec((m, D_SSM), const)],
        out_shape=[jax.ShapeDtypeStruct((m, D_SSM), F32), jax.ShapeDtypeStruct((m, LANES), F32),
                   jax.ShapeDtypeStruct((m, G_SSM * N_SSM), F32),
                   jax.ShapeDtypeStruct((m, G_SSM * N_SSM), F32),
                   jax.ShapeDtypeStruct((m, D_SSM), F32)],
        compiler_params=_cparams(1),
        name="dec_prep",
    )(z, dtr, b0, b1, b2, p["ssm_conv_w"], p["ssm_conv_b"], p["dt_bias"], p["a_log"], p["d_x"],
      p["expand"])


def _dec_state_kernel(dec_ref, st_ref, xdt_ref, bs_ref, cs_ref, *rest):
    so_ref, y_ref = rest[-2:]
    bt = st_ref.shape[0]
    rows_per_group = D_SSM // G_SSM
    grp = lax.broadcasted_iota(jnp.int32, (G_SSM, D_SSM), 0)
    own = (lax.broadcasted_iota(jnp.int32, (G_SSM, D_SSM), 1) // rows_per_group) == grp
    for i in range(bt):
        b = pl.program_id(0) * bt + i
        x = xdt_ref[pl.ds(b, 1), :]
        x8 = jnp.where(own, jnp.broadcast_to(x, own.shape), 0.0).astype(BF16)
        outer = lax.dot_general(x8, bs_ref[b].astype(BF16), (((0,), (0,)), ((), ())),
                                preferred_element_type=F32)
        for h in range(H_SSM):
            lo, hi = h * SSM_HEADDIM, (h + 1) * SSM_HEADDIM
            so_ref[i, lo:hi, :] = st_ref[i, lo:hi, :] * dec_ref[b, h] + outer[lo:hi, :]
        y8 = _dot_nt(cs_ref[b].astype(BF16), so_ref[i].astype(BF16))
        y_ref[pl.ds(b, 1), :] = jnp.sum(jnp.where(own, y8, 0.0), axis=0, keepdims=True)


def _dec_state(state_all, layer, dec, xdt, bs3, cs3, carry):
    nb = xdt.shape[0]
    bt = _pick_tile(nb, 4)
    const2 = lambda b: (0, 0)
    const3 = lambda b: (0, 0, 0)
    slab = lambda b: (layer * (nb // bt) + b, 0, 0)
    extra = [] if carry is None else [carry]
    return pl.pallas_call(
        _dec_state_kernel,
        grid=(nb // bt,),
        in_specs=[pl.BlockSpec(memory_space=pltpu.SMEM),
                  pl.BlockSpec((bt, D_SSM, N_SSM), slab),
                  pl.BlockSpec(xdt.shape, const2),
                  pl.BlockSpec(bs3.shape, const3), pl.BlockSpec(cs3.shape, const3)]
                 + [pl.BlockSpec(memory_space=pl.ANY)] * len(extra),
        out_specs=[pl.BlockSpec((bt, D_SSM, N_SSM), slab),
                   pl.BlockSpec((nb, D_SSM), const2)],
        out_shape=[jax.ShapeDtypeStruct(state_all.shape, F32),
                   jax.ShapeDtypeStruct((nb, D_SSM), F32)],
        input_output_aliases={5: 0} if extra else {},
        compiler_params=_cparams(1),
        name="dec_state",
    )(dec, state_all, xdt, bs3, cs3, *extra)


def _dec_post_kernel(y_ref, xsd_ref, zs_ref, nrm_ref, o_ref):
    y = y_ref[...] + xsd_ref[...]
    y = y * _silu(zs_ref[...].astype(F32))
    o_ref[...] = _rms(y, nrm_ref[...]).astype(o_ref.dtype)


def _dec_post(yt, xsd, z, p):
    m = xsd.shape[0]
    const = lambda i: (0, 0)
    return pl.pallas_call(
        _dec_post_kernel,
        grid=(1,),
        in_specs=[pl.BlockSpec(yt.shape, const), pl.BlockSpec(xsd.shape, const),
                  pl.BlockSpec((m, D_SSM), lambda i: (0, OFF_ZS // D_SSM)),
                  pl.BlockSpec((1, D_SSM), const)],
        out_specs=pl.BlockSpec((m, D_SSM), const),
        out_shape=jax.ShapeDtypeStruct((m, D_SSM), BF16),
        compiler_params=_cparams(1),
        name="dec_post",
    )(yt, xsd, z, p["ssm_norm"])


def _dec_swa_kernel(qe_ref, kn_ref, vn_ref, ck_ref, cv_ref, c_ref, s1_ref, s2_ref, sink_ref, *rest):
    nk_ref, nv_ref, oe_ref = rest[-3:]
    bt = ck_ref.shape[0]
    w = ck_ref.shape[1]
    c, s1, s2 = c_ref[0:1, :], s1_ref[0:1, :], s2_ref[0:1, :]
    kn = _rope(kn_ref[...].astype(F32), c, s1, s2)
    vn = vn_ref[...].astype(F32)
    last = lax.broadcasted_iota(jnp.int32, (w, KV_ATT * HD_ATT), 0) == w - 1
    sink = sink_ref[:, 0:1]
    for i in range(bt):
        nk = jnp.where(last, kn[i:i + 1, :], pltpu.roll(ck_ref[i], w - 1, 0))
        nv = jnp.where(last, vn[i:i + 1, :], pltpu.roll(cv_ref[i], w - 1, 0))
        nk_ref[i] = nk
        nv_ref[i] = nv
        qe = _rope(qe_ref[i].astype(F32), c, s1, s2) * (HD_ATT ** -0.5)
        s = _dot_nt(qe.astype(BF16), nk.astype(BF16))
        m = jnp.maximum(jnp.max(s, axis=-1, keepdims=True), sink)
        e = jnp.exp(s - m)
        den = jnp.sum(e, axis=-1, keepdims=True) + jnp.exp(sink - m)
        pb = (e * (1.0 / den)).astype(BF16)
        oe_ref[i] = _dot(pb, nv.astype(BF16)).astype(oe_ref.dtype)


def _dec_swa(qe, z, ck_all, cv_all, layer, p, bt, carry):
    nb = qe.shape[0]
    _, w, kvw = ck_all.shape
    const = lambda i: (0, 0)
    slab = lambda i: (layer * (nb // bt) + i, 0, 0)
    tab = pl.BlockSpec((SUBLANES, LANES), const)
    extra = [] if carry is None else list(carry)
    return pl.pallas_call(
        _dec_swa_kernel,
        grid=(nb // bt,),
        in_specs=[pl.BlockSpec((bt, H_ATT, kvw), lambda i: (i, 0, 0)),
                  pl.BlockSpec((bt, kvw), lambda i: (i, OFF_K // kvw)),
                  pl.BlockSpec((bt, kvw), lambda i: (i, OFF_V // kvw)),
                  pl.BlockSpec((bt, w, kvw), slab),
                  pl.BlockSpec((bt, w, kvw), slab),
                  tab, tab, tab,
                  pl.BlockSpec((H_ATT, LANES), const)]
                 + [pl.BlockSpec(memory_space=pl.ANY)] * len(extra),
        out_specs=[pl.BlockSpec((bt, w, kvw), slab),
                   pl.BlockSpec((bt, w, kvw), slab),
                   pl.BlockSpec((bt, H_ATT, kvw), lambda i: (i, 0, 0))],
        out_shape=[jax.ShapeDtypeStruct(ck_all.shape, F32), jax.ShapeDtypeStruct(cv_all.shape, F32),
                   jax.ShapeDtypeStruct((nb, H_ATT, kvw), BF16)],
        input_output_aliases={9: 0, 10: 1} if extra else {},
        compiler_params=_cparams(1),
        name="dec_swa",
    )(qe, z, z, ck_all, cv_all, p["rope_c_s"], p["rope_s1_s"], p["rope_s2_s"], p["sinks_x"], *extra)


def _dec_xattn_kernel(q_ref, k_ref, v_ref, o_ref):
    bt = k_ref.shape[1]
    for i in range(bt):
        qh = q_ref[i] * (X_HD ** -0.5)
        s = jnp.sum(k_ref[0, i] * qh[None], axis=-1, keepdims=True)
        m = jnp.max(s, axis=0, keepdims=True)
        e = jnp.exp(s - m)
        pr = e * (1.0 / jnp.sum(e, axis=0, keepdims=True))
        o_ref[i] = jnp.sum(pr * v_ref[0, i], axis=0)


def _dec_xattn(qv, mem_k, mem_v, layer, bt):
    nb = qv.shape[0]
    _, _, mem_len, nh, hd = mem_k.shape
    slab = lambda i: (layer, i, 0, 0, 0)
    return pl.pallas_call(
        _dec_xattn_kernel,
        grid=(nb // bt,),
        in_specs=[pl.BlockSpec((bt, nh, hd), lambda i: (i, 0, 0)),
                  pl.BlockSpec((1, bt, mem_len, nh, hd), slab),
                  pl.BlockSpec((1, bt, mem_len, nh, hd), slab)],
        out_specs=pl.BlockSpec((bt, nh, hd), lambda i: (i, 0, 0)),
        out_shape=jax.ShapeDtypeStruct((nb, nh, hd), F32),
        compiler_params=_cparams(1),
        name="dec_xattn",
    )(qv, mem_k, mem_v)


def _pad_rows(w, rows=SUBLANES):
    return jnp.pad(w, ((0, rows - w.shape[0]), (0, 0)))


def _pad_lanes(v, lanes=LANES):
    return jnp.pad(v, (0, lanes - v.shape[0])).reshape(1, lanes)


def _rope_tables(pos):
    half = ROT_DIM // 2
    inv = ROPE_THETA ** (-jnp.arange(half, dtype=F32) / half)
    ang = pos.astype(F32)[:, None] * inv[None, :]
    cos, sin = jnp.cos(ang), jnp.sin(ang)
    n = pos.shape[0]
    ones = jnp.ones((n, HD_ATT - ROT_DIM), F32)
    zeros = jnp.zeros((n, HD_ATT - ROT_DIM), F32)
    zh = jnp.zeros((n, half), F32)
    c = jnp.concatenate([cos, cos, ones], axis=1)
    s1 = jnp.concatenate([-sin, zh, zeros], axis=1)
    s2 = jnp.concatenate([zh, sin, zeros], axis=1)
    rep = LANES // HD_ATT
    return tuple(jnp.tile(t, (1, rep)) for t in (c, s1, s2))


_BF16_WEIGHTS = ("w_a_out", "w_ssm_out", "w_attn_out", "w_out", "w_xq", "w_xk", "w_xv", "w_xo",
                 "w_ffn_in", "w_ffn_out")


def _prep_weights(prm):
    w_in = prm["w_in"]
    depth = w_in.shape[0]
    o = np.cumsum([0, D_A, D_A, D_A, D_SSM, SSM_CONV_DIM, H_SSM, H_ATT * HD_ATT, KV_ATT * HD_ATT,
                   KV_ATT * HD_ATT, 3 * D_MODEL])
    col = lambda i: w_in[:, :, o[i]:o[i + 1]]
    w = {name: prm[name].astype(BF16) for name in _BF16_WEIGHTS}
    w["w_main"] = jnp.concatenate(
        [col(4), col(3), col(0), col(1), col(2), col(9), col(6), col(7), col(8)], axis=-1).astype(BF16)
    w["w_dt"] = jnp.pad(col(5), ((0, 0), (0, 0), (0, LANES - H_SSM))).astype(BF16)
    rep = H_ATT // KV_ATT
    wq, wc = col(6).astype(BF16), w["w_attn_out"]
    q_blocks, c_blocks = [], []
    for h in range(H_ATT):
        g = h // rep
        q_blocks += [jnp.zeros((depth, D_MODEL, g * HD_ATT), BF16),
                     wq[:, :, h * HD_ATT:(h + 1) * HD_ATT],
                     jnp.zeros((depth, D_MODEL, (KV_ATT - 1 - g) * HD_ATT), BF16)]
        c_blocks += [jnp.zeros((depth, g * HD_ATT, D_MODEL), BF16),
                     wc[:, h * HD_ATT:(h + 1) * HD_ATT, :],
                     jnp.zeros((depth, (KV_ATT - 1 - g) * HD_ATT, D_MODEL), BF16)]
    w["wq_exp"] = jnp.concatenate(q_blocks, axis=2)
    w["w_attn_out_exp"] = jnp.concatenate(c_blocks, axis=1)
    return w


def _prep_layer(l, prm, w, seq):
    expand = jnp.asarray(np.kron(np.eye(LANES, H_SSM, dtype=np.float32),
                                 np.ones((1, SSM_HEADDIM), np.float32))).astype(BF16)
    rope_p = _rope_tables(jnp.arange(seq, dtype=jnp.int32))
    rope_s = _rope_tables(jnp.full((SUBLANES,), PAST_LEN, jnp.int32))
    return {
        **w,
        "layer": l,
        "norm_mix_pre": prm["norm_mix_pre"][l].reshape(1, -1),
        "norm_mix_post": prm["norm_mix_post"][l].reshape(1, -1),
        "conv_a_w": _pad_rows(prm["conv_a_w"][l]),
        "ssm_conv_w": _pad_rows(prm["ssm_conv_w"][l]),
        "ssm_conv_b": prm["ssm_conv_b"][l].reshape(1, -1),
        "dt_bias": _pad_lanes(prm["ssm_dt_bias"][l]),
        "a_log": _pad_lanes(prm["ssm_a_log"][l]),
        "d_x": jnp.repeat(prm["ssm_d"][l], SSM_HEADDIM).reshape(1, -1),
        "ssm_norm": prm["ssm_norm"][l].reshape(1, -1),
        "expand": expand,
        "sinks": prm["attn_sinks"][l],
        "sinks_x": jnp.broadcast_to(prm["attn_sinks"][l][:, None], (H_ATT, LANES)),
        "rope_c": rope_p[0], "rope_s1": rope_p[1], "rope_s2": rope_p[2],
        "rope_c_s": rope_s[0], "rope_s1_s": rope_s[1], "rope_s2_s": rope_s[2],
        "norm_x_pre": prm["norm_x_pre"][l].reshape(1, -1),
        "norm_x_post": prm["norm_x_post"][l].reshape(1, -1),
        "norm_mem": prm["norm_mem"][l].reshape(1, -1),
        "norm_ffn_pre": prm["norm_ffn_pre"][l].reshape(1, -1),
        "norm_ffn_post": prm["norm_ffn_post"][l].reshape(1, -1),
        "ffn_conv_w": _pad_rows(prm["ffn_conv_w"][l]),
        "ffn_conv_b": prm["ffn_conv_b"][l].reshape(1, -1),
    }


def _pick_tile(n, pref):
    t = min(n, pref)
    while n % t:
        t //= 2
    return t


def _prompt_layer(x, mem, p, nb, seq, mem_len):
    rows = nb * seq
    z, dtr = _in_proj(x, p["norm_mix_pre"], p["w_main"], p["w_dt"], p["layer"], _pick_tile(rows, 1024),
                      2304)
    ys, h_last, cbuf = _ssd_prompt(z, dtr, p, nb, seq)
    oc, k_new, v_new = _swa_prompt(z, p, nb, seq)
    x, bufa = _tail_prompt(z, ys, oc, x, p, nb, seq, _pick_tile(seq, 512))
    tmem = _pick_tile(mem.shape[0], 512)
    mk = _norm_matmul(mem, p["norm_mem"], p["w_xk"], p["layer"], F32, tmem, X_H * X_HD)
    mv = _norm_matmul(mem, p["norm_mem"], p["w_xv"], p["layer"], F32, tmem, X_H * X_HD)
    x = _xattn_prompt(x, mk, mv, p, nb, seq, mem_len, _pick_tile(seq, 512))
    x, buff = _ffn_prompt(x, p, nb, seq, _pick_tile(seq, 512))
    state = (bufa[:, SUBLANES - (CONV_A_W - 1):],
             cbuf[:, SUBLANES - (SSM_CONV_W - 1):],
             h_last.reshape(nb, H_SSM, SSM_HEADDIM, N_SSM),
             k_new.reshape(nb, WINDOW, KV_ATT, HD_ATT),
             v_new.reshape(nb, WINDOW, KV_ATT, HD_ATT),
             buff[:, SUBLANES - (FFN_CONV_W - 1):],
             mk.reshape(nb, mem_len, X_H, X_HD),
             mv.reshape(nb, mem_len, X_H, X_HD))
    return x, state


def _sample_layer(x, layer, st, big, carry, p):
    buf_a, buf_ssm, buf_ffn = st
    ssm_all, swa_k_all, swa_v_all, mem_k_all, mem_v_all = big
    nb = x.shape[0]
    kvw = KV_ATT * HD_ATT
    z, dtr = _in_proj(x, p["norm_mix_pre"], p["w_main"], p["w_dt"], layer, nb, 1536)
    qe = _norm_matmul(x, p["norm_mix_pre"], p["wq_exp"], layer, BF16, nb, 1024).reshape(nb, H_ATT, kvw)
    xdt, dec, bs, cs, xsd = _dec_prep(z, dtr, buf_ssm[:, 0], buf_ssm[:, 1], buf_ssm[:, 2], p)
    new_ssm, y_ssd = _dec_state(ssm_all, layer, dec[:, :H_SSM], xdt, bs.reshape(nb, G_SSM, N_SSM),
                                cs.reshape(nb, G_SSM, N_SSM), None if carry is None else carry[0])
    ys = _dec_post(y_ssd, xsd, z, p)
    new_k, new_v, oe = _dec_swa(qe, z, swa_k_all, swa_v_all, layer, p, _pick_tile(nb, 16),
                                None if carry is None else carry[1:3])
    x, cv = _tail_sample(z, ys, oe.reshape(nb, H_ATT * kvw), x, buf_a[:, 0], buf_a[:, 1], p)
    qx = _norm_matmul(x, p["norm_x_pre"], p["w_xq"], layer, F32, nb, 1024)
    ox = _dec_xattn(qx.reshape(nb, X_H, X_HD), mem_k_all, mem_v_all, layer, _pick_tile(nb, 4))
    x = _mm_norm_res(ox.reshape(nb, X_H * X_HD), p["w_xo"], layer, p["norm_x_post"], x)
    x, a_up = _ffn_sample(x, buf_ffn[:, 0], buf_ffn[:, 1], p)
    x_raw = z[:, OFF_XBC:OFF_XBC + SSM_CONV_DIM].astype(F32)
    small = (jnp.stack([buf_a[:, 1], cv], axis=1),
             jnp.concatenate([buf_ssm[:, 1:], x_raw[:, None, :]], axis=1),
             jnp.stack([buf_ffn[:, 1], a_up], axis=1))
    return x, small, (new_ssm, new_k, new_v)


def kernel(x_prompt, x_sample, mem_prompt, state_conv_a, state_ssm_conv, state_ssm, cache_swa_k, cache_swa_v, cache_mem_k, cache_mem_v, state_ffn_conv, norm_mix_pre, norm_mix_post, w_in, conv_a_w, w_a_out, ssm_conv_w, ssm_conv_b, ssm_dt_bias, ssm_a_log, ssm_d, ssm_norm, w_ssm_out, attn_sinks, w_attn_out, w_out, norm_x_pre, norm_x_post, norm_mem, w_xq, w_xk, w_xv, w_xo, norm_ffn_pre, norm_ffn_post, w_ffn_in, ffn_conv_w, ffn_conv_b, w_ffn_out):
    prm = dict(norm_mix_pre=norm_mix_pre, norm_mix_post=norm_mix_post, w_in=w_in, conv_a_w=conv_a_w,
               w_a_out=w_a_out, ssm_conv_w=ssm_conv_w, ssm_conv_b=ssm_conv_b, ssm_dt_bias=ssm_dt_bias,
               ssm_a_log=ssm_a_log, ssm_d=ssm_d, ssm_norm=ssm_norm, w_ssm_out=w_ssm_out,
               attn_sinks=attn_sinks, w_attn_out=w_attn_out, w_out=w_out, norm_x_pre=norm_x_pre,
               norm_x_post=norm_x_post, norm_mem=norm_mem, w_xq=w_xq, w_xk=w_xk, w_xv=w_xv, w_xo=w_xo,
               norm_ffn_pre=norm_ffn_pre, norm_ffn_post=norm_ffn_post, w_ffn_in=w_ffn_in,
               ffn_conv_w=ffn_conv_w, ffn_conv_b=ffn_conv_b, w_ffn_out=w_ffn_out)
    nb, seq, d = x_prompt.shape
    ns = x_sample.shape[0]
    mem_len = mem_prompt.shape[1]
    depth = w_in.shape[0]
    assert x_sample.shape[1] == 1 and seq % WINDOW == 0 and seq % SSD_CHUNK == 0
    assert cache_swa_k.shape[2] == WINDOW and PAST_LEN >= WINDOW

    yp = x_prompt.reshape(nb * seq, d)
    ys = x_sample.reshape(ns, d)
    mem = mem_prompt.reshape(nb * mem_len, d)
    flat = lambda a, *tail: a.reshape((depth * ns,) + tail)
    kvw = KV_ATT * HD_ATT
    big = (flat(state_ssm, D_SSM, N_SSM),
           flat(cache_swa_k, cache_swa_k.shape[2], kvw), flat(cache_swa_v, cache_swa_v.shape[2], kvw),
           cache_mem_k, cache_mem_v)
    w = _prep_weights(prm)
    new_p, new_s, carry = [], [], None
    for l in range(depth):
        p = _prep_layer(l, prm, w, seq)
        yp, st_p = _prompt_layer(yp, mem, p, nb, seq, mem_len)
        new_p.append(st_p)
        ys, st_s, carry = _sample_layer(ys, l, (state_conv_a[l], state_ssm_conv[l], state_ffn_conv[l]),
                                        big, carry, p)
        new_s.append(st_s)
    stack = lambda lst, i: jnp.stack([s[i] for s in lst])
    s_ssm, s_swa_k, s_swa_v = (carry[0].reshape(state_ssm.shape), carry[1].reshape(cache_swa_k.shape),
                               carry[2].reshape(cache_swa_v.shape))
    return ((yp.reshape(nb, seq, d), ys.reshape(ns, 1, d))
            + tuple(stack(new_p, i) for i in range(8))
            + (stack(new_s, 0), stack(new_s, 1), s_ssm, s_swa_k, s_swa_v, stack(new_s, 2)))
```

```python
import functools

import numpy as np
import jax
import jax.numpy as jnp
from jax import lax
from jax.experimental import pallas as pl
from jax.experimental.pallas import tpu as pltpu

F32 = jnp.float32
BF16 = jnp.bfloat16

EPS = 1e-6
D_MODEL = 1024
D_A = D_MODEL
CONV_A_W = 3
D_SSM = 2 * D_MODEL
SSM_HEADDIM = 64
H_SSM = D_SSM // SSM_HEADDIM
G_SSM = 8
N_SSM = 128
SSM_CONV_W = 4
SSM_CONV_DIM = D_SSM + 2 * G_SSM * N_SSM
SSD_CHUNK = 128
H_ATT = 16
KV_ATT = 4
HD_ATT = 64
ROT_DIM = HD_ATT // 4
ROPE_THETA = 500000.0
WINDOW = 128
X_H = 4
X_HD = D_MODEL // X_H
D_FF = 2816
FFN_CONV_W = 3
PAST_LEN = 8192
LOG2E = 1.4426950408889634

LANES = 128
SUBLANES = 8
VMEM_LIMIT = 48 * 1024 * 1024
VMEM_LIMIT_BIG = 56 * 1024 * 1024

OFF_XBC = 0
OFF_ZS = OFF_XBC + SSM_CONV_DIM
OFF_ZA = OFF_ZS + D_SSM
OFF_GATES = OFF_ZA + 3 * D_A
OFF_Q = OFF_GATES + 3 * D_MODEL
OFF_K = OFF_Q + H_ATT * HD_ATT
OFF_V = OFF_K + KV_ATT * HD_ATT
N_MAIN = OFF_V + KV_ATT * HD_ATT
FF_CHUNK = D_FF // 2
FFN_SUB = 4 * LANES


def _cparams(n_axes, vmem=VMEM_LIMIT):
    return pltpu.CompilerParams(dimension_semantics=("arbitrary",) * n_axes,
                                vmem_limit_bytes=vmem)


def _resident(shape, index_map):
    return pl.BlockSpec(shape, index_map, pipeline_mode=pl.Buffered(1))


def _wspec(layer, shape, index_map, resident=False):
    imap = lambda *g: (layer,) + tuple(index_map(*g))
    if resident:
        return pl.BlockSpec((None,) + tuple(shape), imap, pipeline_mode=pl.Buffered(1))
    return pl.BlockSpec((None,) + tuple(shape), imap)


def _rms(x, g):
    return x * lax.rsqrt(jnp.mean(x * x, axis=-1, keepdims=True) + EPS) * g


def _sigmoid(x):
    return 0.5 + 0.5 * jnp.tanh(0.5 * x)


def _silu(x):
    hx = 0.5 * x
    return hx + hx * jnp.tanh(hx)


def _softplus(x):
    return jnp.maximum(x, 0.0) + jnp.log(1.0 + jnp.exp(-jnp.abs(x)))


def _dot(a, b):
    return jnp.dot(a, b, preferred_element_type=F32)


def _dot_nt(a, b):
    return lax.dot_general(a, b, (((1,), (1,)), ((), ())), preferred_element_type=F32)


def _dot_exact_rhs(a01, v):
    hi = v.astype(BF16)
    r1 = v - hi.astype(F32)
    mid = r1.astype(BF16)
    lo = (r1 - mid.astype(F32)).astype(BF16)
    return _dot(a01, hi) + _dot(a01, mid) + _dot(a01, lo)


def _dot_exact_lhs(v, b01):
    hi = v.astype(BF16)
    r1 = v - hi.astype(F32)
    mid = r1.astype(BF16)
    lo = (r1 - mid.astype(F32)).astype(BF16)
    return _dot(hi, b01) + _dot(mid, b01) + _dot(lo, b01)


def _rope(x, c, s1, s2):
    outs = []
    for i in range(x.shape[1] // LANES):
        xi = x[:, i * LANES:(i + 1) * LANES]
        outs.append(xi * c + pltpu.roll(xi, LANES - ROT_DIM // 2, 1) * s1
                    + pltpu.roll(xi, ROT_DIM // 2, 1) * s2)
    return outs[0] if len(outs) == 1 else jnp.concatenate(outs, axis=1)


def _norm_mm_kernel(x_ref, g_ref, w_ref, o_ref, h_scr):
    @pl.when(pl.program_id(1) == 0)
    def _():
        h_scr[...] = _rms(x_ref[...], g_ref[...]).astype(BF16)

    o_ref[...] = _dot(h_scr[...], w_ref[...]).astype(o_ref.dtype)


def _norm_matmul(x, g, w, layer, out_dtype, tm, tn):
    m, k = x.shape
    n = w.shape[2]
    return pl.pallas_call(
        _norm_mm_kernel,
        grid=(m // tm, n // tn),
        in_specs=[pl.BlockSpec((tm, k), lambda i, j: (i, 0)),
                  pl.BlockSpec((1, k), lambda i, j: (0, 0)),
                  _wspec(layer, (k, tn), lambda i, j: (0, j))],
        out_specs=pl.BlockSpec((tm, tn), lambda i, j: (i, j)),
        out_shape=jax.ShapeDtypeStruct((m, n), out_dtype),
        scratch_shapes=[pltpu.VMEM((tm, k), BF16)],
        compiler_params=_cparams(2),
        name="norm_matmul",
    )(x, g, w)


def _inproj_kernel(x_ref, g_ref, w_ref, wdt_ref, z_ref, dt_ref, h_scr):
    @pl.when(pl.program_id(1) == 0)
    def _():
        h = _rms(x_ref[...], g_ref[...]).astype(BF16)
        h_scr[...] = h
        dt_ref[...] = _dot(h, wdt_ref[...])

    z_ref[...] = _dot(h_scr[...], w_ref[...]).astype(z_ref.dtype)


def _in_proj(x, g, w, wdt, layer, tm, tn):
    m, k = x.shape
    n = w.shape[2]
    return pl.pallas_call(
        _inproj_kernel,
        grid=(m // tm, n // tn),
        in_specs=[pl.BlockSpec((tm, k), lambda i, j: (i, 0)),
                  pl.BlockSpec((1, k), lambda i, j: (0, 0)),
                  _wspec(layer, (k, tn), lambda i, j: (0, j)),
                  _wspec(layer, (k, LANES), lambda i, j: (0, 0))],
        out_specs=[pl.BlockSpec((tm, tn), lambda i, j: (i, j)),
                   pl.BlockSpec((tm, LANES), lambda i, j: (i, 0))],
        out_shape=[jax.ShapeDtypeStruct((m, n), BF16),
                   jax.ShapeDtypeStruct((m, LANES), F32)],
        scratch_shapes=[pltpu.VMEM((tm, k), BF16)],
        compiler_params=_cparams(2),
        name="in_proj",
    )(x, g, w, wdt)


def _mm_norm_res_kernel(a_ref, w_ref, g_ref, x_ref, o_ref):
    y = _dot(a_ref[...].astype(BF16), w_ref[...])
    o_ref[...] = x_ref[...] + _rms(y, g_ref[...])


def _mm_norm_res(a, w, layer, g, x):
    m, k = a.shape
    n = w.shape[2]
    return pl.pallas_call(
        _mm_norm_res_kernel,
        grid=(1,),
        in_specs=[pl.BlockSpec((m, k), lambda i: (0, 0)),
                  _wspec(layer, (k, n), lambda i: (0, 0)),
                  pl.BlockSpec((1, n), lambda i: (0, 0)),
                  pl.BlockSpec((m, n), lambda i: (0, 0))],
        out_specs=pl.BlockSpec((m, n), lambda i: (0, 0)),
        out_shape=jax.ShapeDtypeStruct((m, n), F32),
        compiler_params=_cparams(1),
        name="mm_norm_res",
    )(a, w, g, x)


def _ssd_kernel(xbc_ref, zs_ref, dtr_ref, cw_ref, cb_ref, dtb_ref, alog_ref, dx_ref, nrm_ref, e_ref,
                y_ref, hout_ref, cbuf_ref, x_scr, ht_scr):
    c = pl.program_id(1)
    q = xbc_ref.shape[0]
    pad = SUBLANES

    halo = x_scr.shape[0]

    @pl.when(c == 0)
    def _():
        x_scr[...] = jnp.zeros(x_scr.shape, BF16)
        ht_scr[...] = jnp.zeros(ht_scr.shape, F32)

    xb = xbc_ref[...]
    xin = xb.astype(F32)
    ext = jnp.concatenate([x_scr[...], xb], axis=0)
    n_sh = SSM_CONV_W - 1
    ri = lax.broadcasted_iota(jnp.int32, (n_sh * q, halo + q), 0)
    ci = lax.broadcasted_iota(jnp.int32, (n_sh * q, halo + q), 1)
    shift_mat = jnp.where(ci == (ri % q) + halo - n_sh + ri // q, 1.0, 0.0).astype(BF16)
    shifted = _dot(shift_mat, ext)
    cw = cw_ref[...]
    xc = cb_ref[...] + xin * cw[n_sh:n_sh + 1, :]
    for k in range(n_sh):
        xc = xc + shifted[k * q:(k + 1) * q, :] * cw[k:k + 1, :]
    xc = _silu(xc)
    x_scr[...] = xb[q - halo:q, :]
    cbuf_ref[0] = xin[q - pad:q, :]

    xs = xc[:, :D_SSM]
    bs = xc[:, D_SSM:D_SSM + G_SSM * N_SSM]
    cs_in = xc[:, D_SSM + G_SSM * N_SSM:]

    dt = _softplus(dtr_ref[...] + dtb_ref[...])
    a = dt * (-jnp.exp(alog_ref[...]))
    row = lax.broadcasted_iota(jnp.int32, (q, q), 0)
    col = lax.broadcasted_iota(jnp.int32, (q, q), 1)
    tri = row >= col
    cs = _dot_exact_rhs(jnp.where(tri, 1.0, 0.0).astype(BF16), a)
    cs_last = cs[q - 1:q, :]
    cs2 = cs * LOG2E
    src_t = (cs2 - jnp.log2(dt)).T
    wdd = dt * jnp.exp(cs_last - cs)
    cdec = jnp.broadcast_to(jnp.exp(cs_last), (SUBLANES, LANES))
    cdec_x = _dot_exact_lhs(cdec, e_ref[...])[0:1, :]

    lane = lax.broadcasted_iota(jnp.int32, (q, LANES), 1)
    heads_per_group = H_SSM // G_SSM
    y_tiles = []
    for g in range(G_SSM):
        bg = bs[:, g * N_SSM:(g + 1) * N_SSM]
        cg = cs_in[:, g * N_SSM:(g + 1) * N_SSM]
        cb = _dot_nt(cg.astype(BF16), bg.astype(BF16))
        xdd_tiles = []
        for pr in range(heads_per_group // 2):
            hp = g * (heads_per_group // 2) + pr
            xpair = xs[:, hp * LANES:(hp + 1) * LANES]
            wp = jnp.concatenate([xpair, ht_scr[:, hp * LANES:(hp + 1) * LANES]], axis=0).astype(BF16)
            parts = []
            wsel = []
            for r2 in range(2):
                h = 2 * hp + r2
                csb = jnp.broadcast_to(cs2[:, h:h + 1], (q, LANES))
                mmat = cb * jnp.exp2(jnp.where(tri, csb - src_t[h:h + 1, :], -jnp.inf))
                csc = cg * jnp.exp2(csb)
                lhs = jnp.concatenate([mmat, csc], axis=1).astype(BF16)
                parts.append(_dot(lhs, wp))
                wsel.append(jnp.broadcast_to(wdd[:, h:h + 1], (q, LANES)))
            y_tiles.append(jnp.where(lane < SSM_HEADDIM, parts[0], parts[1]))
            xdd_tiles.append(xpair * jnp.where(lane < SSM_HEADDIM, wsel[0], wsel[1]))
        xdd = jnp.concatenate(xdd_tiles, axis=1).astype(BF16)
        st_t = _dot(bg.T.astype(BF16), xdd)
        lo, hi = g * heads_per_group * SSM_HEADDIM, (g + 1) * heads_per_group * SSM_HEADDIM
        ht_scr[:, lo:hi] = ht_scr[:, lo:hi] * cdec_x[:, lo:hi] + st_t

    y = jnp.concatenate(y_tiles, axis=1) + xs * dx_ref[...]
    y = y * _silu(zs_ref[...].astype(F32))
    y_ref[...] = _rms(y, nrm_ref[...]).astype(y_ref.dtype)

    @pl.when(c == pl.num_programs(1) - 1)
    def _():
        hout_ref[0] = ht_scr[...].T


def _ssd_prompt(z, dtr, p, nb, seq):
    q = SSD_CHUNK
    nc = seq // q
    row = lambda b, c: b * nc + c
    const = lambda b, c: (0, 0)
    return pl.pallas_call(
        _ssd_kernel,
        grid=(nb, nc),
        in_specs=[pl.BlockSpec((q, SSM_CONV_DIM), lambda b, c: (row(b, c), OFF_XBC // SSM_CONV_DIM)),
                  pl.BlockSpec((q, D_SSM), lambda b, c: (row(b, c), OFF_ZS // D_SSM)),
                  pl.BlockSpec((q, LANES), lambda b, c: (row(b, c), 0)),
                  pl.BlockSpec((SUBLANES, SSM_CONV_DIM), const),
                  pl.BlockSpec((1, SSM_CONV_DIM), const),
                  pl.BlockSpec((1, LANES), const),
                  pl.BlockSpec((1, LANES), const),
                  pl.BlockSpec((1, D_SSM), const),
                  pl.BlockSpec((1, D_SSM), const),
                  pl.BlockSpec((LANES, D_SSM), const)],
        out_specs=[pl.BlockSpec((q, D_SSM), lambda b, c: (row(b, c), 0)),
                   pl.BlockSpec((1, D_SSM, N_SSM), lambda b, c: (b, 0, 0)),
                   pl.BlockSpec((1, SUBLANES, SSM_CONV_DIM), lambda b, c: (b, 0, 0))],
        out_shape=[jax.ShapeDtypeStruct((nb * seq, D_SSM), BF16),
                   jax.ShapeDtypeStruct((nb, D_SSM, N_SSM), F32),
                   jax.ShapeDtypeStruct((nb, SUBLANES, SSM_CONV_DIM), F32)],
        scratch_shapes=[pltpu.VMEM((2 * SUBLANES, SSM_CONV_DIM), BF16),
                        pltpu.VMEM((N_SSM, D_SSM), F32)],
        compiler_params=_cparams(2),
        name="ssd_prompt",
    )(z, z, dtr, p["ssm_conv_w"], p["ssm_conv_b"], p["dt_bias"], p["a_log"], p["d_x"], p["ssm_norm"],
      p["expand"])


def _swa_kernel(sink_ref, q_ref, k_ref, v_ref, c_ref, s1_ref, s2_ref, o_ref, kn_ref, vn_ref,
                kp_scr, vp_scr):
    qb = pl.program_id(1)
    w = WINDOW
    kvw = KV_ATT * HD_ATT

    @pl.when(qb == 0)
    def _():
        kp_scr[...] = jnp.zeros((w, kvw), F32)
        vp_scr[...] = jnp.zeros((w, kvw), F32)

    c, s1, s2 = c_ref[...], s1_ref[...], s2_ref[...]
    qv = _rope(q_ref[...].astype(F32), c, s1, s2) * (HD_ATT ** -0.5)
    ko = _rope(k_ref[...].astype(F32), c, s1, s2)
    vo = v_ref[...].astype(F32)
    kn_ref[0] = ko
    vn_ref[0] = vo
    kcat = jnp.concatenate([kp_scr[...], ko], axis=0)
    vcat = jnp.concatenate([vp_scr[...], vo], axis=0)
    kp_scr[...] = ko
    vp_scr[...] = vo
    vcat_t = vcat.T.astype(BF16)

    rep = H_ATT // KV_ATT
    keyj = lax.broadcasted_iota(jnp.int32, (2 * w, w), 0)
    qryl = lax.broadcasted_iota(jnp.int32, (2 * w, w), 1)
    valid = (keyj > qryl) & (keyj <= qryl + w) & ((qb > 0) | (keyj >= w))
    valid = jnp.concatenate([valid] * rep, axis=1)
    lane_kv = lax.broadcasted_iota(jnp.int32, (2 * w, LANES), 1)
    lane_q = lax.broadcasted_iota(jnp.int32, (w, LANES), 1)
    lo_kv, lo_q = lane_kv < HD_ATT, lane_q < HD_ATT

    scores, sinks = [], []
    for slab in range(kvw // LANES):
        a_k = kcat[:, slab * LANES:(slab + 1) * LANES]
        b_k = pltpu.roll(a_k, HD_ATT, 1)
        for gi in range(2):
            g = 2 * slab + gi
            kdup = jnp.where(lo_kv, a_k, b_k) if gi == 0 else jnp.where(lo_kv, b_k, a_k)
            q_tiles, sink_tiles = [], []
            for pr in range(rep // 2):
                qp = qv[:, (rep // 2 * g + pr) * LANES:(rep // 2 * g + pr + 1) * LANES]
                q_tiles.append(jnp.where(lo_q, qp, 0.0))
                q_tiles.append(jnp.where(lo_q, 0.0, qp))
                for r2 in range(2):
                    sink_tiles.append(jnp.full((1, w), sink_ref[rep * g + 2 * pr + r2], F32))
            qs = jnp.concatenate(q_tiles, axis=0).astype(BF16)
            sinks.append(jnp.concatenate(sink_tiles, axis=1))
            scores.append(_dot_nt(kdup.astype(BF16), qs))
    probs, dens = [], []
    for g in range(KV_ATT):
        s = jnp.where(valid, scores[g], -jnp.inf)
        m = jnp.maximum(jnp.max(s, axis=0, keepdims=True), sinks[g])
        e = jnp.exp(s - m)
        dens.append(jnp.sum(e, axis=0, keepdims=True) + jnp.exp(sinks[g] - m))
        probs.append(e.astype(BF16))
    o_t = []
    for g in range(KV_ATT):
        o_g = _dot(vcat_t[g * HD_ATT:(g + 1) * HD_ATT, :], probs[g]) * (1.0 / dens[g])
        for r in range(rep):
            o_t.append(o_g[:, r * w:(r + 1) * w])
    o_ref[...] = jnp.concatenate(o_t, axis=0).T.astype(o_ref.dtype)


def _swa_prompt(z, p, nb, seq):
    w = WINDOW
    nq = seq // w
    kvw = KV_ATT * HD_ATT
    row = lambda b, i: b * nq + i
    tab = pl.BlockSpec((w, LANES), lambda b, i: (i, 0))
    return pl.pallas_call(
        _swa_kernel,
        grid=(nb, nq),
        in_specs=[pl.BlockSpec(memory_space=pltpu.SMEM),
                  pl.BlockSpec((w, H_ATT * HD_ATT), lambda b, i: (row(b, i), OFF_Q // (H_ATT * HD_ATT))),
                  pl.BlockSpec((w, kvw), lambda b, i: (row(b, i), OFF_K // kvw)),
                  pl.BlockSpec((w, kvw), lambda b, i: (row(b, i), OFF_V // kvw)),
                  tab, tab, tab],
        out_specs=[pl.BlockSpec((w, H_ATT * HD_ATT), lambda b, i: (row(b, i), 0)),
                   pl.BlockSpec((1, w, kvw), lambda b, i: (b, 0, 0)),
                   pl.BlockSpec((1, w, kvw), lambda b, i: (b, 0, 0))],
        out_shape=[jax.ShapeDtypeStruct((nb * seq, H_ATT * HD_ATT), BF16),
                   jax.ShapeDtypeStruct((nb, w, kvw), F32),
                   jax.ShapeDtypeStruct((nb, w, kvw), F32)],
        scratch_shapes=[pltpu.VMEM((w, kvw), F32), pltpu.VMEM((w, kvw), F32)],
        compiler_params=_cparams(2),
        name="swa_prompt",
    )(p["sinks"], z, z, z, p["rope_c"], p["rope_s1"], p["rope_s2"])


def _tail_body(u, gb, ys, oc, ga, gbg, gcg, x, wa_ref, ws_ref, wc_ref, wo_ref, gpost_ref):
    ta = (gb * u).astype(BF16)
    out = None
    half = D_MODEL // 2
    for lo in range(0, D_MODEL, half):
        hi = lo + half
        y_a = _dot(ta, wa_ref[:, lo:hi])
        y_b = _dot(ys, ws_ref[:, lo:hi])
        y_c = _dot(oc, wc_ref[:, lo:hi])
        merged = (_sigmoid(ga[:, lo:hi]) * y_a + _sigmoid(gbg[:, lo:hi]) * y_b
                  + _sigmoid(gcg[:, lo:hi]) * y_c)
        d = _dot(merged.astype(BF16), wo_ref[lo:hi, :])
        out = d if out is None else out + d
    return x + _rms(out, gpost_ref[...])


def _tail_prompt_kernel(va_ref, gb_ref, gc_ref, ys_ref, oc_ref, ga_ref, gbg_ref, gcg_ref, x_ref,
                        cw_ref, wa_ref, ws_ref, wc_ref, wo_ref, gpost_ref, xo_ref, bufo_ref, cv_scr):
    s = pl.program_id(1)
    tm = x_ref.shape[0]
    pad = SUBLANES

    @pl.when(s == 0)
    def _():
        cv_scr[0:pad, :] = jnp.zeros((pad, D_A), F32)

    cv = gc_ref[...].astype(F32) * va_ref[...].astype(F32)
    cv_scr[pad:pad + tm, :] = cv
    cw = cw_ref[...]
    u = cv * cw[CONV_A_W - 1:CONV_A_W, :]
    for k in range(CONV_A_W - 1):
        off = pad - (CONV_A_W - 1) + k
        u = u + cv_scr[off:off + tm, :] * cw[k:k + 1, :]
    tail = cv_scr[tm:tm + pad, :]
    cv_scr[0:pad, :] = tail
    bufo_ref[0] = tail
    xo_ref[...] = _tail_body(u, gb_ref[...].astype(F32), ys_ref[...], oc_ref[...],
                             ga_ref[...].astype(F32), gbg_ref[...].astype(F32), gcg_ref[...].astype(F32),
                             x_ref[...], wa_ref, ws_ref, wc_ref, wo_ref, gpost_ref)


def _tail_prompt(z, ys, oc, x, p, nb, seq, tm):
    ns = seq // tm
    row = lambda b, s: b * ns + s
    const = lambda b, s: (0, 0)
    zblk = lambda off: pl.BlockSpec((tm, D_MODEL), lambda b, s: (row(b, s), off // D_MODEL))
    return pl.pallas_call(
        _tail_prompt_kernel,
        grid=(nb, ns),
        in_specs=[zblk(OFF_ZA), zblk(OFF_ZA + D_A), zblk(OFF_ZA + 2 * D_A),
                  pl.BlockSpec((tm, D_SSM), lambda b, s: (row(b, s), 0)),
                  pl.BlockSpec((tm, H_ATT * HD_ATT), lambda b, s: (row(b, s), 0)),
                  zblk(OFF_GATES), zblk(OFF_GATES + D_MODEL), zblk(OFF_GATES + 2 * D_MODEL),
                  pl.BlockSpec((tm, D_MODEL), lambda b, s: (row(b, s), 0)),
                  pl.BlockSpec((SUBLANES, D_A), const),
                  _wspec(p["layer"], (D_A, D_MODEL), const, resident=True),
                  _wspec(p["layer"], (D_SSM, D_MODEL), const, resident=True),
                  _wspec(p["layer"], (H_ATT * HD_ATT, D_MODEL), const, resident=True),
                  _wspec(p["layer"], (D_MODEL, D_MODEL), const, resident=True),
                  pl.BlockSpec((1, D_MODEL), const)],
        out_specs=[pl.BlockSpec((tm, D_MODEL), lambda b, s: (row(b, s), 0)),
                   pl.BlockSpec((1, SUBLANES, D_A), lambda b, s: (b, 0, 0))],
        out_shape=[jax.ShapeDtypeStruct((nb * seq, D_MODEL), F32),
                   jax.ShapeDtypeStruct((nb, SUBLANES, D_A), F32)],
        scratch_shapes=[pltpu.VMEM((tm + SUBLANES, D_A), F32)],
        compiler_params=_cparams(2, VMEM_LIMIT_BIG),
        name="tail_prompt",
    )(z, z, z, ys, oc, z, z, z, x, p["conv_a_w"], p["w_a_out"], p["w_ssm_out"], p["w_attn_out"],
      p["w_out"], p["norm_mix_post"])


def _tail_sample_kernel(va_ref, gb_ref, gc_ref, ys_ref, oc_ref, ga_ref, gbg_ref, gcg_ref, x_ref,
                        b0_ref, b1_ref, cw_ref, wa_ref, ws_ref, wc_ref, wo_ref, gpost_ref,
                        xo_ref, cvo_ref):
    cv = gc_ref[...].astype(F32) * va_ref[...].astype(F32)
    cw = cw_ref[...]
    u = b0_ref[...] * cw[0:1, :] + b1_ref[...] * cw[1:2, :] + cv * cw[2:3, :]
    cvo_ref[...] = cv
    xo_ref[...] = _tail_body(u, gb_ref[...].astype(F32), ys_ref[...], oc_ref[...],
                             ga_ref[...].astype(F32), gbg_ref[...].astype(F32), gcg_ref[...].astype(F32),
                             x_ref[...], wa_ref, ws_ref, wc_ref, wo_ref, gpost_ref)


def _tail_sample(z, ys, oc, x, b0, b1, p):
    m = x.shape[0]
    const = lambda i: (0, 0)
    zblk = lambda off: pl.BlockSpec((m, D_MODEL), lambda i: (0, off // D_MODEL))
    full = lambda a: pl.BlockSpec(a.shape, const)
    wfull = lambda a: _wspec(p["layer"], a.shape[1:], const)
    w_c = p["w_attn_out_exp"]
    return pl.pallas_call(
        _tail_sample_kernel,
        grid=(1,),
        in_specs=[zblk(OFF_ZA), zblk(OFF_ZA + D_A), zblk(OFF_ZA + 2 * D_A),
                  full(ys), full(oc),
                  zblk(OFF_GATES), zblk(OFF_GATES + D_MODEL), zblk(OFF_GATES + 2 * D_MODEL),
                  full(x), full(b0), full(b1),
                  full(p["conv_a_w"]), wfull(p["w_a_out"]), wfull(p["w_ssm_out"]), wfull(w_c),
                  wfull(p["w_out"]), full(p["norm_mix_post"])],
        out_specs=[pl.BlockSpec((m, D_MODEL), const), pl.BlockSpec((m, D_A), const)],
        out_shape=[jax.ShapeDtypeStruct((m, D_MODEL), F32), jax.ShapeDtypeStruct((m, D_A), F32)],
        compiler_params=_cparams(1, VMEM_LIMIT_BIG),
        name="tail_sample",
    )(z, z, z, ys, oc, z, z, z, x, b0, b1, p["conv_a_w"], p["w_a_out"], p["w_ssm_out"], w_c,
      p["w_out"], p["norm_mix_post"])


def _xattn_prompt_kernel(x_ref, k_ref, v_ref, gpre_ref, wq_ref, wo_ref, gpost_ref, xo_ref):
    x = x_ref[...]
    h = _rms(x, gpre_ref[...]).astype(BF16)
    qv = (_dot(h, wq_ref[...]) * (X_HD ** -0.5)).astype(BF16)
    tm = x.shape[0]
    s = jnp.concatenate(
        [_dot_nt(qv[:, hh * X_HD:(hh + 1) * X_HD], k_ref[:, hh * X_HD:(hh + 1) * X_HD].astype(BF16))
         for hh in range(X_H)], axis=0)
    m = jnp.max(s, axis=-1, keepdims=True)
    e = jnp.exp(s - m)
    pb = (e * (1.0 / jnp.sum(e, axis=-1, keepdims=True))).astype(BF16)
    o = jnp.concatenate(
        [_dot(pb[hh * tm:(hh + 1) * tm], v_ref[:, hh * X_HD:(hh + 1) * X_HD].astype(BF16))
         for hh in range(X_H)], axis=1).astype(BF16)
    xo_ref[...] = x + _rms(_dot(o, wo_ref[...]), gpost_ref[...])


def _xattn_prompt(x, mk, mv, p, nb, seq, mem_len, tm):
    ns = seq // tm
    const = lambda b, s: (0, 0)
    return pl.pallas_call(
        _xattn_prompt_kernel,
        grid=(nb, ns),
        in_specs=[pl.BlockSpec((tm, D_MODEL), lambda b, s: (b * ns + s, 0)),
                  pl.BlockSpec((mem_len, X_H * X_HD), lambda b, s: (b, 0)),
                  pl.BlockSpec((mem_len, X_H * X_HD), lambda b, s: (b, 0)),
                  pl.BlockSpec((1, D_MODEL), const),
                  _wspec(p["layer"], (D_MODEL, X_H * X_HD), const, resident=True),
                  _wspec(p["layer"], (X_H * X_HD, D_MODEL), const, resident=True),
                  pl.BlockSpec((1, D_MODEL), const)],
        out_specs=pl.BlockSpec((tm, D_MODEL), lambda b, s: (b * ns + s, 0)),
        out_shape=jax.ShapeDtypeStruct((nb * seq, D_MODEL), F32),
        compiler_params=_cparams(2),
        name="xattn_prompt",
    )(x, mk, mv, p["norm_x_pre"], p["w_xq"], p["w_xo"], p["norm_x_post"])


def _ffn_finish(kf, part, x_ref, gpost_ref, xo_ref, acc_scr):
    @pl.when(kf == 0)
    def _():
        acc_scr[...] = part

    @pl.when(kf > 0)
    def _():
        acc_scr[...] = acc_scr[...] + part

    @pl.when(kf == pl.num_programs(2) - 1)
    def _():
        xo_ref[...] = x_ref[...] + _rms(acc_scr[...], gpost_ref[...])


def _ffn_prompt_kernel(x_ref, gpre_ref, win_ref, cw_ref, cb_ref, wo_ref, gpost_ref,
                       xo_ref, bufo_ref, a_scr):
    s = pl.program_id(1)
    tm = x_ref.shape[0]
    pad = SUBLANES

    @pl.when(s == 0)
    def _():
        a_scr[0:pad, :] = jnp.zeros((pad, D_FF), F32)

    x = x_ref[...]
    h = _rms(x, gpre_ref[...]).astype(BF16)
    cw = cw_ref[...]
    cb = cb_ref[...]

    def up(lo, hi):
        return _dot(h, win_ref[:, lo:hi]), _dot(h, win_ref[:, D_FF + lo:D_FF + hi])

    bounds = [(lo, min(lo + FFN_SUB, D_FF)) for lo in range(0, D_FF, FFN_SUB)]
    out = None
    nxt = up(*bounds[0])
    for i, (lo, hi) in enumerate(bounds):
        a, gate = nxt
        if i + 1 < len(bounds):
            nxt = up(*bounds[i + 1])
        a_scr[pad:pad + tm, lo:hi] = a
        ac = cb[:, lo:hi] + a * cw[FFN_CONV_W - 1:FFN_CONV_W, lo:hi]
        for k in range(FFN_CONV_W - 1):
            off = pad - (FFN_CONV_W - 1) + k
            ac = ac + a_scr[off:off + tm, lo:hi] * cw[k:k + 1, lo:hi]
        d = _dot((_silu(ac) * gate).astype(BF16), wo_ref[lo:hi, :])
        out = d if out is None else out + d
    tail = a_scr[tm:tm + pad, :]
    a_scr[0:pad, :] = tail
    bufo_ref[0] = tail
    xo_ref[...] = x + _rms(out, gpost_ref[...])


def _ffn_prompt(x, p, nb, seq, tm):
    ns = seq // tm
    const = lambda b, s: (0, 0)
    return pl.pallas_call(
        _ffn_prompt_kernel,
        grid=(nb, ns),
        in_specs=[pl.BlockSpec((tm, D_MODEL), lambda b, s: (b * ns + s, 0)),
                  pl.BlockSpec((1, D_MODEL), const),
                  _wspec(p["layer"], (D_MODEL, 2 * D_FF), const, resident=True),
                  pl.BlockSpec((SUBLANES, D_FF), const),
                  pl.BlockSpec((1, D_FF), const),
                  _wspec(p["layer"], (D_FF, D_MODEL), const, resident=True),
                  pl.BlockSpec((1, D_MODEL), const)],
        out_specs=[pl.BlockSpec((tm, D_MODEL), lambda b, s: (b * ns + s, 0)),
                   pl.BlockSpec((1, SUBLANES, D_FF), lambda b, s: (b, 0, 0))],
        out_shape=[jax.ShapeDtypeStruct((nb * seq, D_MODEL), F32),
                   jax.ShapeDtypeStruct((nb, SUBLANES, D_FF), F32)],
        scratch_shapes=[pltpu.VMEM((tm + SUBLANES, D_FF), F32)],
        compiler_params=_cparams(2, VMEM_LIMIT_BIG),
        name="ffn_prompt",
    )(x, p["norm_ffn_pre"], p["w_ffn_in"], p["ffn_conv_w"], p["ffn_conv_b"],
      p["w_ffn_out"], p["norm_ffn_post"])


def _ffn_sample_kernel(x_ref, gpre_ref, wa_ref, wg_ref, cw_ref, cb_ref, b0_ref, b1_ref, wo_ref,
                       gpost_ref, xo_ref, ao_ref, h_scr, acc_scr):
    kf = pl.program_id(2)

    @pl.when(kf == 0)
    def _():
        h_scr[...] = _rms(x_ref[...], gpre_ref[...]).astype(BF16)

    h = h_scr[...]
    a = _dot(h, wa_ref[...])
    gate = _dot(h, wg_ref[...])
    ao_ref[...] = a
    cw = cw_ref[...]
    ac = cb_ref[...] + b0_ref[...] * cw[0:1, :] + b1_ref[...] * cw[1:2, :] + a * cw[2:3, :]
    part = _dot((_silu(ac) * gate).astype(BF16), wo_ref[...])
    _ffn_finish(kf, part, x_ref, gpost_ref, xo_ref, acc_scr)


def _ffn_sample(x, b0, b1, p):
    m = x.shape[0]
    nk = D_FF // FF_CHUNK
    const = lambda b, s, k: (0, 0)
    chunk = lambda b, s, k: (0, k)
    return pl.pallas_call(
        _ffn_sample_kernel,
        grid=(1, 1, nk),
        in_specs=[pl.BlockSpec((m, D_MODEL), const),
                  pl.BlockSpec((1, D_MODEL), const),
                  _wspec(p["layer"], (D_MODEL, FF_CHUNK), chunk),
                  _wspec(p["layer"], (D_MODEL, FF_CHUNK), lambda b, s, k: (0, nk + k)),
                  pl.BlockSpec((SUBLANES, FF_CHUNK), chunk),
                  pl.BlockSpec((1, FF_CHUNK), chunk),
                  pl.BlockSpec((m, FF_CHUNK), chunk),
                  pl.BlockSpec((m, FF_CHUNK), chunk),
                  _wspec(p["layer"], (FF_CHUNK, D_MODEL), lambda b, s, k: (k, 0)),
                  pl.BlockSpec((1, D_MODEL), const)],
        out_specs=[pl.BlockSpec((m, D_MODEL), const),
                   pl.BlockSpec((m, FF_CHUNK), chunk)],
        out_shape=[jax.ShapeDtypeStruct((m, D_MODEL), F32),
                   jax.ShapeDtypeStruct((m, D_FF), F32)],
        scratch_shapes=[pltpu.VMEM((m, D_MODEL), BF16),
                        pltpu.VMEM((m, D_MODEL), F32)],
        compiler_params=_cparams(3),
        name="ffn_sample",
    )(x, p["norm_ffn_pre"], p["w_ffn_in"], p["w_ffn_in"], p["ffn_conv_w"], p["ffn_conv_b"], b0, b1,
      p["w_ffn_out"], p["norm_ffn_post"])


def _dec_prep_kernel(xbc_ref, dtr_ref, b0_ref, b1_ref, b2_ref, cw_ref, cb_ref, dtb_ref, alog_ref,
                     dx_ref, e_ref, xdt_ref, dec_ref, bs_ref, cs_ref, xsd_ref):
    cw = cw_ref[...]
    xc = (cb_ref[...] + b0_ref[...] * cw[0:1, :] + b1_ref[...] * cw[1:2, :] + b2_ref[...] * cw[2:3, :]
          + xbc_ref[...].astype(F32) * cw[3:4, :])
    xc = _silu(xc)
    xs = xc[:, :D_SSM]
    bs_ref[...] = xc[:, D_SSM:D_SSM + G_SSM * N_SSM]
    cs_ref[...] = xc[:, D_SSM + G_SSM * N_SSM:]
    dt = _softplus(dtr_ref[...] + dtb_ref[...])
    dec_ref[...] = jnp.exp(dt * (-jnp.exp(alog_ref[...])))
    xdt_ref[...] = xs * _dot_exact_lhs(dt, e_ref[...])
    xsd_ref[...] = xs * dx_ref[...]


def _dec_prep(z, dtr, b0, b1, b2, p):
    m = dtr.shape[0]
    const = lambda i: (0, 0)
    full = lambda a: pl.BlockSpec(a.shape, const)
    return pl.pallas_call(
        _dec_prep_kernel,
        grid=(1,),
        in_specs=[pl.BlockSpec((m, SSM_CONV_DIM), lambda i: (0, OFF_XBC // SSM_CONV_DIM)),
                  full(dtr), full(b0), full(b1), full(b2), full(p["ssm_conv_w"]), full(p["ssm_conv_b"]),
                  full(p["dt_bias"]), full(p["a_log"]), full(p["d_x"]), full(p["expand"])],
        out_specs=[pl.BlockSpec((m, D_SSM), const), pl.BlockSpec((m, LANES), const),
                   pl.BlockSpec((m, G_SSM * N_SSM), const), pl.BlockSpec((m, G_SSM * N_SSM), const),
                   pl.BlockSpec((m, D_SSM), const)],
        out_shape=[jax.ShapeDtypeStruct((m, D_SSM), F32), jax.ShapeDtypeStruct((m, LANES), F32),
                   jax.ShapeDtypeStruct((m, G_SSM * N_SSM), F32),
                   jax.ShapeDtypeStruct((m, G_SSM * N_SSM), F32),
                   jax.ShapeDtypeStruct((m, D_SSM), F32)],
        compiler_params=_cparams(1),
        name="dec_prep",
    )(z, dtr, b0, b1, b2, p["ssm_conv_w"], p["ssm_conv_b"], p["dt_bias"], p["a_log"], p["d_x"],
      p["expand"])


def _dec_state_kernel(dec_ref, st_ref, xdt_ref, bs_ref, cs_ref, *rest):
    so_ref, y_ref = rest[-2:]
    bt = st_ref.shape[0]
    rows_per_group = D_SSM // G_SSM
    grp = lax.broadcasted_iota(jnp.int32, (G_SSM, D_SSM), 0)
    own = (lax.broadcasted_iota(jnp.int32, (G_SSM, D_SSM), 1) // rows_per_group) == grp
    for i in range(bt):
        b = pl.program_id(0) * bt + i
        x = xdt_ref[pl.ds(b, 1), :]
        x8 = jnp.where(own, jnp.broadcast_to(x, own.shape), 0.0).astype(BF16)
        outer = lax.dot_general(x8, bs_ref[b].astype(BF16), (((0,), (0,)), ((), ())),
                                preferred_element_type=F32)
        for h in range(H_SSM):
            lo, hi = h * SSM_HEADDIM, (h + 1) * SSM_HEADDIM
            so_ref[i, lo:hi, :] = st_ref[i, lo:hi, :] * dec_ref[b, h] + outer[lo:hi, :]
        y8 = _dot_nt(cs_ref[b].astype(BF16), so_ref[i].astype(BF16))
        y_ref[pl.ds(b, 1), :] = jnp.sum(jnp.where(own, y8, 0.0), axis=0, keepdims=True)


def _dec_state(state_all, layer, dec, xdt, bs3, cs3, carry):
    nb = xdt.shape[0]
    bt = _pick_tile(nb, 4)
    const2 = lambda b: (0, 0)
    const3 = lambda b: (0, 0, 0)
    slab = lambda b: (layer * (nb // bt) + b, 0, 0)
    extra = [] if carry is None else [carry]
    return pl.pallas_call(
        _dec_state_kernel,
        grid=(nb // bt,),
        in_specs=[pl.BlockSpec(memory_space=pltpu.SMEM),
                  pl.BlockSpec((bt, D_SSM, N_SSM), slab),
                  pl.BlockSpec(xdt.shape, const2),
                  pl.BlockSpec(bs3.shape, const3), pl.BlockSpec(cs3.shape, const3)]
                 + [pl.BlockSpec(memory_space=pl.ANY)] * len(extra),
        out_specs=[pl.BlockSpec((bt, D_SSM, N_SSM), slab),
                   pl.BlockSpec((nb, D_SSM), const2)],
        out_shape=[jax.ShapeDtypeStruct(state_all.shape, F32),
                   jax.ShapeDtypeStruct((nb, D_SSM), F32)],
        input_output_aliases={5: 0} if extra else {},
        compiler_params=_cparams(1),
        name="dec_state",
    )(dec, state_all, xdt, bs3, cs3, *extra)


def _dec_post_kernel(y_ref, xsd_ref, zs_ref, nrm_ref, o_ref):
    y = y_ref[...] + xsd_ref[...]
    y = y * _silu(zs_ref[...].astype(F32))
    o_ref[...] = _rms(y, nrm_ref[...]).astype(o_ref.dtype)


def _dec_post(yt, xsd, z, p):
    m = xsd.shape[0]
    const = lambda i: (0, 0)
    return pl.pallas_call(
        _dec_post_kernel,
        grid=(1,),
        in_specs=[pl.BlockSpec(yt.shape, const), pl.BlockSpec(xsd.shape, const),
                  pl.BlockSpec((m, D_SSM), lambda i: (0, OFF_ZS // D_SSM)),
                  pl.BlockSpec((1, D_SSM), const)],
        out_specs=pl.BlockSpec((m, D_SSM), const),
        out_shape=jax.ShapeDtypeStruct((m, D_SSM), BF16),
        compiler_params=_cparams(1),
        name="dec_post",
    )(yt, xsd, z, p["ssm_norm"])


def _dec_swa_kernel(qe_ref, kn_ref, vn_ref, ck_ref, cv_ref, c_ref, s1_ref, s2_ref, sink_ref, *rest):
    nk_ref, nv_ref, oe_ref = rest[-3:]
    bt = ck_ref.shape[0]
    w = ck_ref.shape[1]
    c, s1, s2 = c_ref[0:1, :], s1_ref[0:1, :], s2_ref[0:1, :]
    kn = _rope(kn_ref[...].astype(F32), c, s1, s2)
    vn = vn_ref[...].astype(F32)
    last = lax.broadcasted_iota(jnp.int32, (w, KV_ATT * HD_ATT), 0) == w - 1
    nh = qe_ref.shape[1]
    qe = (_rope(qe_ref[...].astype(F32).reshape(bt * nh, KV_ATT * HD_ATT), c, s1, s2)
          * (HD_ATT ** -0.5)).astype(BF16)
    nvs, s_tiles = [], []
    for i in range(bt):
        nk = jnp.where(last, kn[i:i + 1, :], pltpu.roll(ck_ref[i], w - 1, 0))
        nv = jnp.where(last, vn[i:i + 1, :], pltpu.roll(cv_ref[i], w - 1, 0))
        nk_ref[i] = nk
        nv_ref[i] = nv
        nvs.append(nv.astype(BF16))
        s_tiles.append(_dot_nt(qe[i * nh:(i + 1) * nh], nk.astype(BF16)))
    s = jnp.concatenate(s_tiles, axis=0)
    sink = jnp.concatenate([sink_ref[:, 0:1]] * bt, axis=0)
    m = jnp.maximum(jnp.max(s, axis=-1, keepdims=True), sink)
    e = jnp.exp(s - m)
    den = jnp.sum(e, axis=-1, keepdims=True) + jnp.exp(sink - m)
    pb = (e * (1.0 / den)).astype(BF16)
    for i in range(bt):
        oe_ref[i] = _dot(pb[i * nh:(i + 1) * nh], nvs[i]).astype(oe_ref.dtype)


def _dec_swa(qe, z, ck_all, cv_all, layer, p, bt, carry):
    nb = qe.shape[0]
    _, w, kvw = ck_all.shape
    const = lambda i: (0, 0)
    slab = lambda i: (layer * (nb // bt) + i, 0, 0)
    tab = pl.BlockSpec((SUBLANES, LANES), const)
    extra = [] if carry is None else list(carry)
    return pl.pallas_call(
        _dec_swa_kernel,
        grid=(nb // bt,),
        in_specs=[pl.BlockSpec((bt, H_ATT, kvw), lambda i: (i, 0, 0)),
                  pl.BlockSpec((bt, kvw), lambda i: (i, OFF_K // kvw)),
                  pl.BlockSpec((bt, kvw), lambda i: (i, OFF_V // kvw)),
                  pl.BlockSpec((bt, w, kvw), slab),
                  pl.BlockSpec((bt, w, kvw), slab),
                  tab, tab, tab,
                  pl.BlockSpec((H_ATT, LANES), const)]
                 + [pl.BlockSpec(memory_space=pl.ANY)] * len(extra),
        out_specs=[pl.BlockSpec((bt, w, kvw), slab),
                   pl.BlockSpec((bt, w, kvw), slab),
                   pl.BlockSpec((bt, H_ATT, kvw), lambda i: (i, 0, 0))],
        out_shape=[jax.ShapeDtypeStruct(ck_all.shape, F32), jax.ShapeDtypeStruct(cv_all.shape, F32),
                   jax.ShapeDtypeStruct((nb, H_ATT, kvw), BF16)],
        input_output_aliases={9: 0, 10: 1} if extra else {},
        compiler_params=_cparams(1),
        name="dec_swa",
    )(qe, z, z, ck_all, cv_all, p["rope_c_s"], p["rope_s1_s"], p["rope_s2_s"], p["sinks_x"], *extra)


def _dec_xattn_kernel(q_ref, k_ref, v_ref, o_ref):
    bt, mem_len = k_ref.shape[1], k_ref.shape[2]
    rows = mem_len * X_H
    col_head = lax.broadcasted_iota(jnp.int32, (SUBLANES, rows), 1) % X_H
    own = col_head == lax.broadcasted_iota(jnp.int32, (SUBLANES, rows), 0) % X_H
    for i in range(bt):
        qh = q_ref[i] * (X_HD ** -0.5)
        q8 = jnp.concatenate([qh] * (SUBLANES // X_H), axis=0).astype(BF16)
        k2 = k_ref[0, i].reshape(rows, X_HD).astype(BF16)
        v2 = v_ref[0, i].reshape(rows, X_HD).astype(BF16)
        s = _dot_nt(q8, k2)
        s = jnp.where(own, s, -jnp.inf)
        m = jnp.max(s, axis=-1, keepdims=True)
        e = jnp.exp(s - m)
        den = jnp.sum(e, axis=-1, keepdims=True)
        o8 = _dot(e.astype(BF16), v2) * (1.0 / den)
        o_ref[i] = o8[0:X_H, :]


def _dec_xattn(qv, mem_k, mem_v, layer, bt):
    nb = qv.shape[0]
    _, _, mem_len, nh, hd = mem_k.shape
    slab = lambda i: (layer, i, 0, 0, 0)
    return pl.pallas_call(
        _dec_xattn_kernel,
        grid=(nb // bt,),
        in_specs=[pl.BlockSpec((bt, nh, hd), lambda i: (i, 0, 0)),
                  pl.BlockSpec((1, bt, mem_len, nh, hd), slab),
                  pl.BlockSpec((1, bt, mem_len, nh, hd), slab)],
        out_specs=pl.BlockSpec((bt, nh, hd), lambda i: (i, 0, 0)),
        out_shape=jax.ShapeDtypeStruct((nb, nh, hd), F32),
        compiler_params=_cparams(1),
        name="dec_xattn",
    )(qv, mem_k, mem_v)


def _pad_rows(w, rows=SUBLANES):
    return jnp.pad(w, ((0, rows - w.shape[0]), (0, 0)))


def _pad_lanes(v, lanes=LANES):
    return jnp.pad(v, (0, lanes - v.shape[0])).reshape(1, lanes)


def _rope_tables(pos):
    half = ROT_DIM // 2
    inv = ROPE_THETA ** (-jnp.arange(half, dtype=F32) / half)
    ang = pos.astype(F32)[:, None] * inv[None, :]
    cos, sin = jnp.cos(ang), jnp.sin(ang)
    n = pos.shape[0]
    ones = jnp.ones((n, HD_ATT - ROT_DIM), F32)
    zeros = jnp.zeros((n, HD_ATT - ROT_DIM), F32)
    zh = jnp.zeros((n, half), F32)
    c = jnp.concatenate([cos, cos, ones], axis=1)
    s1 = jnp.concatenate([-sin, zh, zeros], axis=1)
    s2 = jnp.concatenate([zh, sin, zeros], axis=1)
    rep = LANES // HD_ATT
    return tuple(jnp.tile(t, (1, rep)) for t in (c, s1, s2))


_BF16_WEIGHTS = ("w_a_out", "w_ssm_out", "w_attn_out", "w_out", "w_xq", "w_xk", "w_xv", "w_xo",
                 "w_ffn_in", "w_ffn_out")


def _prep_weights(prm):
    w_in = prm["w_in"]
    depth = w_in.shape[0]
    o = np.cumsum([0, D_A, D_A, D_A, D_SSM, SSM_CONV_DIM, H_SSM, H_ATT * HD_ATT, KV_ATT * HD_ATT,
                   KV_ATT * HD_ATT, 3 * D_MODEL])
    col = lambda i: w_in[:, :, o[i]:o[i + 1]]
    w = {name: prm[name].astype(BF16) for name in _BF16_WEIGHTS}
    w["w_main"] = jnp.concatenate(
        [col(4), col(3), col(0), col(1), col(2), col(9), col(6), col(7), col(8)], axis=-1).astype(BF16)
    w["w_dt"] = jnp.pad(col(5), ((0, 0), (0, 0), (0, LANES - H_SSM))).astype(BF16)
    rep = H_ATT // KV_ATT
    wq, wc = col(6).astype(BF16), w["w_attn_out"]
    q_blocks, c_blocks = [], []
    for h in range(H_ATT):
        g = h // rep
        q_blocks += [jnp.zeros((depth, D_MODEL, g * HD_ATT), BF16),
                     wq[:, :, h * HD_ATT:(h + 1) * HD_ATT],
                     jnp.zeros((depth, D_MODEL, (KV_ATT - 1 - g) * HD_ATT), BF16)]
        c_blocks += [jnp.zeros((depth, g * HD_ATT, D_MODEL), BF16),
                     wc[:, h * HD_ATT:(h + 1) * HD_ATT, :],
                     jnp.zeros((depth, (KV_ATT - 1 - g) * HD_ATT, D_MODEL), BF16)]
    w["wq_exp"] = jnp.concatenate(q_blocks, axis=2)
    w["w_attn_out_exp"] = jnp.concatenate(c_blocks, axis=1)
    return w


def _prep_layer(l, prm, w, seq):
    expand = jnp.asarray(np.kron(np.eye(LANES, H_SSM, dtype=np.float32),
                                 np.ones((1, SSM_HEADDIM), np.float32))).astype(BF16)
    rope_p = _rope_tables(jnp.arange(seq, dtype=jnp.int32))
    rope_s = _rope_tables(jnp.full((SUBLANES,), PAST_LEN, jnp.int32))
    return {
        **w,
        "layer": l,
        "norm_mix_pre": prm["norm_mix_pre"][l].reshape(1, -1),
        "norm_mix_post": prm["norm_mix_post"][l].reshape(1, -1),
        "conv_a_w": _pad_rows(prm["conv_a_w"][l]),
        "ssm_conv_w": _pad_rows(prm["ssm_conv_w"][l]),
        "ssm_conv_b": prm["ssm_conv_b"][l].reshape(1, -1),
        "dt_bias": _pad_lanes(prm["ssm_dt_bias"][l]),
        "a_log": _pad_lanes(prm["ssm_a_log"][l]),
        "d_x": jnp.repeat(prm["ssm_d"][l], SSM_HEADDIM).reshape(1, -1),
        "ssm_norm": prm["ssm_norm"][l].reshape(1, -1),
        "expand": expand,
        "sinks": prm["attn_sinks"][l],
        "sinks_x": jnp.broadcast_to(prm["attn_sinks"][l][:, None], (H_ATT, LANES)),
        "rope_c": rope_p[0], "rope_s1": rope_p[1], "rope_s2": rope_p[2],
        "rope_c_s": rope_s[0], "rope_s1_s": rope_s[1], "rope_s2_s": rope_s[2],
        "norm_x_pre": prm["norm_x_pre"][l].reshape(1, -1),
        "norm_x_post": prm["norm_x_post"][l].reshape(1, -1),
        "norm_mem": prm["norm_mem"][l].reshape(1, -1),
        "norm_ffn_pre": prm["norm_ffn_pre"][l].reshape(1, -1),
        "norm_ffn_post": prm["norm_ffn_post"][l].reshape(1, -1),
        "ffn_conv_w": _pad_rows(prm["ffn_conv_w"][l]),
        "ffn_conv_b": prm["ffn_conv_b"][l].reshape(1, -1),
    }


def _pick_tile(n, pref):
    t = min(n, pref)
    while n % t:
        t //= 2
    return t


def _prompt_layer(x, mem, p, nb, seq, mem_len):
    rows = nb * seq
    z, dtr = _in_proj(x, p["norm_mix_pre"], p["w_main"], p["w_dt"], p["layer"], _pick_tile(rows, 1024),
                      2304)
    ys, h_last, cbuf = _ssd_prompt(z, dtr, p, nb, seq)
    oc, k_new, v_new = _swa_prompt(z, p, nb, seq)
    x, bufa = _tail_prompt(z, ys, oc, x, p, nb, seq, _pick_tile(seq, 512))
    tmem = _pick_tile(mem.shape[0], 512)
    mk = _norm_matmul(mem, p["norm_mem"], p["w_xk"], p["layer"], F32, tmem, X_H * X_HD)
    mv = _norm_matmul(mem, p["norm_mem"], p["w_xv"], p["layer"], F32, tmem, X_H * X_HD)
    x = _xattn_prompt(x, mk, mv, p, nb, seq, mem_len, _pick_tile(seq, 512))
    x, buff = _ffn_prompt(x, p, nb, seq, _pick_tile(seq, 512))
    state = (bufa[:, SUBLANES - (CONV_A_W - 1):],
             cbuf[:, SUBLANES - (SSM_CONV_W - 1):],
             h_last.reshape(nb, H_SSM, SSM_HEADDIM, N_SSM),
             k_new.reshape(nb, WINDOW, KV_ATT, HD_ATT),
             v_new.reshape(nb, WINDOW, KV_ATT, HD_ATT),
             buff[:, SUBLANES - (FFN_CONV_W - 1):],
             mk.reshape(nb, mem_len, X_H, X_HD),
             mv.reshape(nb, mem_len, X_H, X_HD))
    return x, state


def _sample_layer(x, layer, st, big, carry, p):
    buf_a, buf_ssm, buf_ffn = st
    ssm_all, swa_k_all, swa_v_all, mem_k_all, mem_v_all = big
    nb = x.shape[0]
    kvw = KV_ATT * HD_ATT
    z, dtr = _in_proj(x, p["norm_mix_pre"], p["w_main"], p["w_dt"], layer, nb, 1536)
    qe = _norm_matmul(x, p["norm_mix_pre"], p["wq_exp"], layer, BF16, nb, 1024).reshape(nb, H_ATT, kvw)
    xdt, dec, bs, cs, xsd = _dec_prep(z, dtr, buf_ssm[:, 0], buf_ssm[:, 1], buf_ssm[:, 2], p)
    new_ssm, y_ssd = _dec_state(ssm_all, layer, dec[:, :H_SSM], xdt, bs.reshape(nb, G_SSM, N_SSM),
                                cs.reshape(nb, G_SSM, N_SSM), None if carry is None else carry[0])
    ys = _dec_post(y_ssd, xsd, z, p)
    new_k, new_v, oe = _dec_swa(qe, z, swa_k_all, swa_v_all, layer, p, _pick_tile(nb, 16),
                                None if carry is None else carry[1:3])
    x, cv = _tail_sample(z, ys, oe.reshape(nb, H_ATT * kvw), x, buf_a[:, 0], buf_a[:, 1], p)
    qx = _norm_matmul(x, p["norm_x_pre"], p["w_xq"], layer, F32, nb, 1024)
    ox = _dec_xattn(qx.reshape(nb, X_H, X_HD), mem_k_all, mem_v_all, layer, _pick_tile(nb, 4))
    x = _mm_norm_res(ox.reshape(nb, X_H * X_HD), p["w_xo"], layer, p["norm_x_post"], x)
    x, a_up = _ffn_sample(x, buf_ffn[:, 0], buf_ffn[:, 1], p)
    x_raw = z[:, OFF_XBC:OFF_XBC + SSM_CONV_DIM].astype(F32)
    small = (jnp.stack([buf_a[:, 1], cv], axis=1),
             jnp.concatenate([buf_ssm[:, 1:], x_raw[:, None, :]], axis=1),
             jnp.stack([buf_ffn[:, 1], a_up], axis=1))
    return x, small, (new_ssm, new_k, new_v)


def kernel(x_prompt, x_sample, mem_prompt, state_conv_a, state_ssm_conv, state_ssm, cache_swa_k, cache_swa_v, cache_mem_k, cache_mem_v, state_ffn_conv, norm_mix_pre, norm_mix_post, w_in, conv_a_w, w_a_out, ssm_conv_w, ssm_conv_b, ssm_dt_bias, ssm_a_log, ssm_d, ssm_norm, w_ssm_out, attn_sinks, w_attn_out, w_out, norm_x_pre, norm_x_post, norm_mem, w_xq, w_xk, w_xv, w_xo, norm_ffn_pre, norm_ffn_post, w_ffn_in, ffn_conv_w, ffn_conv_b, w_ffn_out):
    prm = dict(norm_mix_pre=norm_mix_pre, norm_mix_post=norm_mix_post, w_in=w_in, conv_a_w=conv_a_w,
               w_a_out=w_a_out, ssm_conv_w=ssm_conv_w, ssm_conv_b=ssm_conv_b, ssm_dt_bias=ssm_dt_bias,
               ssm_a_log=ssm_a_log, ssm_d=ssm_d, ssm_norm=ssm_norm, w_ssm_out=w_ssm_out,
               attn_sinks=attn_sinks, w_attn_out=w_attn_out, w_out=w_out, norm_x_pre=norm_x_pre,
               norm_x_post=norm_x_post, norm_mem=norm_mem, w_xq=w_xq, w_xk=w_xk, w_xv=w_xv, w_xo=w_xo,
               norm_ffn_pre=norm_ffn_pre, norm_ffn_post=norm_ffn_post, w_ffn_in=w_ffn_in,
               ffn_conv_w=ffn_conv_w, ffn_conv_b=ffn_conv_b, w_ffn_out=w_ffn_out)
    nb, seq, d = x_prompt.shape
    ns = x_sample.shape[0]
    mem_len = mem_prompt.shape[1]
    depth = w_in.shape[0]
    assert x_sample.shape[1] == 1 and seq % WINDOW == 0 and seq % SSD_CHUNK == 0
    assert cache_swa_k.shape[2] == WINDOW and PAST_LEN >= WINDOW

    yp = x_prompt.reshape(nb * seq, d)
    ys = x_sample.reshape(ns, d)
    mem = mem_prompt.reshape(nb * mem_len, d)
    flat = lambda a, *tail: a.reshape((depth * ns,) + tail)
    kvw = KV_ATT * HD_ATT
    big = (flat(state_ssm, D_SSM, N_SSM),
           flat(cache_swa_k, cache_swa_k.shape[2], kvw), flat(cache_swa_v, cache_swa_v.shape[2], kvw),
           cache_mem_k, cache_mem_v)
    w = _prep_weights(prm)
    new_p, new_s, carry = [], [], None
    for l in range(depth):
        p = _prep_layer(l, prm, w, seq)
        yp, st_p = _prompt_layer(yp, mem, p, nb, seq, mem_len)
        new_p.append(st_p)
        ys, st_s, carry = _sample_layer(ys, l, (state_conv_a[l], state_ssm_conv[l], state_ffn_conv[l]),
                                        big, carry, p)
        new_s.append(st_s)
    stack = lambda lst, i: jnp.stack([s[i] for s in lst])
    s_ssm, s_swa_k, s_swa_v = (carry[0].reshape(state_ssm.shape), carry[1].reshape(cache_swa_k.shape),
                               carry[2].reshape(cache_swa_v.shape))
    return ((yp.reshape(nb, seq, d), ys.reshape(ns, 1, d))
            + tuple(stack(new_p, i) for i in range(8))
            + (stack(new_s, 0), stack(new_s, 1), s_ssm, s_swa_k, s_swa_v, stack(new_s, 2)))
```

```python
import functools

import numpy as np
import jax
import jax.numpy as jnp
from jax import lax
from jax.experimental import pallas as pl
from jax.experimental.pallas import tpu as pltpu

F32 = jnp.float32
BF16 = jnp.bfloat16

EPS = 1e-6
D_MODEL = 1024
D_A = D_MODEL
CONV_A_W = 3
D_SSM = 2 * D_MODEL
SSM_HEADDIM = 64
H_SSM = D_SSM // SSM_HEADDIM
G_SSM = 8
N_SSM = 128
SSM_CONV_W = 4
SSM_CONV_DIM = D_SSM + 2 * G_SSM * N_SSM
SSD_CHUNK = 128
H_ATT = 16
KV_ATT = 4
HD_ATT = 64
ROT_DIM = HD_ATT // 4
ROPE_THETA = 500000.0
WINDOW = 128
X_H = 4
X_HD = D_MODEL // X_H
D_FF = 2816
FFN_CONV_W = 3
PAST_LEN = 8192
LOG2E = 1.4426950408889634

LANES = 128
SUBLANES = 8
VMEM_LIMIT = 48 * 1024 * 1024
VMEM_LIMIT_BIG = 56 * 1024 * 1024

OFF_XBC = 0
OFF_ZS = OFF_XBC + SSM_CONV_DIM
OFF_ZA = OFF_ZS + D_SSM
OFF_GATES = OFF_ZA + 3 * D_A
OFF_Q = OFF_GATES + 3 * D_MODEL
OFF_K = OFF_Q + H_ATT * HD_ATT
OFF_V = OFF_K + KV_ATT * HD_ATT
N_MAIN = OFF_V + KV_ATT * HD_ATT
FF_CHUNK = D_FF // 2
FFN_SUB = 4 * LANES


def _cparams(n_axes, vmem=VMEM_LIMIT):
    return pltpu.CompilerParams(dimension_semantics=("arbitrary",) * n_axes,
                                vmem_limit_bytes=vmem)


def _resident(shape, index_map):
    return pl.BlockSpec(shape, index_map, pipeline_mode=pl.Buffered(1))


def _wspec(layer, shape, index_map, resident=False):
    imap = lambda *g: (layer,) + tuple(index_map(*g))
    if resident:
        return pl.BlockSpec((None,) + tuple(shape), imap, pipeline_mode=pl.Buffered(1))
    return pl.BlockSpec((None,) + tuple(shape), imap)


def _rms(x, g):
    return x * lax.rsqrt(jnp.mean(x * x, axis=-1, keepdims=True) + EPS) * g


def _sigmoid(x):
    return 0.5 + 0.5 * jnp.tanh(0.5 * x)


def _silu(x):
    hx = 0.5 * x
    return hx + hx * jnp.tanh(hx)


def _softplus(x):
    return jnp.maximum(x, 0.0) + jnp.log(1.0 + jnp.exp(-jnp.abs(x)))


def _dot(a, b):
    return jnp.dot(a, b, preferred_element_type=F32)


def _dot_nt(a, b):
    return lax.dot_general(a, b, (((1,), (1,)), ((), ())), preferred_element_type=F32)


def _dot_exact_rhs(a01, v):
    hi = v.astype(BF16)
    r1 = v - hi.astype(F32)
    mid = r1.astype(BF16)
    lo = (r1 - mid.astype(F32)).astype(BF16)
    return _dot(a01, hi) + _dot(a01, mid) + _dot(a01, lo)


def _dot_exact_lhs(v, b01):
    hi = v.astype(BF16)
    r1 = v - hi.astype(F32)
    mid = r1.astype(BF16)
    lo = (r1 - mid.astype(F32)).astype(BF16)
    return _dot(hi, b01) + _dot(mid, b01) + _dot(lo, b01)


def _rope(x, c, s1, s2):
    outs = []
    for i in range(x.shape[1] // LANES):
        xi = x[:, i * LANES:(i + 1) * LANES]
        outs.append(xi * c + pltpu.roll(xi, LANES - ROT_DIM // 2, 1) * s1
                    + pltpu.roll(xi, ROT_DIM // 2, 1) * s2)
    return outs[0] if len(outs) == 1 else jnp.concatenate(outs, axis=1)


def _norm_mm_kernel(x_ref, g_ref, w_ref, o_ref, h_scr):
    @pl.when(pl.program_id(1) == 0)
    def _():
        h_scr[...] = _rms(x_ref[...], g_ref[...]).astype(BF16)

    o_ref[...] = _dot(h_scr[...], w_ref[...]).astype(o_ref.dtype)


def _norm_matmul(x, g, w, layer, out_dtype, tm, tn):
    m, k = x.shape
    n = w.shape[2]
    return pl.pallas_call(
        _norm_mm_kernel,
        grid=(m // tm, n // tn),
        in_specs=[pl.BlockSpec((tm, k), lambda i, j: (i, 0)),
                  pl.BlockSpec((1, k), lambda i, j: (0, 0)),
                  _wspec(layer, (k, tn), lambda i, j: (0, j))],
        out_specs=pl.BlockSpec((tm, tn), lambda i, j: (i, j)),
        out_shape=jax.ShapeDtypeStruct((m, n), out_dtype),
        scratch_shapes=[pltpu.VMEM((tm, k), BF16)],
        compiler_params=_cparams(2),
        name="norm_matmul",
    )(x, g, w)


def _mem_kv_kernel(x_ref, g_ref, wk_ref, wv_ref, *rest):
    k2_ref, v2_ref, k5_ref, v5_ref = rest[-4:]
    h = _rms(x_ref[...], g_ref[...]).astype(BF16)
    for w_ref, o2_ref, o5_ref in ((wk_ref, k2_ref, k5_ref), (wv_ref, v2_ref, v5_ref)):
        res = _dot(h, w_ref[...])
        o2_ref[...] = res
        for hh in range(X_H):
            o5_ref[0, 0, :, hh, :] = res[:, hh * X_HD:(hh + 1) * X_HD]


def _mem_kv(mem, p, nb, mem_len, depth, carry):
    d = X_H * X_HD
    layer = p["layer"]
    const = lambda b: (0, 0)
    slab = lambda b: (layer, b, 0, 0, 0)
    extra = [] if carry is None else list(carry)
    out5 = jax.ShapeDtypeStruct((depth, nb, mem_len, X_H, X_HD), F32)
    return pl.pallas_call(
        _mem_kv_kernel,
        grid=(nb,),
        in_specs=[pl.BlockSpec((mem_len, D_MODEL), lambda b: (b, 0)),
                  pl.BlockSpec((1, D_MODEL), const),
                  _wspec(layer, (D_MODEL, d), const, resident=True),
                  _wspec(layer, (D_MODEL, d), const, resident=True)]
                 + [pl.BlockSpec(memory_space=pl.ANY)] * len(extra),
        out_specs=[pl.BlockSpec((mem_len, d), lambda b: (b, 0)),
                   pl.BlockSpec((mem_len, d), lambda b: (b, 0)),
                   pl.BlockSpec((1, 1, mem_len, X_H, X_HD), slab),
                   pl.BlockSpec((1, 1, mem_len, X_H, X_HD), slab)],
        out_shape=[jax.ShapeDtypeStruct((nb * mem_len, d), F32),
                   jax.ShapeDtypeStruct((nb * mem_len, d), F32), out5, out5],
        input_output_aliases={4: 2, 5: 3} if extra else {},
        compiler_params=_cparams(1),
        name="mem_kv",
    )(mem, p["norm_mem"], p["w_xk"], p["w_xv"], *extra)


def _inproj_kernel(x_ref, g_ref, w_ref, wdt_ref, z_ref, dt_ref, h_scr):
    @pl.when(pl.program_id(1) == 0)
    def _():
        h = _rms(x_ref[...], g_ref[...]).astype(BF16)
        h_scr[...] = h
        dt_ref[...] = _dot(h, wdt_ref[...])

    z_ref[...] = _dot(h_scr[...], w_ref[...]).astype(z_ref.dtype)


def _in_proj(x, g, w, wdt, layer, tm, tn):
    m, k = x.shape
    n = w.shape[2]
    return pl.pallas_call(
        _inproj_kernel,
        grid=(m // tm, n // tn),
        in_specs=[pl.BlockSpec((tm, k), lambda i, j: (i, 0)),
                  pl.BlockSpec((1, k), lambda i, j: (0, 0)),
                  _wspec(layer, (k, tn), lambda i, j: (0, j)),
                  _wspec(layer, (k, LANES), lambda i, j: (0, 0))],
        out_specs=[pl.BlockSpec((tm, tn), lambda i, j: (i, j)),
                   pl.BlockSpec((tm, LANES), lambda i, j: (i, 0))],
        out_shape=[jax.ShapeDtypeStruct((m, n), BF16),
                   jax.ShapeDtypeStruct((m, LANES), F32)],
        scratch_shapes=[pltpu.VMEM((tm, k), BF16)],
        compiler_params=_cparams(2),
        name="in_proj",
    )(x, g, w, wdt)


def _mm_norm_res_kernel(a_ref, w_ref, g_ref, x_ref, o_ref):
    y = _dot(a_ref[...].astype(BF16), w_ref[...])
    o_ref[...] = x_ref[...] + _rms(y, g_ref[...])


def _mm_norm_res(a, w, layer, g, x):
    m, k = a.shape
    n = w.shape[2]
    return pl.pallas_call(
        _mm_norm_res_kernel,
        grid=(1,),
        in_specs=[pl.BlockSpec((m, k), lambda i: (0, 0)),
                  _wspec(layer, (k, n), lambda i: (0, 0)),
                  pl.BlockSpec((1, n), lambda i: (0, 0)),
                  pl.BlockSpec((m, n), lambda i: (0, 0))],
        out_specs=pl.BlockSpec((m, n), lambda i: (0, 0)),
        out_shape=jax.ShapeDtypeStruct((m, n), F32),
        compiler_params=_cparams(1),
        name="mm_norm_res",
    )(a, w, g, x)


def _ssd_kernel(xbc_ref, zs_ref, dtr_ref, cw_ref, cb_ref, dtb_ref, alog_ref, dx_ref, nrm_ref, e_ref,
                y_ref, hout_ref, cbuf_ref, x_scr, ht_scr):
    c = pl.program_id(1)
    q = xbc_ref.shape[0]
    pad = SUBLANES

    halo = x_scr.shape[0]

    @pl.when(c == 0)
    def _():
        x_scr[...] = jnp.zeros(x_scr.shape, BF16)
        ht_scr[...] = jnp.zeros(ht_scr.shape, F32)

    xb = xbc_ref[...]
    xin = xb.astype(F32)
    ext = jnp.concatenate([x_scr[...], xb], axis=0)
    n_sh = SSM_CONV_W - 1
    ri = lax.broadcasted_iota(jnp.int32, (n_sh * q, halo + q), 0)
    ci = lax.broadcasted_iota(jnp.int32, (n_sh * q, halo + q), 1)
    shift_mat = jnp.where(ci == (ri % q) + halo - n_sh + ri // q, 1.0, 0.0).astype(BF16)
    shifted = _dot(shift_mat, ext)
    cw = cw_ref[...]
    xc = cb_ref[...] + xin * cw[n_sh:n_sh + 1, :]
    for k in range(n_sh):
        xc = xc + shifted[k * q:(k + 1) * q, :] * cw[k:k + 1, :]
    xc = _silu(xc)
    x_scr[...] = xb[q - halo:q, :]
    cbuf_ref[0] = xin[q - pad:q, :]

    xs = xc[:, :D_SSM]
    bs = xc[:, D_SSM:D_SSM + G_SSM * N_SSM]
    cs_in = xc[:, D_SSM + G_SSM * N_SSM:]

    dt = _softplus(dtr_ref[...] + dtb_ref[...])
    a = dt * (-jnp.exp(alog_ref[...]))
    row = lax.broadcasted_iota(jnp.int32, (q, q), 0)
    col = lax.broadcasted_iota(jnp.int32, (q, q), 1)
    tri = row >= col
    cs = _dot_exact_rhs(jnp.where(tri, 1.0, 0.0).astype(BF16), a)
    cs_last = cs[q - 1:q, :]
    cs2 = cs * LOG2E
    src_t = (cs2 - jnp.log2(dt)).T
    wdd = dt * jnp.exp(cs_last - cs)
    cdec = jnp.broadcast_to(jnp.exp(cs_last), (SUBLANES, LANES))
    cdec_x = _dot_exact_lhs(cdec, e_ref[...])[0:1, :]

    lane = lax.broadcasted_iota(jnp.int32, (q, LANES), 1)
    heads_per_group = H_SSM // G_SSM
    y_tiles = []
    for g in range(G_SSM):
        bg = bs[:, g * N_SSM:(g + 1) * N_SSM]
        cg = cs_in[:, g * N_SSM:(g + 1) * N_SSM]
        cb = _dot_nt(cg.astype(BF16), bg.astype(BF16))
        xdd_tiles = []
        for pr in range(heads_per_group // 2):
            hp = g * (heads_per_group // 2) + pr
            xpair = xs[:, hp * LANES:(hp + 1) * LANES]
            wp = jnp.concatenate([xpair, ht_scr[:, hp * LANES:(hp + 1) * LANES]], axis=0).astype(BF16)
            parts = []
            wsel = []
            for r2 in range(2):
                h = 2 * hp + r2
                csb = jnp.broadcast_to(cs2[:, h:h + 1], (q, LANES))
                mmat = cb * jnp.exp2(jnp.where(tri, csb - src_t[h:h + 1, :], -jnp.inf))
                csc = cg * jnp.exp2(csb)
                lhs = jnp.concatenate([mmat, csc], axis=1).astype(BF16)
                parts.append(_dot(lhs, wp))
                wsel.append(jnp.broadcast_to(wdd[:, h:h + 1], (q, LANES)))
            y_tiles.append(jnp.where(lane < SSM_HEADDIM, parts[0], parts[1]))
            xdd_tiles.append(xpair * jnp.where(lane < SSM_HEADDIM, wsel[0], wsel[1]))
        xdd = jnp.concatenate(xdd_tiles, axis=1).astype(BF16)
        st_t = _dot(bg.T.astype(BF16), xdd)
        lo, hi = g * heads_per_group * SSM_HEADDIM, (g + 1) * heads_per_group * SSM_HEADDIM
        ht_scr[:, lo:hi] = ht_scr[:, lo:hi] * cdec_x[:, lo:hi] + st_t

    y = jnp.concatenate(y_tiles, axis=1) + xs * dx_ref[...]
    y = y * _silu(zs_ref[...].astype(F32))
    y_ref[...] = _rms(y, nrm_ref[...]).astype(y_ref.dtype)

    @pl.when(c == pl.num_programs(1) - 1)
    def _():
        hout_ref[0] = ht_scr[...].T


def _ssd_prompt(z, dtr, p, nb, seq):
    q = SSD_CHUNK
    nc = seq // q
    row = lambda b, c: b * nc + c
    const = lambda b, c: (0, 0)
    return pl.pallas_call(
        _ssd_kernel,
        grid=(nb, nc),
        in_specs=[pl.BlockSpec((q, SSM_CONV_DIM), lambda b, c: (row(b, c), OFF_XBC // SSM_CONV_DIM)),
                  pl.BlockSpec((q, D_SSM), lambda b, c: (row(b, c), OFF_ZS // D_SSM)),
                  pl.BlockSpec((q, LANES), lambda b, c: (row(b, c), 0)),
                  pl.BlockSpec((SUBLANES, SSM_CONV_DIM), const),
                  pl.BlockSpec((1, SSM_CONV_DIM), const),
                  pl.BlockSpec((1, LANES), const),
                  pl.BlockSpec((1, LANES), const),
                  pl.BlockSpec((1, D_SSM), const),
                  pl.BlockSpec((1, D_SSM), const),
                  pl.BlockSpec((LANES, D_SSM), const)],
        out_specs=[pl.BlockSpec((q, D_SSM), lambda b, c: (row(b, c), 0)),
                   pl.BlockSpec((1, D_SSM, N_SSM), lambda b, c: (b, 0, 0)),
                   pl.BlockSpec((1, SUBLANES, SSM_CONV_DIM), lambda b, c: (b, 0, 0))],
        out_shape=[jax.ShapeDtypeStruct((nb * seq, D_SSM), BF16),
                   jax.ShapeDtypeStruct((nb, D_SSM, N_SSM), F32),
                   jax.ShapeDtypeStruct((nb, SUBLANES, SSM_CONV_DIM), F32)],
        scratch_shapes=[pltpu.VMEM((2 * SUBLANES, SSM_CONV_DIM), BF16),
                        pltpu.VMEM((N_SSM, D_SSM), F32)],
        compiler_params=_cparams(2),
        name="ssd_prompt",
    )(z, z, dtr, p["ssm_conv_w"], p["ssm_conv_b"], p["dt_bias"], p["a_log"], p["d_x"], p["ssm_norm"],
      p["expand"])


def _swa_kernel(sink_ref, q_ref, k_ref, v_ref, c_ref, s1_ref, s2_ref, o_ref, kn_ref, vn_ref,
                kp_scr, vp_scr):
    qb = pl.program_id(1)
    w = WINDOW
    kvw = KV_ATT * HD_ATT

    @pl.when(qb == 0)
    def _():
        kp_scr[...] = jnp.zeros((w, kvw), F32)
        vp_scr[...] = jnp.zeros((w, kvw), F32)

    c, s1, s2 = c_ref[...], s1_ref[...], s2_ref[...]
    qv = _rope(q_ref[...].astype(F32), c, s1, s2) * (HD_ATT ** -0.5)
    ko = _rope(k_ref[...].astype(F32), c, s1, s2)
    vo = v_ref[...].astype(F32)
    kn_ref[0] = ko
    vn_ref[0] = vo
    kcat = jnp.concatenate([kp_scr[...], ko], axis=0)
    vcat = jnp.concatenate([vp_scr[...], vo], axis=0)
    kp_scr[...] = ko
    vp_scr[...] = vo
    vcat_t = vcat.T.astype(BF16)

    rep = H_ATT // KV_ATT
    keyj = lax.broadcasted_iota(jnp.int32, (2 * w, w), 0)
    qryl = lax.broadcasted_iota(jnp.int32, (2 * w, w), 1)
    valid = (keyj > qryl) & (keyj <= qryl + w) & ((qb > 0) | (keyj >= w))
    valid = jnp.concatenate([valid] * rep, axis=1)
    lane_kv = lax.broadcasted_iota(jnp.int32, (2 * w, LANES), 1)
    lane_q = lax.broadcasted_iota(jnp.int32, (w, LANES), 1)
    lo_kv, lo_q = lane_kv < HD_ATT, lane_q < HD_ATT

    scores, sinks = [], []
    for slab in range(kvw // LANES):
        a_k = kcat[:, slab * LANES:(slab + 1) * LANES]
        b_k = pltpu.roll(a_k, HD_ATT, 1)
        for gi in range(2):
            g = 2 * slab + gi
            kdup = jnp.where(lo_kv, a_k, b_k) if gi == 0 else jnp.where(lo_kv, b_k, a_k)
            q_tiles, sink_tiles = [], []
            for pr in range(rep // 2):
                qp = qv[:, (rep // 2 * g + pr) * LANES:(rep // 2 * g + pr + 1) * LANES]
                q_tiles.append(jnp.where(lo_q, qp, 0.0))
                q_tiles.append(jnp.where(lo_q, 0.0, qp))
                for r2 in range(2):
                    sink_tiles.append(jnp.full((1, w), sink_ref[rep * g + 2 * pr + r2], F32))
            qs = jnp.concatenate(q_tiles, axis=0).astype(BF16)
            sinks.append(jnp.concatenate(sink_tiles, axis=1))
            scores.append(_dot_nt(kdup.astype(BF16), qs))
    probs, dens = [], []
    for g in range(KV_ATT):
        s = jnp.where(valid, scores[g], -jnp.inf)
        m = jnp.maximum(jnp.max(s, axis=0, keepdims=True), sinks[g])
        e = jnp.exp(s - m)
        dens.append(jnp.sum(e, axis=0, keepdims=True) + jnp.exp(sinks[g] - m))
        probs.append(e.astype(BF16))
    o_t = []
    for g in range(KV_ATT):
        o_g = _dot(vcat_t[g * HD_ATT:(g + 1) * HD_ATT, :], probs[g]) * (1.0 / dens[g])
        for r in range(rep):
            o_t.append(o_g[:, r * w:(r + 1) * w])
    o_ref[...] = jnp.concatenate(o_t, axis=0).T.astype(o_ref.dtype)


def _swa_prompt(z, p, nb, seq):
    w = WINDOW
    nq = seq // w
    kvw = KV_ATT * HD_ATT
    row = lambda b, i: b * nq + i
    tab = pl.BlockSpec((w, LANES), lambda b, i: (i, 0))
    return pl.pallas_call(
        _swa_kernel,
        grid=(nb, nq),
        in_specs=[pl.BlockSpec(memory_space=pltpu.SMEM),
                  pl.BlockSpec((w, H_ATT * HD_ATT), lambda b, i: (row(b, i), OFF_Q // (H_ATT * HD_ATT))),
                  pl.BlockSpec((w, kvw), lambda b, i: (row(b, i), OFF_K // kvw)),
                  pl.BlockSpec((w, kvw), lambda b, i: (row(b, i), OFF_V // kvw)),
                  tab, tab, tab],
        out_specs=[pl.BlockSpec((w, H_ATT * HD_ATT), lambda b, i: (row(b, i), 0)),
                   pl.BlockSpec((1, w, kvw), lambda b, i: (b, 0, 0)),
                   pl.BlockSpec((1, w, kvw), lambda b, i: (b, 0, 0))],
        out_shape=[jax.ShapeDtypeStruct((nb * seq, H_ATT * HD_ATT), BF16),
                   jax.ShapeDtypeStruct((nb, w, kvw), F32),
                   jax.ShapeDtypeStruct((nb, w, kvw), F32)],
        scratch_shapes=[pltpu.VMEM((w, kvw), F32), pltpu.VMEM((w, kvw), F32)],
        compiler_params=_cparams(2),
        name="swa_prompt",
    )(p["sinks"], z, z, z, p["rope_c"], p["rope_s1"], p["rope_s2"])


def _tail_body(u, gb, ys, oc, ga, gbg, gcg, x, wa_ref, ws_ref, wc_ref, wo_ref, gpost_ref):
    ta = (gb * u).astype(BF16)
    out = None
    half = D_MODEL // 2
    for lo in range(0, D_MODEL, half):
        hi = lo + half
        y_a = _dot(ta, wa_ref[:, lo:hi])
        y_b = _dot(ys, ws_ref[:, lo:hi])
        y_c = _dot(oc, wc_ref[:, lo:hi])
        merged = (_sigmoid(ga[:, lo:hi]) * y_a + _sigmoid(gbg[:, lo:hi]) * y_b
                  + _sigmoid(gcg[:, lo:hi]) * y_c)
        d = _dot(merged.astype(BF16), wo_ref[lo:hi, :])
        out = d if out is None else out + d
    return x + _rms(out, gpost_ref[...])


def _tail_prompt_kernel(va_ref, gb_ref, gc_ref, ys_ref, oc_ref, ga_ref, gbg_ref, gcg_ref, x_ref,
                        cw_ref, wa_ref, ws_ref, wc_ref, wo_ref, gpost_ref, xo_ref, bufo_ref, cv_scr):
    s = pl.program_id(1)
    tm = x_ref.shape[0]
    pad = SUBLANES

    @pl.when(s == 0)
    def _():
        cv_scr[0:pad, :] = jnp.zeros((pad, D_A), F32)

    cv = gc_ref[...].astype(F32) * va_ref[...].astype(F32)
    cv_scr[pad:pad + tm, :] = cv
    cw = cw_ref[...]
    u = cv * cw[CONV_A_W - 1:CONV_A_W, :]
    for k in range(CONV_A_W - 1):
        off = pad - (CONV_A_W - 1) + k
        u = u + cv_scr[off:off + tm, :] * cw[k:k + 1, :]
    tail = cv_scr[tm:tm + pad, :]
    cv_scr[0:pad, :] = tail
    bufo_ref[0] = tail
    xo_ref[...] = _tail_body(u, gb_ref[...].astype(F32), ys_ref[...], oc_ref[...],
                             ga_ref[...].astype(F32), gbg_ref[...].astype(F32), gcg_ref[...].astype(F32),
                             x_ref[...], wa_ref, ws_ref, wc_ref, wo_ref, gpost_ref)


def _tail_prompt(z, ys, oc, x, p, nb, seq, tm):
    ns = seq // tm
    row = lambda b, s: b * ns + s
    const = lambda b, s: (0, 0)
    zblk = lambda off: pl.BlockSpec((tm, D_MODEL), lambda b, s: (row(b, s), off // D_MODEL))
    return pl.pallas_call(
        _tail_prompt_kernel,
        grid=(nb, ns),
        in_specs=[zblk(OFF_ZA), zblk(OFF_ZA + D_A), zblk(OFF_ZA + 2 * D_A),
                  pl.BlockSpec((tm, D_SSM), lambda b, s: (row(b, s), 0)),
                  pl.BlockSpec((tm, H_ATT * HD_ATT), lambda b, s: (row(b, s), 0)),
                  zblk(OFF_GATES), zblk(OFF_GATES + D_MODEL), zblk(OFF_GATES + 2 * D_MODEL),
                  pl.BlockSpec((tm, D_MODEL), lambda b, s: (row(b, s), 0)),
                  pl.BlockSpec((SUBLANES, D_A), const),
                  _wspec(p["layer"], (D_A, D_MODEL), const, resident=True),
                  _wspec(p["layer"], (D_SSM, D_MODEL), const, resident=True),
                  _wspec(p["layer"], (H_ATT * HD_ATT, D_MODEL), const, resident=True),
                  _wspec(p["layer"], (D_MODEL, D_MODEL), const, resident=True),
                  pl.BlockSpec((1, D_MODEL), const)],
        out_specs=[pl.BlockSpec((tm, D_MODEL), lambda b, s: (row(b, s), 0)),
                   pl.BlockSpec((1, SUBLANES, D_A), lambda b, s: (b, 0, 0))],
        out_shape=[jax.ShapeDtypeStruct((nb * seq, D_MODEL), F32),
                   jax.ShapeDtypeStruct((nb, SUBLANES, D_A), F32)],
        scratch_shapes=[pltpu.VMEM((tm + SUBLANES, D_A), F32)],
        compiler_params=_cparams(2, VMEM_LIMIT_BIG),
        name="tail_prompt",
    )(z, z, z, ys, oc, z, z, z, x, p["conv_a_w"], p["w_a_out"], p["w_ssm_out"], p["w_attn_out"],
      p["w_out"], p["norm_mix_post"])


def _tail_sample_kernel(va_ref, gb_ref, gc_ref, ys_ref, oc_ref, ga_ref, gbg_ref, gcg_ref, x_ref,
                        b0_ref, b1_ref, cw_ref, wa_ref, ws_ref, wc_ref, wo_ref, gpost_ref,
                        xo_ref, cvo_ref):
    cv = gc_ref[...].astype(F32) * va_ref[...].astype(F32)
    cw = cw_ref[...]
    u = b0_ref[...] * cw[0:1, :] + b1_ref[...] * cw[1:2, :] + cv * cw[2:3, :]
    cvo_ref[...] = cv
    xo_ref[...] = _tail_body(u, gb_ref[...].astype(F32), ys_ref[...], oc_ref[...],
                             ga_ref[...].astype(F32), gbg_ref[...].astype(F32), gcg_ref[...].astype(F32),
                             x_ref[...], wa_ref, ws_ref, wc_ref, wo_ref, gpost_ref)


def _tail_sample(z, ys, oc, x, b0, b1, p):
    m = x.shape[0]
    const = lambda i: (0, 0)
    zblk = lambda off: pl.BlockSpec((m, D_MODEL), lambda i: (0, off // D_MODEL))
    full = lambda a: pl.BlockSpec(a.shape, const)
    wfull = lambda a: _wspec(p["layer"], a.shape[1:], const)
    w_c = p["w_attn_out_exp"]
    return pl.pallas_call(
        _tail_sample_kernel,
        grid=(1,),
        in_specs=[zblk(OFF_ZA), zblk(OFF_ZA + D_A), zblk(OFF_ZA + 2 * D_A),
                  full(ys), full(oc),
                  zblk(OFF_GATES), zblk(OFF_GATES + D_MODEL), zblk(OFF_GATES + 2 * D_MODEL),
                  full(x), full(b0), full(b1),
                  full(p["conv_a_w"]), wfull(p["w_a_out"]), wfull(p["w_ssm_out"]), wfull(w_c),
                  wfull(p["w_out"]), full(p["norm_mix_post"])],
        out_specs=[pl.BlockSpec((m, D_MODEL), const), pl.BlockSpec((m, D_A), const)],
        out_shape=[jax.ShapeDtypeStruct((m, D_MODEL), F32), jax.ShapeDtypeStruct((m, D_A), F32)],
        compiler_params=_cparams(1, VMEM_LIMIT_BIG),
        name="tail_sample",
    )(z, z, z, ys, oc, z, z, z, x, b0, b1, p["conv_a_w"], p["w_a_out"], p["w_ssm_out"], w_c,
      p["w_out"], p["norm_mix_post"])


def _xattn_prompt_kernel(x_ref, k_ref, v_ref, gpre_ref, wq_ref, wo_ref, gpost_ref, xo_ref):
    x = x_ref[...]
    h = _rms(x, gpre_ref[...]).astype(BF16)
    qv = (_dot(h, wq_ref[...]) * (X_HD ** -0.5)).astype(BF16)
    tm = x.shape[0]
    s = jnp.concatenate(
        [_dot_nt(qv[:, hh * X_HD:(hh + 1) * X_HD], k_ref[:, hh * X_HD:(hh + 1) * X_HD].astype(BF16))
         for hh in range(X_H)], axis=0)
    m = jnp.max(s, axis=-1, keepdims=True)
    e = jnp.exp(s - m)
    pb = (e * (1.0 / jnp.sum(e, axis=-1, keepdims=True))).astype(BF16)
    o = jnp.concatenate(
        [_dot(pb[hh * tm:(hh + 1) * tm], v_ref[:, hh * X_HD:(hh + 1) * X_HD].astype(BF16))
         for hh in range(X_H)], axis=1).astype(BF16)
    xo_ref[...] = x + _rms(_dot(o, wo_ref[...]), gpost_ref[...])


def _xattn_prompt(x, mk, mv, p, nb, seq, mem_len, tm):
    ns = seq // tm
    const = lambda b, s: (0, 0)
    return pl.pallas_call(
        _xattn_prompt_kernel,
        grid=(nb, ns),
        in_specs=[pl.BlockSpec((tm, D_MODEL), lambda b, s: (b * ns + s, 0)),
                  pl.BlockSpec((mem_len, X_H * X_HD), lambda b, s: (b, 0)),
                  pl.BlockSpec((mem_len, X_H * X_HD), lambda b, s: (b, 0)),
                  pl.BlockSpec((1, D_MODEL), const),
                  _wspec(p["layer"], (D_MODEL, X_H * X_HD), const, resident=True),
                  _wspec(p["layer"], (X_H * X_HD, D_MODEL), const, resident=True),
                  pl.BlockSpec((1, D_MODEL), const)],
        out_specs=pl.BlockSpec((tm, D_MODEL), lambda b, s: (b * ns + s, 0)),
        out_shape=jax.ShapeDtypeStruct((nb * seq, D_MODEL), F32),
        compiler_params=_cparams(2),
        name="xattn_prompt",
    )(x, mk, mv, p["norm_x_pre"], p["w_xq"], p["w_xo"], p["norm_x_post"])


def _ffn_finish(kf, part, x_ref, gpost_ref, xo_ref, acc_scr):
    @pl.when(kf == 0)
    def _():
        acc_scr[...] = part

    @pl.when(kf > 0)
    def _():
        acc_scr[...] = acc_scr[...] + part

    @pl.when(kf == pl.num_programs(2) - 1)
    def _():
        xo_ref[...] = x_ref[...] + _rms(acc_scr[...], gpost_ref[...])


def _ffn_prompt_kernel(x_ref, gpre_ref, win_ref, cw_ref, cb_ref, wo_ref, gpost_ref,
                       xo_ref, bufo_ref, a_scr):
    s = pl.program_id(1)
    tm = x_ref.shape[0]
    pad = SUBLANES

    @pl.when(s == 0)
    def _():
        a_scr[0:pad, :] = jnp.zeros((pad, D_FF), F32)

    x = x_ref[...]
    h = _rms(x, gpre_ref[...]).astype(BF16)
    cw = cw_ref[...]
    cb = cb_ref[...]

    def up(lo, hi):
        return _dot(h, win_ref[:, lo:hi]), _dot(h, win_ref[:, D_FF + lo:D_FF + hi])

    bounds = [(lo, min(lo + FFN_SUB, D_FF)) for lo in range(0, D_FF, FFN_SUB)]
    out = None
    nxt = up(*bounds[0])
    for i, (lo, hi) in enumerate(bounds):
        a, gate = nxt
        if i + 1 < len(bounds):
            nxt = up(*bounds[i + 1])
        a_scr[pad:pad + tm, lo:hi] = a
        ac = cb[:, lo:hi] + a * cw[FFN_CONV_W - 1:FFN_CONV_W, lo:hi]
        for k in range(FFN_CONV_W - 1):
            off = pad - (FFN_CONV_W - 1) + k
            ac = ac + a_scr[off:off + tm, lo:hi] * cw[k:k + 1, lo:hi]
        d = _dot((_silu(ac) * gate).astype(BF16), wo_ref[lo:hi, :])
        out = d if out is None else out + d
    tail = a_scr[tm:tm + pad, :]
    a_scr[0:pad, :] = tail
    bufo_ref[0] = tail
    xo_ref[...] = x + _rms(out, gpost_ref[...])


def _ffn_prompt(x, p, nb, seq, tm):
    ns = seq // tm
    const = lambda b, s: (0, 0)
    return pl.pallas_call(
        _ffn_prompt_kernel,
        grid=(nb, ns),
        in_specs=[pl.BlockSpec((tm, D_MODEL), lambda b, s: (b * ns + s, 0)),
                  pl.BlockSpec((1, D_MODEL), const),
                  _wspec(p["layer"], (D_MODEL, 2 * D_FF), const, resident=True),
                  pl.BlockSpec((SUBLANES, D_FF), const),
                  pl.BlockSpec((1, D_FF), const),
                  _wspec(p["layer"], (D_FF, D_MODEL), const, resident=True),
                  pl.BlockSpec((1, D_MODEL), const)],
        out_specs=[pl.BlockSpec((tm, D_MODEL), lambda b, s: (b * ns + s, 0)),
                   pl.BlockSpec((1, SUBLANES, D_FF), lambda b, s: (b, 0, 0))],
        out_shape=[jax.ShapeDtypeStruct((nb * seq, D_MODEL), F32),
                   jax.ShapeDtypeStruct((nb, SUBLANES, D_FF), F32)],
        scratch_shapes=[pltpu.VMEM((tm + SUBLANES, D_FF), F32)],
        compiler_params=_cparams(2, VMEM_LIMIT_BIG),
        name="ffn_prompt",
    )(x, p["norm_ffn_pre"], p["w_ffn_in"], p["ffn_conv_w"], p["ffn_conv_b"],
      p["w_ffn_out"], p["norm_ffn_post"])


def _ffn_sample_kernel(x_ref, gpre_ref, wa_ref, wg_ref, cw_ref, cb_ref, b0_ref, b1_ref, wo_ref,
                       gpost_ref, xo_ref, ao_ref, h_scr, acc_scr):
    kf = pl.program_id(2)

    @pl.when(kf == 0)
    def _():
        h_scr[...] = _rms(x_ref[...], gpre_ref[...]).astype(BF16)

    h = h_scr[...]
    a = _dot(h, wa_ref[...])
    gate = _dot(h, wg_ref[...])
    ao_ref[...] = a
    cw = cw_ref[...]
    ac = cb_ref[...] + b0_ref[...] * cw[0:1, :] + b1_ref[...] * cw[1:2, :] + a * cw[2:3, :]
    part = _dot((_silu(ac) * gate).astype(BF16), wo_ref[...])
    _ffn_finish(kf, part, x_ref, gpost_ref, xo_ref, acc_scr)


def _ffn_sample(x, b0, b1, p):
    m = x.shape[0]
    nk = D_FF // FF_CHUNK
    const = lambda b, s, k: (0, 0)
    chunk = lambda b, s, k: (0, k)
    return pl.pallas_call(
        _ffn_sample_kernel,
        grid=(1, 1, nk),
        in_specs=[pl.BlockSpec((m, D_MODEL), const),
                  pl.BlockSpec((1, D_MODEL), const),
                  _wspec(p["layer"], (D_MODEL, FF_CHUNK), chunk),
                  _wspec(p["layer"], (D_MODEL, FF_CHUNK), lambda b, s, k: (0, nk + k)),
                  pl.BlockSpec((SUBLANES, FF_CHUNK), chunk),
                  pl.BlockSpec((1, FF_CHUNK), chunk),
                  pl.BlockSpec((m, FF_CHUNK), chunk),
                  pl.BlockSpec((m, FF_CHUNK), chunk),
                  _wspec(p["layer"], (FF_CHUNK, D_MODEL), lambda b, s, k: (k, 0)),
                  pl.BlockSpec((1, D_MODEL), const)],
        out_specs=[pl.BlockSpec((m, D_MODEL), const),
                   pl.BlockSpec((m, FF_CHUNK), chunk)],
        out_shape=[jax.ShapeDtypeStruct((m, D_MODEL), F32),
                   jax.ShapeDtypeStruct((m, D_FF), F32)],
        scratch_shapes=[pltpu.VMEM((m, D_MODEL), BF16),
                        pltpu.VMEM((m, D_MODEL), F32)],
        compiler_params=_cparams(3),
        name="ffn_sample",
    )(x, p["norm_ffn_pre"], p["w_ffn_in"], p["w_ffn_in"], p["ffn_conv_w"], p["ffn_conv_b"], b0, b1,
      p["w_ffn_out"], p["norm_ffn_post"])


def _dec_prep_kernel(xbc_ref, dtr_ref, b0_ref, b1_ref, b2_ref, cw_ref, cb_ref, dtb_ref, alog_ref,
                     dx_ref, e_ref, xdt_ref, dec_ref, bs_ref, cs_ref, xsd_ref):
    cw = cw_ref[...]
    xc = (cb_ref[...] + b0_ref[...] * cw[0:1, :] + b1_ref[...] * cw[1:2, :] + b2_ref[...] * cw[2:3, :]
          + xbc_ref[...].astype(F32) * cw[3:4, :])
    xc = _silu(xc)
    xs = xc[:, :D_SSM]
    bs_ref[...] = xc[:, D_SSM:D_SSM + G_SSM * N_SSM]
    cs_ref[...] = xc[:, D_SSM + G_SSM * N_SSM:]
    dt = _softplus(dtr_ref[...] + dtb_ref[...])
    dec_ref[...] = jnp.exp(dt * (-jnp.exp(alog_ref[...])))
    xdt_ref[...] = xs * _dot_exact_lhs(dt, e_ref[...])
    xsd_ref[...] = xs * dx_ref[...]


def _dec_prep(z, dtr, b0, b1, b2, p):
    m = dtr.shape[0]
    const = lambda i: (0, 0)
    full = lambda a: pl.BlockSpec(a.shape, const)
    return pl.pallas_call(
        _dec_prep_kernel,
        grid=(1,),
        in_specs=[pl.BlockSpec((m, SSM_CONV_DIM), lambda i: (0, OFF_XBC // SSM_CONV_DIM)),
                  full(dtr), full(b0), full(b1), full(b2), full(p["ssm_conv_w"]), full(p["ssm_conv_b"]),
                  full(p["dt_bias"]), full(p["a_log"]), full(p["d_x"]), full(p["expand"])],
        out_specs=[pl.BlockSpec((m, D_SSM), const), pl.BlockSpec((m, LANES), const),
                   pl.BlockSpec((m, G_SSM * N_SSM), const), pl.BlockSpec((m, G_SSM * N_SSM), const),
                   pl.BlockSpec((m, D_SSM), const)],
        out_shape=[jax.ShapeDtypeStruct((m, D_SSM), F32), jax.ShapeDtypeStruct((m, LANES), F32),
                   jax.ShapeDtypeStruct((m, G_SSM * N_SSM), F32),
                   jax.ShapeDtypeStruct((m, G_SSM * N_SSM), F32),
                   jax.ShapeDtypeStruct((m, D_SSM), F32)],
        compiler_params=_cparams(1),
        name="dec_prep",
    )(z, dtr, b0, b1, b2, p["ssm_conv_w"], p["ssm_conv_b"], p["dt_bias"], p["a_log"], p["d_x"],
      p["expand"])


def _dec_state_kernel(dec_ref, st_ref, xdt_ref, bs_ref, cs_ref, *rest):
    so_ref, y_ref = rest[-2:]
    bt = st_ref.shape[0]
    rows_per_group = D_SSM // G_SSM
    grp = lax.broadcasted_iota(jnp.int32, (G_SSM, D_SSM), 0)
    own = (lax.broadcasted_iota(jnp.int32, (G_SSM, D_SSM), 1) // rows_per_group) == grp
    for i in range(bt):
        b = pl.program_id(0) * bt + i
        x = xdt_ref[pl.ds(b, 1), :]
        x8 = jnp.where(own, jnp.broadcast_to(x, own.shape), 0.0).astype(BF16)
        outer = lax.dot_general(x8, bs_ref[b].astype(BF16), (((0,), (0,)), ((), ())),
                                preferred_element_type=F32)
        for h in range(H_SSM):
            lo, hi = h * SSM_HEADDIM, (h + 1) * SSM_HEADDIM
            so_ref[i, lo:hi, :] = st_ref[i, lo:hi, :] * dec_ref[b, h] + outer[lo:hi, :]
        y8 = _dot_nt(cs_ref[b].astype(BF16), so_ref[i].astype(BF16))
        y_ref[pl.ds(b, 1), :] = jnp.sum(jnp.where(own, y8, 0.0), axis=0, keepdims=True)


def _dec_state(state_all, layer, dec, xdt, bs3, cs3, carry):
    nb = xdt.shape[0]
    bt = _pick_tile(nb, 4)
    const2 = lambda b: (0, 0)
    const3 = lambda b: (0, 0, 0)
    slab = lambda b: (layer * (nb // bt) + b, 0, 0)
    extra = [] if carry is None else [carry]
    return pl.pallas_call(
        _dec_state_kernel,
        grid=(nb // bt,),
        in_specs=[pl.BlockSpec(memory_space=pltpu.SMEM),
                  pl.BlockSpec((bt, D_SSM, N_SSM), slab),
                  pl.BlockSpec(xdt.shape, const2),
                  pl.BlockSpec(bs3.shape, const3), pl.BlockSpec(cs3.shape, const3)]
                 + [pl.BlockSpec(memory_space=pl.ANY)] * len(extra),
        out_specs=[pl.BlockSpec((bt, D_SSM, N_SSM), slab),
                   pl.BlockSpec((nb, D_SSM), const2)],
        out_shape=[jax.ShapeDtypeStruct(state_all.shape, F32),
                   jax.ShapeDtypeStruct((nb, D_SSM), F32)],
        input_output_aliases={5: 0} if extra else {},
        compiler_params=_cparams(1),
        name="dec_state",
    )(dec, state_all, xdt, bs3, cs3, *extra)


def _dec_post_kernel(y_ref, xsd_ref, zs_ref, nrm_ref, o_ref):
    y = y_ref[...] + xsd_ref[...]
    y = y * _silu(zs_ref[...].astype(F32))
    o_ref[...] = _rms(y, nrm_ref[...]).astype(o_ref.dtype)


def _dec_post(yt, xsd, z, p):
    m = xsd.shape[0]
    const = lambda i: (0, 0)
    return pl.pallas_call(
        _dec_post_kernel,
        grid=(1,),
        in_specs=[pl.BlockSpec(yt.shape, const), pl.BlockSpec(xsd.shape, const),
                  pl.BlockSpec((m, D_SSM), lambda i: (0, OFF_ZS // D_SSM)),
                  pl.BlockSpec((1, D_SSM), const)],
        out_specs=pl.BlockSpec((m, D_SSM), const),
        out_shape=jax.ShapeDtypeStruct((m, D_SSM), BF16),
        compiler_params=_cparams(1),
        name="dec_post",
    )(yt, xsd, z, p["ssm_norm"])


def _dec_swa_kernel(qe_ref, kn_ref, vn_ref, ck_ref, cv_ref, c_ref, s1_ref, s2_ref, sink_ref, *rest):
    nk_ref, nv_ref, oe_ref = rest[-3:]
    bt = ck_ref.shape[0]
    w = ck_ref.shape[1]
    c, s1, s2 = c_ref[0:1, :], s1_ref[0:1, :], s2_ref[0:1, :]
    kn = _rope(kn_ref[...].astype(F32), c, s1, s2)
    vn = vn_ref[...].astype(F32)
    last = lax.broadcasted_iota(jnp.int32, (w, KV_ATT * HD_ATT), 0) == w - 1
    nh = qe_ref.shape[1]
    qe = (_rope(qe_ref[...].astype(F32).reshape(bt * nh, KV_ATT * HD_ATT), c, s1, s2)
          * (HD_ATT ** -0.5)).astype(BF16)
    nvs, s_tiles = [], []
    for i in range(bt):
        nk = jnp.where(last, kn[i:i + 1, :], pltpu.roll(ck_ref[i], w - 1, 0))
        nv = jnp.where(last, vn[i:i + 1, :], pltpu.roll(cv_ref[i], w - 1, 0))
        nk_ref[i] = nk
        nv_ref[i] = nv
        nvs.append(nv.astype(BF16))
        s_tiles.append(_dot_nt(qe[i * nh:(i + 1) * nh], nk.astype(BF16)))
    s = jnp.concatenate(s_tiles, axis=0)
    sink = jnp.concatenate([sink_ref[:, 0:1]] * bt, axis=0)
    m = jnp.maximum(jnp.max(s, axis=-1, keepdims=True), sink)
    e = jnp.exp(s - m)
    den = jnp.sum(e, axis=-1, keepdims=True) + jnp.exp(sink - m)
    pb = (e * (1.0 / den)).astype(BF16)
    for i in range(bt):
        oe_ref[i] = _dot(pb[i * nh:(i + 1) * nh], nvs[i]).astype(oe_ref.dtype)


def _dec_swa(qe, z, ck_all, cv_all, layer, p, bt, carry):
    nb = qe.shape[0]
    _, w, kvw = ck_all.shape
    const = lambda i: (0, 0)
    slab = lambda i: (layer * (nb // bt) + i, 0, 0)
    tab = pl.BlockSpec((SUBLANES, LANES), const)
    extra = [] if carry is None else list(carry)
    return pl.pallas_call(
        _dec_swa_kernel,
        grid=(nb // bt,),
        in_specs=[pl.BlockSpec((bt, H_ATT, kvw), lambda i: (i, 0, 0)),
                  pl.BlockSpec((bt, kvw), lambda i: (i, OFF_K // kvw)),
                  pl.BlockSpec((bt, kvw), lambda i: (i, OFF_V // kvw)),
                  pl.BlockSpec((bt, w, kvw), slab),
                  pl.BlockSpec((bt, w, kvw), slab),
                  tab, tab, tab,
                  pl.BlockSpec((H_ATT, LANES), const)]
                 + [pl.BlockSpec(memory_space=pl.ANY)] * len(extra),
        out_specs=[pl.BlockSpec((bt, w, kvw), slab),
                   pl.BlockSpec((bt, w, kvw), slab),
                   pl.BlockSpec((bt, H_ATT, kvw), lambda i: (i, 0, 0))],
        out_shape=[jax.ShapeDtypeStruct(ck_all.shape, F32), jax.ShapeDtypeStruct(cv_all.shape, F32),
                   jax.ShapeDtypeStruct((nb, H_ATT, kvw), BF16)],
        input_output_aliases={9: 0, 10: 1} if extra else {},
        compiler_params=_cparams(1),
        name="dec_swa",
    )(qe, z, z, ck_all, cv_all, p["rope_c_s"], p["rope_s1_s"], p["rope_s2_s"], p["sinks_x"], *extra)


def _dec_xattn_kernel(q_ref, k_ref, v_ref, o_ref):
    bt, mem_len = k_ref.shape[1], k_ref.shape[2]
    rows = mem_len * X_H
    col_head = lax.broadcasted_iota(jnp.int32, (SUBLANES, rows), 1) % X_H
    own = col_head == lax.broadcasted_iota(jnp.int32, (SUBLANES, rows), 0) % X_H
    for i in range(bt):
        qh = q_ref[i] * (X_HD ** -0.5)
        q8 = jnp.concatenate([qh] * (SUBLANES // X_H), axis=0).astype(BF16)
        k2 = k_ref[0, i].reshape(rows, X_HD).astype(BF16)
        v2 = v_ref[0, i].reshape(rows, X_HD).astype(BF16)
        s = _dot_nt(q8, k2)
        s = jnp.where(own, s, -jnp.inf)
        m = jnp.max(s, axis=-1, keepdims=True)
        e = jnp.exp(s - m)
        den = jnp.sum(e, axis=-1, keepdims=True)
        o8 = _dot(e.astype(BF16), v2) * (1.0 / den)
        o_ref[i] = o8[0:X_H, :]


def _dec_xattn(qv, mem_k, mem_v, layer, bt):
    nb = qv.shape[0]
    _, _, mem_len, nh, hd = mem_k.shape
    slab = lambda i: (layer, i, 0, 0, 0)
    return pl.pallas_call(
        _dec_xattn_kernel,
        grid=(nb // bt,),
        in_specs=[pl.BlockSpec((bt, nh, hd), lambda i: (i, 0, 0)),
                  pl.BlockSpec((1, bt, mem_len, nh, hd), slab),
                  pl.BlockSpec((1, bt, mem_len, nh, hd), slab)],
        out_specs=pl.BlockSpec((bt, nh, hd), lambda i: (i, 0, 0)),
        out_shape=jax.ShapeDtypeStruct((nb, nh, hd), F32),
        compiler_params=_cparams(1),
        name="dec_xattn",
    )(qv, mem_k, mem_v)


def _pad_rows(w, rows=SUBLANES):
    return jnp.pad(w, ((0, rows - w.shape[0]), (0, 0)))


def _pad_lanes(v, lanes=LANES):
    return jnp.pad(v, (0, lanes - v.shape[0])).reshape(1, lanes)


def _rope_tables(pos):
    half = ROT_DIM // 2
    inv = ROPE_THETA ** (-jnp.arange(half, dtype=F32) / half)
    ang = pos.astype(F32)[:, None] * inv[None, :]
    cos, sin = jnp.cos(ang), jnp.sin(ang)
    n = pos.shape[0]
    ones = jnp.ones((n, HD_ATT - ROT_DIM), F32)
    zeros = jnp.zeros((n, HD_ATT - ROT_DIM), F32)
    zh = jnp.zeros((n, half), F32)
    c = jnp.concatenate([cos, cos, ones], axis=1)
    s1 = jnp.concatenate([-sin, zh, zeros], axis=1)
    s2 = jnp.concatenate([zh, sin, zeros], axis=1)
    rep = LANES // HD_ATT
    return tuple(jnp.tile(t, (1, rep)) for t in (c, s1, s2))


_BF16_WEIGHTS = ("w_a_out", "w_ssm_out", "w_attn_out", "w_out", "w_xq", "w_xk", "w_xv", "w_xo",
                 "w_ffn_in", "w_ffn_out")


def _prep_weights(prm):
    w_in = prm["w_in"]
    depth = w_in.shape[0]
    o = np.cumsum([0, D_A, D_A, D_A, D_SSM, SSM_CONV_DIM, H_SSM, H_ATT * HD_ATT, KV_ATT * HD_ATT,
                   KV_ATT * HD_ATT, 3 * D_MODEL])
    col = lambda i: w_in[:, :, o[i]:o[i + 1]]
    w = {name: prm[name].astype(BF16) for name in _BF16_WEIGHTS}
    w["w_main"] = jnp.concatenate(
        [col(4), col(3), col(0), col(1), col(2), col(9), col(6), col(7), col(8)], axis=-1).astype(BF16)
    w["w_dt"] = jnp.pad(col(5), ((0, 0), (0, 0), (0, LANES - H_SSM))).astype(BF16)
    rep = H_ATT // KV_ATT
    wq, wc = col(6).astype(BF16), w["w_attn_out"]
    q_blocks, c_blocks = [], []
    for h in range(H_ATT):
        g = h // rep
        q_blocks += [jnp.zeros((depth, D_MODEL, g * HD_ATT), BF16),
                     wq[:, :, h * HD_ATT:(h + 1) * HD_ATT],
                     jnp.zeros((depth, D_MODEL, (KV_ATT - 1 - g) * HD_ATT), BF16)]
        c_blocks += [jnp.zeros((depth, g * HD_ATT, D_MODEL), BF16),
                     wc[:, h * HD_ATT:(h + 1) * HD_ATT, :],
                     jnp.zeros((depth, (KV_ATT - 1 - g) * HD_ATT, D_MODEL), BF16)]
    w["wq_exp"] = jnp.concatenate(q_blocks, axis=2)
    w["w_attn_out_exp"] = jnp.concatenate(c_blocks, axis=1)
    return w


def _prep_layer(l, prm, w, seq):
    expand = jnp.asarray(np.kron(np.eye(LANES, H_SSM, dtype=np.float32),
                                 np.ones((1, SSM_HEADDIM), np.float32))).astype(BF16)
    rope_p = _rope_tables(jnp.arange(seq, dtype=jnp.int32))
    rope_s = _rope_tables(jnp.full((SUBLANES,), PAST_LEN, jnp.int32))
    return {
        **w,
        "layer": l,
        "norm_mix_pre": prm["norm_mix_pre"][l].reshape(1, -1),
        "norm_mix_post": prm["norm_mix_post"][l].reshape(1, -1),
        "conv_a_w": _pad_rows(prm["conv_a_w"][l]),
        "ssm_conv_w": _pad_rows(prm["ssm_conv_w"][l]),
        "ssm_conv_b": prm["ssm_conv_b"][l].reshape(1, -1),
        "dt_bias": _pad_lanes(prm["ssm_dt_bias"][l]),
        "a_log": _pad_lanes(prm["ssm_a_log"][l]),
        "d_x": jnp.repeat(prm["ssm_d"][l], SSM_HEADDIM).reshape(1, -1),
        "ssm_norm": prm["ssm_norm"][l].reshape(1, -1),
        "expand": expand,
        "sinks": prm["attn_sinks"][l],
        "sinks_x": jnp.broadcast_to(prm["attn_sinks"][l][:, None], (H_ATT, LANES)),
        "rope_c": rope_p[0], "rope_s1": rope_p[1], "rope_s2": rope_p[2],
        "rope_c_s": rope_s[0], "rope_s1_s": rope_s[1], "rope_s2_s": rope_s[2],
        "norm_x_pre": prm["norm_x_pre"][l].reshape(1, -1),
        "norm_x_post": prm["norm_x_post"][l].reshape(1, -1),
        "norm_mem": prm["norm_mem"][l].reshape(1, -1),
        "norm_ffn_pre": prm["norm_ffn_pre"][l].reshape(1, -1),
        "norm_ffn_post": prm["norm_ffn_post"][l].reshape(1, -1),
        "ffn_conv_w": _pad_rows(prm["ffn_conv_w"][l]),
        "ffn_conv_b": prm["ffn_conv_b"][l].reshape(1, -1),
    }


def _pick_tile(n, pref):
    t = min(n, pref)
    while n % t:
        t //= 2
    return t


def _prompt_layer(x, mem, p, nb, seq, mem_len, depth, mem_carry):
    rows = nb * seq
    z, dtr = _in_proj(x, p["norm_mix_pre"], p["w_main"], p["w_dt"], p["layer"], _pick_tile(rows, 1024),
                      2304)
    ys, h_last, cbuf = _ssd_prompt(z, dtr, p, nb, seq)
    oc, k_new, v_new = _swa_prompt(z, p, nb, seq)
    x, bufa = _tail_prompt(z, ys, oc, x, p, nb, seq, _pick_tile(seq, 512))
    mk, mv, mk_all, mv_all = _mem_kv(mem, p, nb, mem_len, depth, mem_carry)
    x = _xattn_prompt(x, mk, mv, p, nb, seq, mem_len, _pick_tile(seq, 512))
    x, buff = _ffn_prompt(x, p, nb, seq, _pick_tile(seq, 512))
    state = (bufa[:, SUBLANES - (CONV_A_W - 1):],
             cbuf[:, SUBLANES - (SSM_CONV_W - 1):],
             h_last.reshape(nb, H_SSM, SSM_HEADDIM, N_SSM),
             k_new.reshape(nb, WINDOW, KV_ATT, HD_ATT),
             v_new.reshape(nb, WINDOW, KV_ATT, HD_ATT),
             buff[:, SUBLANES - (FFN_CONV_W - 1):])
    return x, state, (mk_all, mv_all)


def _sample_layer(x, layer, st, big, carry, p):
    buf_a, buf_ssm, buf_ffn = st
    ssm_all, swa_k_all, swa_v_all, mem_k_all, mem_v_all = big
    nb = x.shape[0]
    kvw = KV_ATT * HD_ATT
    z, dtr = _in_proj(x, p["norm_mix_pre"], p["w_main"], p["w_dt"], layer, nb, 1536)
    qe = _norm_matmul(x, p["norm_mix_pre"], p["wq_exp"], layer, BF16, nb, 1024).reshape(nb, H_ATT, kvw)
    xdt, dec, bs, cs, xsd = _dec_prep(z, dtr, buf_ssm[:, 0], buf_ssm[:, 1], buf_ssm[:, 2], p)
    new_ssm, y_ssd = _dec_state(ssm_all, layer, dec[:, :H_SSM], xdt, bs.reshape(nb, G_SSM, N_SSM),
                                cs.reshape(nb, G_SSM, N_SSM), None if carry is None else carry[0])
    ys = _dec_post(y_ssd, xsd, z, p)
    new_k, new_v, oe = _dec_swa(qe, z, swa_k_all, swa_v_all, layer, p, _pick_tile(nb, 16),
                                None if carry is None else carry[1:3])
    x, cv = _tail_sample(z, ys, oe.reshape(nb, H_ATT * kvw), x, buf_a[:, 0], buf_a[:, 1], p)
    qx = _norm_matmul(x, p["norm_x_pre"], p["w_xq"], layer, F32, nb, 1024)
    ox = _dec_xattn(qx.reshape(nb, X_H, X_HD), mem_k_all, mem_v_all, layer, _pick_tile(nb, 4))
    x = _mm_norm_res(ox.reshape(nb, X_H * X_HD), p["w_xo"], layer, p["norm_x_post"], x)
    x, a_up = _ffn_sample(x, buf_ffn[:, 0], buf_ffn[:, 1], p)
    x_raw = z[:, OFF_XBC:OFF_XBC + SSM_CONV_DIM].astype(F32)
    small = (jnp.stack([buf_a[:, 1], cv], axis=1),
             jnp.concatenate([buf_ssm[:, 1:], x_raw[:, None, :]], axis=1),
             jnp.stack([buf_ffn[:, 1], a_up], axis=1))
    return x, small, (new_ssm, new_k, new_v)


def kernel(x_prompt, x_sample, mem_prompt, state_conv_a, state_ssm_conv, state_ssm, cache_swa_k, cache_swa_v, cache_mem_k, cache_mem_v, state_ffn_conv, norm_mix_pre, norm_mix_post, w_in, conv_a_w, w_a_out, ssm_conv_w, ssm_conv_b, ssm_dt_bias, ssm_a_log, ssm_d, ssm_norm, w_ssm_out, attn_sinks, w_attn_out, w_out, norm_x_pre, norm_x_post, norm_mem, w_xq, w_xk, w_xv, w_xo, norm_ffn_pre, norm_ffn_post, w_ffn_in, ffn_conv_w, ffn_conv_b, w_ffn_out):
    prm = dict(norm_mix_pre=norm_mix_pre, norm_mix_post=norm_mix_post, w_in=w_in, conv_a_w=conv_a_w,
               w_a_out=w_a_out, ssm_conv_w=ssm_conv_w, ssm_conv_b=ssm_conv_b, ssm_dt_bias=ssm_dt_bias,
               ssm_a_log=ssm_a_log, ssm_d=ssm_d, ssm_norm=ssm_norm, w_ssm_out=w_ssm_out,
               attn_sinks=attn_sinks, w_attn_out=w_attn_out, w_out=w_out, norm_x_pre=norm_x_pre,
               norm_x_post=norm_x_post, norm_mem=norm_mem, w_xq=w_xq, w_xk=w_xk, w_xv=w_xv, w_xo=w_xo,
               norm_ffn_pre=norm_ffn_pre, norm_ffn_post=norm_ffn_post, w_ffn_in=w_ffn_in,
               ffn_conv_w=ffn_conv_w, ffn_conv_b=ffn_conv_b, w_ffn_out=w_ffn_out)
    nb, seq, d = x_prompt.shape
    ns = x_sample.shape[0]
    mem_len = mem_prompt.shape[1]
    depth = w_in.shape[0]
    assert x_sample.shape[1] == 1 and seq % WINDOW == 0 and seq % SSD_CHUNK == 0
    assert cache_swa_k.shape[2] == WINDOW and PAST_LEN >= WINDOW

    yp = x_prompt.reshape(nb * seq, d)
    ys = x_sample.reshape(ns, d)
    mem = mem_prompt.reshape(nb * mem_len, d)
    flat = lambda a, *tail: a.reshape((depth * ns,) + tail)
    kvw = KV_ATT * HD_ATT
    big = (flat(state_ssm, D_SSM, N_SSM),
           flat(cache_swa_k, cache_swa_k.shape[2], kvw), flat(cache_swa_v, cache_swa_v.shape[2], kvw),
           cache_mem_k, cache_mem_v)
    w = _prep_weights(prm)
    new_p, new_s, carry, mem_carry = [], [], None, None
    for l in range(depth):
        p = _prep_layer(l, prm, w, seq)
        yp, st_p, mem_carry = _prompt_layer(yp, mem, p, nb, seq, mem_len, depth, mem_carry)
        new_p.append(st_p)
        ys, st_s, carry = _sample_layer(ys, l, (state_conv_a[l], state_ssm_conv[l], state_ffn_conv[l]),
                                        big, carry, p)
        new_s.append(st_s)
    stack = lambda lst, i: jnp.stack([s[i] for s in lst])
    s_ssm, s_swa_k, s_swa_v = (carry[0].reshape(state_ssm.shape), carry[1].reshape(cache_swa_k.shape),
                               carry[2].reshape(cache_swa_v.shape))
    return ((yp.reshape(nb, seq, d), ys.reshape(ns, 1, d))
            + tuple(stack(new_p, i) for i in range(6)) + tuple(mem_carry)
            + (stack(new_s, 0), stack(new_s, 1), s_ssm, s_swa_k, s_swa_v, stack(new_s, 2)))
```

```python
import functools

import numpy as np
import jax
import jax.numpy as jnp
from jax import lax
from jax.experimental import pallas as pl
from jax.experimental.pallas import tpu as pltpu

F32 = jnp.float32
BF16 = jnp.bfloat16

EPS = 1e-6
D_MODEL = 1024
D_A = D_MODEL
CONV_A_W = 3
D_SSM = 2 * D_MODEL
SSM_HEADDIM = 64
H_SSM = D_SSM // SSM_HEADDIM
G_SSM = 8
N_SSM = 128
SSM_CONV_W = 4
SSM_CONV_DIM = D_SSM + 2 * G_SSM * N_SSM
SSD_CHUNK = 128
H_ATT = 16
KV_ATT = 4
HD_ATT = 64
ROT_DIM = HD_ATT // 4
ROPE_THETA = 500000.0
WINDOW = 128
X_H = 4
X_HD = D_MODEL // X_H
D_FF = 2816
FFN_CONV_W = 3
PAST_LEN = 8192
LOG2E = 1.4426950408889634

LANES = 128
SUBLANES = 8
VMEM_LIMIT = 48 * 1024 * 1024
VMEM_LIMIT_BIG = 56 * 1024 * 1024

OFF_XBC = 0
OFF_ZS = OFF_XBC + SSM_CONV_DIM
OFF_ZA = OFF_ZS + D_SSM
OFF_GATES = OFF_ZA + 3 * D_A
OFF_Q = OFF_GATES + 3 * D_MODEL
OFF_K = OFF_Q + H_ATT * HD_ATT
OFF_V = OFF_K + KV_ATT * HD_ATT
N_MAIN = OFF_V + KV_ATT * HD_ATT
FF_CHUNK = D_FF // 2
FFN_SUB = 4 * LANES


def _cparams(n_axes, vmem=VMEM_LIMIT):
    return pltpu.CompilerParams(dimension_semantics=("arbitrary",) * n_axes,
                                vmem_limit_bytes=vmem)


def _resident(shape, index_map):
    return pl.BlockSpec(shape, index_map, pipeline_mode=pl.Buffered(1))


def _wspec(layer, shape, index_map, resident=False):
    imap = lambda *g: (layer,) + tuple(index_map(*g))
    if resident:
        return pl.BlockSpec((None,) + tuple(shape), imap, pipeline_mode=pl.Buffered(1))
    return pl.BlockSpec((None,) + tuple(shape), imap)


def _rms(x, g):
    return x * lax.rsqrt(jnp.mean(x * x, axis=-1, keepdims=True) + EPS) * g


def _sigmoid(x):
    return 0.5 + 0.5 * jnp.tanh(0.5 * x)


def _silu(x):
    hx = 0.5 * x
    return hx + hx * jnp.tanh(hx)


def _softplus(x):
    return jnp.maximum(x, 0.0) + jnp.log(1.0 + jnp.exp(-jnp.abs(x)))


def _dot(a, b):
    return jnp.dot(a, b, preferred_element_type=F32)


def _dot_nt(a, b):
    return lax.dot_general(a, b, (((1,), (1,)), ((), ())), preferred_element_type=F32)


def _dot_exact_rhs(a01, v):
    hi = v.astype(BF16)
    r1 = v - hi.astype(F32)
    mid = r1.astype(BF16)
    lo = (r1 - mid.astype(F32)).astype(BF16)
    return _dot(a01, hi) + _dot(a01, mid) + _dot(a01, lo)


def _dot_exact_lhs(v, b01):
    hi = v.astype(BF16)
    r1 = v - hi.astype(F32)
    mid = r1.astype(BF16)
    lo = (r1 - mid.astype(F32)).astype(BF16)
    return _dot(hi, b01) + _dot(mid, b01) + _dot(lo, b01)


def _rope(x, c, s1, s2):
    outs = []
    for i in range(x.shape[1] // LANES):
        xi = x[:, i * LANES:(i + 1) * LANES]
        outs.append(xi * c + pltpu.roll(xi, LANES - ROT_DIM // 2, 1) * s1
                    + pltpu.roll(xi, ROT_DIM // 2, 1) * s2)
    return outs[0] if len(outs) == 1 else jnp.concatenate(outs, axis=1)


def _norm_mm_kernel(x_ref, g_ref, w_ref, o_ref, h_scr):
    @pl.when(pl.program_id(1) == 0)
    def _():
        h_scr[...] = _rms(x_ref[...], g_ref[...]).astype(BF16)

    o_ref[...] = _dot(h_scr[...], w_ref[...]).astype(o_ref.dtype)


def _norm_matmul(x, g, w, layer, out_dtype, tm, tn):
    m, k = x.shape
    n = w.shape[2]
    return pl.pallas_call(
        _norm_mm_kernel,
        grid=(m // tm, n // tn),
        in_specs=[pl.BlockSpec((tm, k), lambda i, j: (i, 0)),
                  pl.BlockSpec((1, k), lambda i, j: (0, 0)),
                  _wspec(layer, (k, tn), lambda i, j: (0, j))],
        out_specs=pl.BlockSpec((tm, tn), lambda i, j: (i, j)),
        out_shape=jax.ShapeDtypeStruct((m, n), out_dtype),
        scratch_shapes=[pltpu.VMEM((tm, k), BF16)],
        compiler_params=_cparams(2),
        name="norm_matmul",
    )(x, g, w)


def _mem_kv_kernel(x_ref, g_ref, wk_ref, wv_ref, *rest):
    k2_ref, v2_ref, k5_ref, v5_ref = rest[-4:]
    h = _rms(x_ref[...], g_ref[...]).astype(BF16)
    for w_ref, o2_ref, o5_ref in ((wk_ref, k2_ref, k5_ref), (wv_ref, v2_ref, v5_ref)):
        res = _dot(h, w_ref[...])
        o2_ref[...] = res
        for hh in range(X_H):
            o5_ref[0, 0, :, hh, :] = res[:, hh * X_HD:(hh + 1) * X_HD]


def _mem_kv(mem, p, nb, mem_len, depth, carry):
    d = X_H * X_HD
    layer = p["layer"]
    const = lambda b: (0, 0)
    slab = lambda b: (layer, b, 0, 0, 0)
    extra = [] if carry is None else list(carry)
    out5 = jax.ShapeDtypeStruct((depth, nb, mem_len, X_H, X_HD), F32)
    return pl.pallas_call(
        _mem_kv_kernel,
        grid=(nb,),
        in_specs=[pl.BlockSpec((mem_len, D_MODEL), lambda b: (b, 0)),
                  pl.BlockSpec((1, D_MODEL), const),
                  _wspec(layer, (D_MODEL, d), const, resident=True),
                  _wspec(layer, (D_MODEL, d), const, resident=True)]
                 + [pl.BlockSpec(memory_space=pl.ANY)] * len(extra),
        out_specs=[pl.BlockSpec((mem_len, d), lambda b: (b, 0)),
                   pl.BlockSpec((mem_len, d), lambda b: (b, 0)),
                   pl.BlockSpec((1, 1, mem_len, X_H, X_HD), slab),
                   pl.BlockSpec((1, 1, mem_len, X_H, X_HD), slab)],
        out_shape=[jax.ShapeDtypeStruct((nb * mem_len, d), F32),
                   jax.ShapeDtypeStruct((nb * mem_len, d), F32), out5, out5],
        input_output_aliases={4: 2, 5: 3} if extra else {},
        compiler_params=_cparams(1),
        name="mem_kv",
    )(mem, p["norm_mem"], p["w_xk"], p["w_xv"], *extra)


def _inproj_kernel(x_ref, g_ref, w_ref, wdt_ref, z_ref, dt_ref, h_scr):
    @pl.when(pl.program_id(1) == 0)
    def _():
        h = _rms(x_ref[...], g_ref[...]).astype(BF16)
        h_scr[...] = h
        dt_ref[...] = _dot(h, wdt_ref[...])

    z_ref[...] = _dot(h_scr[...], w_ref[...]).astype(z_ref.dtype)


def _in_proj(x, g, w, wdt, layer, tm, tn):
    m, k = x.shape
    n = w.shape[2]
    return pl.pallas_call(
        _inproj_kernel,
        grid=(m // tm, n // tn),
        in_specs=[pl.BlockSpec((tm, k), lambda i, j: (i, 0)),
                  pl.BlockSpec((1, k), lambda i, j: (0, 0)),
                  _wspec(layer, (k, tn), lambda i, j: (0, j)),
                  _wspec(layer, (k, LANES), lambda i, j: (0, 0))],
        out_specs=[pl.BlockSpec((tm, tn), lambda i, j: (i, j)),
                   pl.BlockSpec((tm, LANES), lambda i, j: (i, 0))],
        out_shape=[jax.ShapeDtypeStruct((m, n), BF16),
                   jax.ShapeDtypeStruct((m, LANES), F32)],
        scratch_shapes=[pltpu.VMEM((tm, k), BF16)],
        compiler_params=_cparams(2),
        name="in_proj",
    )(x, g, w, wdt)


def _mm_norm_res_kernel(a_ref, w_ref, g_ref, x_ref, o_ref):
    y = _dot(a_ref[...].astype(BF16), w_ref[...])
    o_ref[...] = x_ref[...] + _rms(y, g_ref[...])


def _mm_norm_res(a, w, layer, g, x):
    m, k = a.shape
    n = w.shape[2]
    return pl.pallas_call(
        _mm_norm_res_kernel,
        grid=(1,),
        in_specs=[pl.BlockSpec((m, k), lambda i: (0, 0)),
                  _wspec(layer, (k, n), lambda i: (0, 0)),
                  pl.BlockSpec((1, n), lambda i: (0, 0)),
                  pl.BlockSpec((m, n), lambda i: (0, 0))],
        out_specs=pl.BlockSpec((m, n), lambda i: (0, 0)),
        out_shape=jax.ShapeDtypeStruct((m, n), F32),
        compiler_params=_cparams(1),
        name="mm_norm_res",
    )(a, w, g, x)


def _ssd_kernel(xbc_ref, zs_ref, dtr_ref, cw_ref, cb_ref, dtb_ref, alog_ref, dx_ref, nrm_ref, e_ref,
                y_ref, hout_ref, cbuf_ref, x_scr, ht_scr):
    c = pl.program_id(1)
    q = xbc_ref.shape[0]
    pad = SUBLANES

    halo = x_scr.shape[0]

    @pl.when(c == 0)
    def _():
        x_scr[...] = jnp.zeros(x_scr.shape, BF16)
        ht_scr[...] = jnp.zeros(ht_scr.shape, F32)

    xb = xbc_ref[...]
    xin = xb.astype(F32)
    ext = jnp.concatenate([x_scr[...], xb], axis=0)
    n_sh = SSM_CONV_W - 1
    ri = lax.broadcasted_iota(jnp.int32, (n_sh * q, halo + q), 0)
    ci = lax.broadcasted_iota(jnp.int32, (n_sh * q, halo + q), 1)
    shift_mat = jnp.where(ci == (ri % q) + halo - n_sh + ri // q, 1.0, 0.0).astype(BF16)
    shifted = _dot(shift_mat, ext)
    cw = cw_ref[...]
    xc = cb_ref[...] + xin * cw[n_sh:n_sh + 1, :]
    for k in range(n_sh):
        xc = xc + shifted[k * q:(k + 1) * q, :] * cw[k:k + 1, :]
    xc = _silu(xc)
    x_scr[...] = xb[q - halo:q, :]
    cbuf_ref[0] = xin[q - pad:q, :]

    xs = xc[:, :D_SSM]
    bs = xc[:, D_SSM:D_SSM + G_SSM * N_SSM]
    cs_in = xc[:, D_SSM + G_SSM * N_SSM:]

    dt = _softplus(dtr_ref[...] + dtb_ref[...])
    a = dt * (-jnp.exp(alog_ref[...]))
    row = lax.broadcasted_iota(jnp.int32, (q, q), 0)
    col = lax.broadcasted_iota(jnp.int32, (q, q), 1)
    tri = row >= col
    cs = _dot_exact_rhs(jnp.where(tri, 1.0, 0.0).astype(BF16), a)
    cs_last = cs[q - 1:q, :]
    cs2 = cs * LOG2E
    src_t = (cs2 - jnp.log2(dt)).T
    wdd = dt * jnp.exp(cs_last - cs)
    cdec = jnp.broadcast_to(jnp.exp(cs_last), (SUBLANES, LANES))
    cdec_x = _dot_exact_lhs(cdec, e_ref[...])[0:1, :]

    lane = lax.broadcasted_iota(jnp.int32, (q, LANES), 1)
    heads_per_group = H_SSM // G_SSM
    y_tiles = []
    for g in range(G_SSM):
        bg = bs[:, g * N_SSM:(g + 1) * N_SSM]
        cg = cs_in[:, g * N_SSM:(g + 1) * N_SSM]
        cb = _dot_nt(cg.astype(BF16), bg.astype(BF16))
        xdd_tiles = []
        for pr in range(heads_per_group // 2):
            hp = g * (heads_per_group // 2) + pr
            xpair = xs[:, hp * LANES:(hp + 1) * LANES]
            wp = jnp.concatenate([xpair, ht_scr[:, hp * LANES:(hp + 1) * LANES]], axis=0).astype(BF16)
            parts = []
            wsel = []
            for r2 in range(2):
                h = 2 * hp + r2
                csb = jnp.broadcast_to(cs2[:, h:h + 1], (q, LANES))
                mmat = cb * jnp.exp2(jnp.where(tri, csb - src_t[h:h + 1, :], -jnp.inf))
                csc = cg * jnp.exp2(csb)
                lhs = jnp.concatenate([mmat, csc], axis=1).astype(BF16)
                parts.append(_dot(lhs, wp))
                wsel.append(jnp.broadcast_to(wdd[:, h:h + 1], (q, LANES)))
            y_tiles.append(jnp.where(lane < SSM_HEADDIM, parts[0], parts[1]))
            xdd_tiles.append(xpair * jnp.where(lane < SSM_HEADDIM, wsel[0], wsel[1]))
        xdd = jnp.concatenate(xdd_tiles, axis=1).astype(BF16)
        st_t = _dot(bg.T.astype(BF16), xdd)
        lo, hi = g * heads_per_group * SSM_HEADDIM, (g + 1) * heads_per_group * SSM_HEADDIM
        ht_scr[:, lo:hi] = ht_scr[:, lo:hi] * cdec_x[:, lo:hi] + st_t

    y = jnp.concatenate(y_tiles, axis=1) + xs * dx_ref[...]
    y = y * _silu(zs_ref[...].astype(F32))
    y_ref[...] = _rms(y, nrm_ref[...]).astype(y_ref.dtype)

    @pl.when(c == pl.num_programs(1) - 1)
    def _():
        hout_ref[0] = ht_scr[...].T


def _ssd_prompt(z, dtr, p, nb, seq):
    q = SSD_CHUNK
    nc = seq // q
    row = lambda b, c: b * nc + c
    const = lambda b, c: (0, 0)
    return pl.pallas_call(
        _ssd_kernel,
        grid=(nb, nc),
        in_specs=[pl.BlockSpec((q, SSM_CONV_DIM), lambda b, c: (row(b, c), OFF_XBC // SSM_CONV_DIM)),
                  pl.BlockSpec((q, D_SSM), lambda b, c: (row(b, c), OFF_ZS // D_SSM)),
                  pl.BlockSpec((q, LANES), lambda b, c: (row(b, c), 0)),
                  pl.BlockSpec((SUBLANES, SSM_CONV_DIM), const),
                  pl.BlockSpec((1, SSM_CONV_DIM), const),
                  pl.BlockSpec((1, LANES), const),
                  pl.BlockSpec((1, LANES), const),
                  pl.BlockSpec((1, D_SSM), const),
                  pl.BlockSpec((1, D_SSM), const),
                  pl.BlockSpec((LANES, D_SSM), const)],
        out_specs=[pl.BlockSpec((q, D_SSM), lambda b, c: (row(b, c), 0)),
                   pl.BlockSpec((1, D_SSM, N_SSM), lambda b, c: (b, 0, 0)),
                   pl.BlockSpec((1, SUBLANES, SSM_CONV_DIM), lambda b, c: (b, 0, 0))],
        out_shape=[jax.ShapeDtypeStruct((nb * seq, D_SSM), BF16),
                   jax.ShapeDtypeStruct((nb, D_SSM, N_SSM), F32),
                   jax.ShapeDtypeStruct((nb, SUBLANES, SSM_CONV_DIM), F32)],
        scratch_shapes=[pltpu.VMEM((2 * SUBLANES, SSM_CONV_DIM), BF16),
                        pltpu.VMEM((N_SSM, D_SSM), F32)],
        compiler_params=_cparams(2),
        name="ssd_prompt",
    )(z, z, dtr, p["ssm_conv_w"], p["ssm_conv_b"], p["dt_bias"], p["a_log"], p["d_x"], p["ssm_norm"],
      p["expand"])


def _swa_kernel(sink_ref, q_ref, k_ref, v_ref, c_ref, s1_ref, s2_ref, o_ref, kn_ref, vn_ref,
                kp_scr, vp_scr):
    qb = pl.program_id(1)
    w = WINDOW
    kvw = KV_ATT * HD_ATT

    @pl.when(qb == 0)
    def _():
        kp_scr[...] = jnp.zeros((w, kvw), F32)
        vp_scr[...] = jnp.zeros((w, kvw), F32)

    c, s1, s2 = c_ref[...], s1_ref[...], s2_ref[...]
    qv = _rope(q_ref[...].astype(F32), c, s1, s2) * (HD_ATT ** -0.5)
    ko = _rope(k_ref[...].astype(F32), c, s1, s2)
    vo = v_ref[...].astype(F32)
    kn_ref[0] = ko
    vn_ref[0] = vo
    kcat = jnp.concatenate([kp_scr[...], ko], axis=0)
    vcat = jnp.concatenate([vp_scr[...], vo], axis=0)
    kp_scr[...] = ko
    vp_scr[...] = vo
    vcat_t = vcat.T.astype(BF16)

    rep = H_ATT // KV_ATT
    keyj = lax.broadcasted_iota(jnp.int32, (2 * w, w), 0)
    qryl = lax.broadcasted_iota(jnp.int32, (2 * w, w), 1)
    valid = (keyj > qryl) & (keyj <= qryl + w) & ((qb > 0) | (keyj >= w))
    valid = jnp.concatenate([valid] * rep, axis=1)
    lane_kv = lax.broadcasted_iota(jnp.int32, (2 * w, LANES), 1)
    lane_q = lax.broadcasted_iota(jnp.int32, (w, LANES), 1)
    lo_kv, lo_q = lane_kv < HD_ATT, lane_q < HD_ATT

    scores, sinks = [], []
    for slab in range(kvw // LANES):
        a_k = kcat[:, slab * LANES:(slab + 1) * LANES]
        b_k = pltpu.roll(a_k, HD_ATT, 1)
        for gi in range(2):
            g = 2 * slab + gi
            kdup = jnp.where(lo_kv, a_k, b_k) if gi == 0 else jnp.where(lo_kv, b_k, a_k)
            q_tiles, sink_tiles = [], []
            for pr in range(rep // 2):
                qp = qv[:, (rep // 2 * g + pr) * LANES:(rep // 2 * g + pr + 1) * LANES]
                q_tiles.append(jnp.where(lo_q, qp, 0.0))
                q_tiles.append(jnp.where(lo_q, 0.0, qp))
                for r2 in range(2):
                    sink_tiles.append(jnp.full((1, w), sink_ref[rep * g + 2 * pr + r2], F32))
            qs = jnp.concatenate(q_tiles, axis=0).astype(BF16)
            sinks.append(jnp.concatenate(sink_tiles, axis=1))
            scores.append(_dot_nt(kdup.astype(BF16), qs))
    probs, dens = [], []
    for g in range(KV_ATT):
        s = jnp.where(valid, scores[g], -jnp.inf)
        m = jnp.maximum(jnp.max(s, axis=0, keepdims=True), sinks[g])
        e = jnp.exp(s - m)
        dens.append(jnp.sum(e, axis=0, keepdims=True) + jnp.exp(sinks[g] - m))
        probs.append(e.astype(BF16))
    o_t = []
    for g in range(KV_ATT):
        o_g = _dot(vcat_t[g * HD_ATT:(g + 1) * HD_ATT, :], probs[g]) * (1.0 / dens[g])
        for r in range(rep):
            o_t.append(o_g[:, r * w:(r + 1) * w])
    o_ref[...] = jnp.concatenate(o_t, axis=0).T.astype(o_ref.dtype)


def _swa_prompt(z, p, nb, seq):
    w = WINDOW
    nq = seq // w
    kvw = KV_ATT * HD_ATT
    row = lambda b, i: b * nq + i
    tab = pl.BlockSpec((w, LANES), lambda b, i: (i, 0))
    return pl.pallas_call(
        _swa_kernel,
        grid=(nb, nq),
        in_specs=[pl.BlockSpec(memory_space=pltpu.SMEM),
                  pl.BlockSpec((w, H_ATT * HD_ATT), lambda b, i: (row(b, i), OFF_Q // (H_ATT * HD_ATT))),
                  pl.BlockSpec((w, kvw), lambda b, i: (row(b, i), OFF_K // kvw)),
                  pl.BlockSpec((w, kvw), lambda b, i: (row(b, i), OFF_V // kvw)),
                  tab, tab, tab],
        out_specs=[pl.BlockSpec((w, H_ATT * HD_ATT), lambda b, i: (row(b, i), 0)),
                   pl.BlockSpec((1, w, kvw), lambda b, i: (b, 0, 0)),
                   pl.BlockSpec((1, w, kvw), lambda b, i: (b, 0, 0))],
        out_shape=[jax.ShapeDtypeStruct((nb * seq, H_ATT * HD_ATT), BF16),
                   jax.ShapeDtypeStruct((nb, w, kvw), F32),
                   jax.ShapeDtypeStruct((nb, w, kvw), F32)],
        scratch_shapes=[pltpu.VMEM((w, kvw), F32), pltpu.VMEM((w, kvw), F32)],
        compiler_params=_cparams(2),
        name="swa_prompt",
    )(p["sinks"], z, z, z, p["rope_c"], p["rope_s1"], p["rope_s2"])


def _tail_body(u, gb, ys, oc, ga, gbg, gcg, x, wa_ref, ws_ref, wc_ref, wo_ref, gpost_ref):
    ta = (gb * u).astype(BF16)
    out = None
    half = D_MODEL // 2
    for lo in range(0, D_MODEL, half):
        hi = lo + half
        y_a = _dot(ta, wa_ref[:, lo:hi])
        y_b = _dot(ys, ws_ref[:, lo:hi])
        y_c = _dot(oc, wc_ref[:, lo:hi])
        merged = (_sigmoid(ga[:, lo:hi]) * y_a + _sigmoid(gbg[:, lo:hi]) * y_b
                  + _sigmoid(gcg[:, lo:hi]) * y_c)
        d = _dot(merged.astype(BF16), wo_ref[lo:hi, :])
        out = d if out is None else out + d
    return x + _rms(out, gpost_ref[...])


def _tail_prompt_kernel(va_ref, gb_ref, gc_ref, ys_ref, oc_ref, ga_ref, gbg_ref, gcg_ref, x_ref,
                        cw_ref, wa_ref, ws_ref, wc_ref, wo_ref, gpost_ref, xo_ref, bufo_ref, cv_scr):
    s = pl.program_id(1)
    tm = x_ref.shape[0]
    pad = SUBLANES

    @pl.when(s == 0)
    def _():
        cv_scr[0:pad, :] = jnp.zeros((pad, D_A), F32)

    cv = gc_ref[...].astype(F32) * va_ref[...].astype(F32)
    cv_scr[pad:pad + tm, :] = cv
    cw = cw_ref[...]
    u = cv * cw[CONV_A_W - 1:CONV_A_W, :]
    for k in range(CONV_A_W - 1):
        off = pad - (CONV_A_W - 1) + k
        u = u + cv_scr[off:off + tm, :] * cw[k:k + 1, :]
    tail = cv_scr[tm:tm + pad, :]
    cv_scr[0:pad, :] = tail
    bufo_ref[0] = tail
    xo_ref[...] = _tail_body(u, gb_ref[...].astype(F32), ys_ref[...], oc_ref[...],
                             ga_ref[...].astype(F32), gbg_ref[...].astype(F32), gcg_ref[...].astype(F32),
                             x_ref[...], wa_ref, ws_ref, wc_ref, wo_ref, gpost_ref)


def _tail_prompt(z, ys, oc, x, p, nb, seq, tm):
    ns = seq // tm
    row = lambda b, s: b * ns + s
    const = lambda b, s: (0, 0)
    zblk = lambda off: pl.BlockSpec((tm, D_MODEL), lambda b, s: (row(b, s), off // D_MODEL))
    return pl.pallas_call(
        _tail_prompt_kernel,
        grid=(nb, ns),
        in_specs=[zblk(OFF_ZA), zblk(OFF_ZA + D_A), zblk(OFF_ZA + 2 * D_A),
                  pl.BlockSpec((tm, D_SSM), lambda b, s: (row(b, s), 0)),
                  pl.BlockSpec((tm, H_ATT * HD_ATT), lambda b, s: (row(b, s), 0)),
                  zblk(OFF_GATES), zblk(OFF_GATES + D_MODEL), zblk(OFF_GATES + 2 * D_MODEL),
                  pl.BlockSpec((tm, D_MODEL), lambda b, s: (row(b, s), 0)),
                  pl.BlockSpec((SUBLANES, D_A), const),
                  _wspec(p["layer"], (D_A, D_MODEL), const, resident=True),
                  _wspec(p["layer"], (D_SSM, D_MODEL), const, resident=True),
                  _wspec(p["layer"], (H_ATT * HD_ATT, D_MODEL), const, resident=True),
                  _wspec(p["layer"], (D_MODEL, D_MODEL), const, resident=True),
                  pl.BlockSpec((1, D_MODEL), const)],
        out_specs=[pl.BlockSpec((tm, D_MODEL), lambda b, s: (row(b, s), 0)),
                   pl.BlockSpec((1, SUBLANES, D_A), lambda b, s: (b, 0, 0))],
        out_shape=[jax.ShapeDtypeStruct((nb * seq, D_MODEL), F32),
                   jax.ShapeDtypeStruct((nb, SUBLANES, D_A), F32)],
        scratch_shapes=[pltpu.VMEM((tm + SUBLANES, D_A), F32)],
        compiler_params=_cparams(2, VMEM_LIMIT_BIG),
        name="tail_prompt",
    )(z, z, z, ys, oc, z, z, z, x, p["conv_a_w"], p["w_a_out"], p["w_ssm_out"], p["w_attn_out"],
      p["w_out"], p["norm_mix_post"])


def _tail_sample_kernel(va_ref, gb_ref, gc_ref, ys_ref, oc_ref, ga_ref, gbg_ref, gcg_ref, x_ref,
                        b0_ref, b1_ref, cw_ref, wa_ref, ws_ref, wc_ref, wo_ref, gpost_ref,
                        xo_ref, cvo_ref):
    cv = gc_ref[...].astype(F32) * va_ref[...].astype(F32)
    cw = cw_ref[...]
    u = b0_ref[...] * cw[0:1, :] + b1_ref[...] * cw[1:2, :] + cv * cw[2:3, :]
    cvo_ref[...] = cv
    xo_ref[...] = _tail_body(u, gb_ref[...].astype(F32), ys_ref[...], oc_ref[...],
                             ga_ref[...].astype(F32), gbg_ref[...].astype(F32), gcg_ref[...].astype(F32),
                             x_ref[...], wa_ref, ws_ref, wc_ref, wo_ref, gpost_ref)


def _tail_sample(z, ys, oc, x, b0, b1, p):
    m = x.shape[0]
    const = lambda i: (0, 0)
    zblk = lambda off: pl.BlockSpec((m, D_MODEL), lambda i: (0, off // D_MODEL))
    full = lambda a: pl.BlockSpec(a.shape, const)
    wfull = lambda a: _wspec(p["layer"], a.shape[1:], const)
    w_c = p["w_attn_out_exp"]
    return pl.pallas_call(
        _tail_sample_kernel,
        grid=(1,),
        in_specs=[zblk(OFF_ZA), zblk(OFF_ZA + D_A), zblk(OFF_ZA + 2 * D_A),
                  full(ys), full(oc),
                  zblk(OFF_GATES), zblk(OFF_GATES + D_MODEL), zblk(OFF_GATES + 2 * D_MODEL),
                  full(x), full(b0), full(b1),
                  full(p["conv_a_w"]), wfull(p["w_a_out"]), wfull(p["w_ssm_out"]), wfull(w_c),
                  wfull(p["w_out"]), full(p["norm_mix_post"])],
        out_specs=[pl.BlockSpec((m, D_MODEL), const), pl.BlockSpec((m, D_A), const)],
        out_shape=[jax.ShapeDtypeStruct((m, D_MODEL), F32), jax.ShapeDtypeStruct((m, D_A), F32)],
        compiler_params=_cparams(1, VMEM_LIMIT_BIG),
        name="tail_sample",
    )(z, z, z, ys, oc, z, z, z, x, b0, b1, p["conv_a_w"], p["w_a_out"], p["w_ssm_out"], w_c,
      p["w_out"], p["norm_mix_post"])


def _xattn_prompt_kernel(x_ref, k_ref, v_ref, gpre_ref, wq_ref, wo_ref, gpost_ref, xo_ref):
    x = x_ref[...]
    h = _rms(x, gpre_ref[...]).astype(BF16)
    qv = (_dot(h, wq_ref[...]) * (X_HD ** -0.5)).astype(BF16)
    tm = x.shape[0]
    s = jnp.concatenate(
        [_dot_nt(qv[:, hh * X_HD:(hh + 1) * X_HD], k_ref[:, hh * X_HD:(hh + 1) * X_HD].astype(BF16))
         for hh in range(X_H)], axis=0)
    m = jnp.max(s, axis=-1, keepdims=True)
    e = jnp.exp(s - m)
    pb = (e * (1.0 / jnp.sum(e, axis=-1, keepdims=True))).astype(BF16)
    o = jnp.concatenate(
        [_dot(pb[hh * tm:(hh + 1) * tm], v_ref[:, hh * X_HD:(hh + 1) * X_HD].astype(BF16))
         for hh in range(X_H)], axis=1).astype(BF16)
    xo_ref[...] = x + _rms(_dot(o, wo_ref[...]), gpost_ref[...])


def _xattn_prompt(x, mk, mv, p, nb, seq, mem_len, tm):
    ns = seq // tm
    const = lambda b, s: (0, 0)
    return pl.pallas_call(
        _xattn_prompt_kernel,
        grid=(nb, ns),
        in_specs=[pl.BlockSpec((tm, D_MODEL), lambda b, s: (b * ns + s, 0)),
                  pl.BlockSpec((mem_len, X_H * X_HD), lambda b, s: (b, 0)),
                  pl.BlockSpec((mem_len, X_H * X_HD), lambda b, s: (b, 0)),
                  pl.BlockSpec((1, D_MODEL), const),
                  _wspec(p["layer"], (D_MODEL, X_H * X_HD), const, resident=True),
                  _wspec(p["layer"], (X_H * X_HD, D_MODEL), const, resident=True),
                  pl.BlockSpec((1, D_MODEL), const)],
        out_specs=pl.BlockSpec((tm, D_MODEL), lambda b, s: (b * ns + s, 0)),
        out_shape=jax.ShapeDtypeStruct((nb * seq, D_MODEL), F32),
        compiler_params=_cparams(2),
        name="xattn_prompt",
    )(x, mk, mv, p["norm_x_pre"], p["w_xq"], p["w_xo"], p["norm_x_post"])


def _ffn_finish(kf, part, x_ref, gpost_ref, xo_ref, acc_scr):
    @pl.when(kf == 0)
    def _():
        acc_scr[...] = part

    @pl.when(kf > 0)
    def _():
        acc_scr[...] = acc_scr[...] + part

    @pl.when(kf == pl.num_programs(2) - 1)
    def _():
        xo_ref[...] = x_ref[...] + _rms(acc_scr[...], gpost_ref[...])


def _ffn_prompt_kernel(x_ref, gpre_ref, win_ref, cw_ref, cb_ref, wo_ref, gpost_ref,
                       xo_ref, bufo_ref, a_scr):
    s = pl.program_id(1)
    tm = x_ref.shape[0]
    pad = SUBLANES

    @pl.when(s == 0)
    def _():
        a_scr[0:pad, :] = jnp.zeros((pad, D_FF), F32)

    x = x_ref[...]
    h = _rms(x, gpre_ref[...]).astype(BF16)
    cw = cw_ref[...]
    cb = cb_ref[...]

    def up(lo, hi):
        return _dot(h, win_ref[:, lo:hi]), _dot(h, win_ref[:, D_FF + lo:D_FF + hi])

    bounds = [(lo, min(lo + FFN_SUB, D_FF)) for lo in range(0, D_FF, FFN_SUB)]
    out = None
    nxt = up(*bounds[0])
    for i, (lo, hi) in enumerate(bounds):
        a, gate = nxt
        if i + 1 < len(bounds):
            nxt = up(*bounds[i + 1])
        a_scr[pad:pad + tm, lo:hi] = a
        ac = cb[:, lo:hi] + a * cw[FFN_CONV_W - 1:FFN_CONV_W, lo:hi]
        for k in range(FFN_CONV_W - 1):
            off = pad - (FFN_CONV_W - 1) + k
            ac = ac + a_scr[off:off + tm, lo:hi] * cw[k:k + 1, lo:hi]
        d = _dot((_silu(ac) * gate).astype(BF16), wo_ref[lo:hi, :])
        out = d if out is None else out + d
    tail = a_scr[tm:tm + pad, :]
    a_scr[0:pad, :] = tail
    bufo_ref[0] = tail
    xo_ref[...] = x + _rms(out, gpost_ref[...])


def _ffn_prompt(x, p, nb, seq, tm):
    ns = seq // tm
    const = lambda b, s: (0, 0)
    return pl.pallas_call(
        _ffn_prompt_kernel,
        grid=(nb, ns),
        in_specs=[pl.BlockSpec((tm, D_MODEL), lambda b, s: (b * ns + s, 0)),
                  pl.BlockSpec((1, D_MODEL), const),
                  _wspec(p["layer"], (D_MODEL, 2 * D_FF), const, resident=True),
                  pl.BlockSpec((SUBLANES, D_FF), const),
                  pl.BlockSpec((1, D_FF), const),
                  _wspec(p["layer"], (D_FF, D_MODEL), const, resident=True),
                  pl.BlockSpec((1, D_MODEL), const)],
        out_specs=[pl.BlockSpec((tm, D_MODEL), lambda b, s: (b * ns + s, 0)),
                   pl.BlockSpec((1, SUBLANES, D_FF), lambda b, s: (b, 0, 0))],
        out_shape=[jax.ShapeDtypeStruct((nb * seq, D_MODEL), F32),
                   jax.ShapeDtypeStruct((nb, SUBLANES, D_FF), F32)],
        scratch_shapes=[pltpu.VMEM((tm + SUBLANES, D_FF), F32)],
        compiler_params=_cparams(2, VMEM_LIMIT_BIG),
        name="ffn_prompt",
    )(x, p["norm_ffn_pre"], p["w_ffn_in"], p["ffn_conv_w"], p["ffn_conv_b"],
      p["w_ffn_out"], p["norm_ffn_post"])


def _ffn_sample_kernel(x_ref, gpre_ref, wa_ref, wg_ref, cw_ref, cb_ref, b0_ref, b1_ref, wo_ref,
                       gpost_ref, xo_ref, ao_ref, h_scr, acc_scr):
    kf = pl.program_id(2)

    @pl.when(kf == 0)
    def _():
        h_scr[...] = _rms(x_ref[...], gpre_ref[...]).astype(BF16)

    h = h_scr[...]
    a = _dot(h, wa_ref[...])
    gate = _dot(h, wg_ref[...])
    ao_ref[...] = a
    cw = cw_ref[...]
    ac = cb_ref[...] + b0_ref[...] * cw[0:1, :] + b1_ref[...] * cw[1:2, :] + a * cw[2:3, :]
    part = _dot((_silu(ac) * gate).astype(BF16), wo_ref[...])
    _ffn_finish(kf, part, x_ref, gpost_ref, xo_ref, acc_scr)


def _ffn_sample(x, b0, b1, p):
    m = x.shape[0]
    nk = D_FF // FF_CHUNK
    const = lambda b, s, k: (0, 0)
    chunk = lambda b, s, k: (0, k)
    return pl.pallas_call(
        _ffn_sample_kernel,
        grid=(1, 1, nk),
        in_specs=[pl.BlockSpec((m, D_MODEL), const),
                  pl.BlockSpec((1, D_MODEL), const),
                  _wspec(p["layer"], (D_MODEL, FF_CHUNK), chunk),
                  _wspec(p["layer"], (D_MODEL, FF_CHUNK), lambda b, s, k: (0, nk + k)),
                  pl.BlockSpec((SUBLANES, FF_CHUNK), chunk),
                  pl.BlockSpec((1, FF_CHUNK), chunk),
                  pl.BlockSpec((m, FF_CHUNK), chunk),
                  pl.BlockSpec((m, FF_CHUNK), chunk),
                  _wspec(p["layer"], (FF_CHUNK, D_MODEL), lambda b, s, k: (k, 0)),
                  pl.BlockSpec((1, D_MODEL), const)],
        out_specs=[pl.BlockSpec((m, D_MODEL), const),
                   pl.BlockSpec((m, FF_CHUNK), chunk)],
        out_shape=[jax.ShapeDtypeStruct((m, D_MODEL), F32),
                   jax.ShapeDtypeStruct((m, D_FF), F32)],
        scratch_shapes=[pltpu.VMEM((m, D_MODEL), BF16),
                        pltpu.VMEM((m, D_MODEL), F32)],
        compiler_params=_cparams(3),
        name="ffn_sample",
    )(x, p["norm_ffn_pre"], p["w_ffn_in"], p["w_ffn_in"], p["ffn_conv_w"], p["ffn_conv_b"], b0, b1,
      p["w_ffn_out"], p["norm_ffn_post"])


def _dec_prep_kernel(xbc_ref, dtr_ref, b0_ref, b1_ref, b2_ref, cw_ref, cb_ref, dtb_ref, alog_ref,
                     dx_ref, e_ref, xdt_ref, dec_ref, bs_ref, cs_ref, xsd_ref):
    cw = cw_ref[...]
    xc = (cb_ref[...] + b0_ref[...] * cw[0:1, :] + b1_ref[...] * cw[1:2, :] + b2_ref[...] * cw[2:3, :]
          + xbc_ref[...].astype(F32) * cw[3:4, :])
    xc = _silu(xc)
    xs = xc[:, :D_SSM]
    bs_ref[...] = xc[:, D_SSM:D_SSM + G_SSM * N_SSM]
    cs_ref[...] = xc[:, D_SSM + G_SSM * N_SSM:]
    dt = _softplus(dtr_ref[...] + dtb_ref[...])
    dec_ref[...] = jnp.exp(dt * (-jnp.exp(alog_ref[...])))
    xdt_ref[...] = xs * _dot_exact_lhs(dt, e_ref[...])
    xsd_ref[...] = xs * dx_ref[...]


def _dec_prep(z, dtr, b0, b1, b2, p):
    m = dtr.shape[0]
    const = lambda i: (0, 0)
    full = lambda a: pl.BlockSpec(a.shape, const)
    return pl.pallas_call(
        _dec_prep_kernel,
        grid=(1,),
        in_specs=[pl.BlockSpec((m, SSM_CONV_DIM), lambda i: (0, OFF_XBC // SSM_CONV_DIM)),
                  full(dtr), full(b0), full(b1), full(b2), full(p["ssm_conv_w"]), full(p["ssm_conv_b"]),
                  full(p["dt_bias"]), full(p["a_log"]), full(p["d_x"]), full(p["expand"])],
        out_specs=[pl.BlockSpec((m, D_SSM), const), pl.BlockSpec((m, LANES), const),
                   pl.BlockSpec((m, G_SSM * N_SSM), const), pl.BlockSpec((m, G_SSM * N_SSM), const),
                   pl.BlockSpec((m, D_SSM), const)],
        out_shape=[jax.ShapeDtypeStruct((m, D_SSM), F32), jax.ShapeDtypeStruct((m, LANES), F32),
                   jax.ShapeDtypeStruct((m, G_SSM * N_SSM), F32),
                   jax.ShapeDtypeStruct((m, G_SSM * N_SSM), F32),
                   jax.ShapeDtypeStruct((m, D_SSM), F32)],
        compiler_params=_cparams(1),
        name="dec_prep",
    )(z, dtr, b0, b1, b2, p["ssm_conv_w"], p["ssm_conv_b"], p["dt_bias"], p["a_log"], p["d_x"],
      p["expand"])


def _dec_state_kernel(dec_ref, st_ref, xdt_ref, bs_ref, cs_ref, *rest):
    so_ref, y_ref = rest[-2:]
    bt = st_ref.shape[0]
    rows_per_group = D_SSM // G_SSM
    grp = lax.broadcasted_iota(jnp.int32, (G_SSM, D_SSM), 0)
    own = (lax.broadcasted_iota(jnp.int32, (G_SSM, D_SSM), 1) // rows_per_group) == grp
    for i in range(bt):
        b = pl.program_id(0) * bt + i
        x = xdt_ref[pl.ds(b, 1), :]
        x8 = jnp.where(own, jnp.broadcast_to(x, own.shape), 0.0).astype(BF16)
        outer = lax.dot_general(x8, bs_ref[b].astype(BF16), (((0,), (0,)), ((), ())),
                                preferred_element_type=F32)
        for h in range(H_SSM):
            lo, hi = h * SSM_HEADDIM, (h + 1) * SSM_HEADDIM
            so_ref[i, lo:hi, :] = st_ref[i, lo:hi, :] * dec_ref[b, h] + outer[lo:hi, :]
        y8 = _dot_nt(cs_ref[b].astype(BF16), so_ref[i].astype(BF16))
        y_ref[pl.ds(b, 1), :] = jnp.sum(jnp.where(own, y8, 0.0), axis=0, keepdims=True)


def _dec_state(state_all, layer, dec, xdt, bs3, cs3, carry):
    nb = xdt.shape[0]
    bt = _pick_tile(nb, 4)
    const2 = lambda b: (0, 0)
    const3 = lambda b: (0, 0, 0)
    slab = lambda b: (layer * (nb // bt) + b, 0, 0)
    extra = [] if carry is None else [carry]
    return pl.pallas_call(
        _dec_state_kernel,
        grid=(nb // bt,),
        in_specs=[pl.BlockSpec(memory_space=pltpu.SMEM),
                  pl.BlockSpec((bt, D_SSM, N_SSM), slab),
                  pl.BlockSpec(xdt.shape, const2),
                  pl.BlockSpec(bs3.shape, const3), pl.BlockSpec(cs3.shape, const3)]
                 + [pl.BlockSpec(memory_space=pl.ANY)] * len(extra),
        out_specs=[pl.BlockSpec((bt, D_SSM, N_SSM), slab),
                   pl.BlockSpec((nb, D_SSM), const2)],
        out_shape=[jax.ShapeDtypeStruct(state_all.shape, F32),
                   jax.ShapeDtypeStruct((nb, D_SSM), F32)],
        input_output_aliases={5: 0} if extra else {},
        compiler_params=_cparams(1),
        name="dec_state",
    )(dec, state_all, xdt, bs3, cs3, *extra)


def _dec_post_kernel(y_ref, xsd_ref, zs_ref, nrm_ref, o_ref):
    y = y_ref[...] + xsd_ref[...]
    y = y * _silu(zs_ref[...].astype(F32))
    o_ref[...] = _rms(y, nrm_ref[...]).astype(o_ref.dtype)


def _dec_post(yt, xsd, z, p):
    m = xsd.shape[0]
    const = lambda i: (0, 0)
    return pl.pallas_call(
        _dec_post_kernel,
        grid=(1,),
        in_specs=[pl.BlockSpec(yt.shape, const), pl.BlockSpec(xsd.shape, const),
                  pl.BlockSpec((m, D_SSM), lambda i: (0, OFF_ZS // D_SSM)),
                  pl.BlockSpec((1, D_SSM), const)],
        out_specs=pl.BlockSpec((m, D_SSM), const),
        out_shape=jax.ShapeDtypeStruct((m, D_SSM), BF16),
        compiler_params=_cparams(1),
        name="dec_post",
    )(yt, xsd, z, p["ssm_norm"])


def _dec_swa_kernel(qe_ref, kn_ref, vn_ref, ck_ref, cv_ref, c_ref, s1_ref, s2_ref, sink_ref, *rest):
    nk_ref, nv_ref, oe_ref = rest[-3:]
    bt = ck_ref.shape[0]
    w = ck_ref.shape[1]
    c, s1, s2 = c_ref[0:1, :], s1_ref[0:1, :], s2_ref[0:1, :]
    kn = _rope(kn_ref[...].astype(F32), c, s1, s2)
    vn = vn_ref[...].astype(F32)
    last = lax.broadcasted_iota(jnp.int32, (w, KV_ATT * HD_ATT), 0) == w - 1
    nh = qe_ref.shape[1]
    qe = (_rope(qe_ref[...].astype(F32).reshape(bt * nh, KV_ATT * HD_ATT), c, s1, s2)
          * (HD_ATT ** -0.5)).astype(BF16)
    nvs, s_tiles = [], []
    for i in range(bt):
        nk = jnp.where(last, kn[i:i + 1, :], pltpu.roll(ck_ref[i], w - 1, 0))
        nv = jnp.where(last, vn[i:i + 1, :], pltpu.roll(cv_ref[i], w - 1, 0))
        nk_ref[i] = nk
        nv_ref[i] = nv
        nvs.append(nv.astype(BF16))
        s_tiles.append(_dot_nt(qe[i * nh:(i + 1) * nh], nk.astype(BF16)))
    s = jnp.concatenate(s_tiles, axis=0)
    sink = jnp.concatenate([sink_ref[:, 0:1]] * bt, axis=0)
    m = jnp.maximum(jnp.max(s, axis=-1, keepdims=True), sink)
    e = jnp.exp(s - m)
    den = jnp.sum(e, axis=-1, keepdims=True) + jnp.exp(sink - m)
    pb = (e * (1.0 / den)).astype(BF16)
    for i in range(bt):
        oe_ref[i] = _dot(pb[i * nh:(i + 1) * nh], nvs[i]).astype(oe_ref.dtype)


def _dec_swa(qe, z, ck_all, cv_all, layer, p, bt, carry):
    nb = qe.shape[0]
    _, w, kvw = ck_all.shape
    const = lambda i: (0, 0)
    slab = lambda i: (layer * (nb // bt) + i, 0, 0)
    tab = pl.BlockSpec((SUBLANES, LANES), const)
    extra = [] if carry is None else list(carry)
    return pl.pallas_call(
        _dec_swa_kernel,
        grid=(nb // bt,),
        in_specs=[pl.BlockSpec((bt, H_ATT, kvw), lambda i: (i, 0, 0)),
                  pl.BlockSpec((bt, kvw), lambda i: (i, OFF_K // kvw)),
                  pl.BlockSpec((bt, kvw), lambda i: (i, OFF_V // kvw)),
                  pl.BlockSpec((bt, w, kvw), slab),
                  pl.BlockSpec((bt, w, kvw), slab),
                  tab, tab, tab,
                  pl.BlockSpec((H_ATT, LANES), const)]
                 + [pl.BlockSpec(memory_space=pl.ANY)] * len(extra),
        out_specs=[pl.BlockSpec((bt, w, kvw), slab),
                   pl.BlockSpec((bt, w, kvw), slab),
                   pl.BlockSpec((bt, H_ATT, kvw), lambda i: (i, 0, 0))],
        out_shape=[jax.ShapeDtypeStruct(ck_all.shape, F32), jax.ShapeDtypeStruct(cv_all.shape, F32),
                   jax.ShapeDtypeStruct((nb, H_ATT, kvw), BF16)],
        input_output_aliases={9: 0, 10: 1} if extra else {},
        compiler_params=_cparams(1),
        name="dec_swa",
    )(qe, z, z, ck_all, cv_all, p["rope_c_s"], p["rope_s1_s"], p["rope_s2_s"], p["sinks_x"], *extra)


def _dec_xattn_kernel(q_ref, k_ref, v_ref, o_ref):
    bt, mem_len = k_ref.shape[1], k_ref.shape[2]
    rows = mem_len * X_H
    col_head = lax.broadcasted_iota(jnp.int32, (SUBLANES, rows), 1) % X_H
    own = col_head == lax.broadcasted_iota(jnp.int32, (SUBLANES, rows), 0) % X_H
    for i in range(bt):
        qh = q_ref[i] * (X_HD ** -0.5)
        q8 = jnp.concatenate([qh] * (SUBLANES // X_H), axis=0).astype(BF16)
        k2 = k_ref[0, i].reshape(rows, X_HD).astype(BF16)
        v2 = v_ref[0, i].reshape(rows, X_HD).astype(BF16)
        s = _dot_nt(q8, k2)
        s = jnp.where(own, s, -jnp.inf)
        m = jnp.max(s, axis=-1, keepdims=True)
        e = jnp.exp(s - m)
        den = jnp.sum(e, axis=-1, keepdims=True)
        o8 = _dot(e.astype(BF16), v2) * (1.0 / den)
        o_ref[i] = o8[0:X_H, :]


def _dec_xattn(qv, mem_k, mem_v, layer, bt):
    nb = qv.shape[0]
    _, _, mem_len, nh, hd = mem_k.shape
    slab = lambda i: (layer, i, 0, 0, 0)
    return pl.pallas_call(
        _dec_xattn_kernel,
        grid=(nb // bt,),
        in_specs=[pl.BlockSpec((bt, nh, hd), lambda i: (i, 0, 0)),
                  pl.BlockSpec((1, bt, mem_len, nh, hd), slab),
                  pl.BlockSpec((1, bt, mem_len, nh, hd), slab)],
        out_specs=pl.BlockSpec((bt, nh, hd), lambda i: (i, 0, 0)),
        out_shape=jax.ShapeDtypeStruct((nb, nh, hd), F32),
        compiler_params=_cparams(1),
        name="dec_xattn",
    )(qv, mem_k, mem_v)


def _pad_rows(w, rows=SUBLANES):
    return jnp.pad(w, ((0, rows - w.shape[0]), (0, 0)))


def _pad_lanes(v, lanes=LANES):
    return jnp.pad(v, (0, lanes - v.shape[0])).reshape(1, lanes)


def _rope_tables(pos):
    half = ROT_DIM // 2
    inv = ROPE_THETA ** (-jnp.arange(half, dtype=F32) / half)
    ang = pos.astype(F32)[:, None] * inv[None, :]
    cos, sin = jnp.cos(ang), jnp.sin(ang)
    n = pos.shape[0]
    ones = jnp.ones((n, HD_ATT - ROT_DIM), F32)
    zeros = jnp.zeros((n, HD_ATT - ROT_DIM), F32)
    zh = jnp.zeros((n, half), F32)
    c = jnp.concatenate([cos, cos, ones], axis=1)
    s1 = jnp.concatenate([-sin, zh, zeros], axis=1)
    s2 = jnp.concatenate([zh, sin, zeros], axis=1)
    rep = LANES // HD_ATT
    return tuple(jnp.tile(t, (1, rep)) for t in (c, s1, s2))


_BF16_WEIGHTS = ("w_a_out", "w_ssm_out", "w_attn_out", "w_out", "w_xq", "w_xk", "w_xv", "w_xo",
                 "w_ffn_in", "w_ffn_out")


def _prep_weights(prm):
    w_in = prm["w_in"]
    depth = w_in.shape[0]
    o = np.cumsum([0, D_A, D_A, D_A, D_SSM, SSM_CONV_DIM, H_SSM, H_ATT * HD_ATT, KV_ATT * HD_ATT,
                   KV_ATT * HD_ATT, 3 * D_MODEL])
    col = lambda i: w_in[:, :, o[i]:o[i + 1]]
    w = {name: prm[name].astype(BF16) for name in _BF16_WEIGHTS}
    w["w_main"] = jnp.concatenate(
        [col(4), col(3), col(0), col(1), col(2), col(9), col(6), col(7), col(8)], axis=-1).astype(BF16)
    w["w_dt"] = jnp.pad(col(5), ((0, 0), (0, 0), (0, LANES - H_SSM))).astype(BF16)
    rep = H_ATT // KV_ATT
    wq, wc = col(6).astype(BF16), w["w_attn_out"]
    q_blocks, c_blocks = [], []
    for h in range(H_ATT):
        g = h // rep
        q_blocks += [jnp.zeros((depth, D_MODEL, g * HD_ATT), BF16),
                     wq[:, :, h * HD_ATT:(h + 1) * HD_ATT],
                     jnp.zeros((depth, D_MODEL, (KV_ATT - 1 - g) * HD_ATT), BF16)]
        c_blocks += [jnp.zeros((depth, g * HD_ATT, D_MODEL), BF16),
                     wc[:, h * HD_ATT:(h + 1) * HD_ATT, :],
                     jnp.zeros((depth, (KV_ATT - 1 - g) * HD_ATT, D_MODEL), BF16)]
    w["wq_exp"] = jnp.concatenate(q_blocks, axis=2)
    w["w_attn_out_exp"] = jnp.concatenate(c_blocks, axis=1)
    return w


def _prep_layer(l, prm, w, seq):
    expand = jnp.asarray(np.kron(np.eye(LANES, H_SSM, dtype=np.float32),
                                 np.ones((1, SSM_HEADDIM), np.float32))).astype(BF16)
    rope_p = _rope_tables(jnp.arange(seq, dtype=jnp.int32))
    rope_s = _rope_tables(jnp.full((SUBLANES,), PAST_LEN, jnp.int32))
    return {
        **w,
        "layer": l,
        "norm_mix_pre": prm["norm_mix_pre"][l].reshape(1, -1),
        "norm_mix_post": prm["norm_mix_post"][l].reshape(1, -1),
        "conv_a_w": _pad_rows(prm["conv_a_w"][l]),
        "ssm_conv_w": _pad_rows(prm["ssm_conv_w"][l]),
        "ssm_conv_b": prm["ssm_conv_b"][l].reshape(1, -1),
        "dt_bias": _pad_lanes(prm["ssm_dt_bias"][l]),
        "a_log": _pad_lanes(prm["ssm_a_log"][l]),
        "d_x": jnp.repeat(prm["ssm_d"][l], SSM_HEADDIM).reshape(1, -1),
        "ssm_norm": prm["ssm_norm"][l].reshape(1, -1),
        "expand": expand,
        "sinks": prm["attn_sinks"][l],
        "sinks_x": jnp.broadcast_to(prm["attn_sinks"][l][:, None], (H_ATT, LANES)),
        "rope_c": rope_p[0], "rope_s1": rope_p[1], "rope_s2": rope_p[2],
        "rope_c_s": rope_s[0], "rope_s1_s": rope_s[1], "rope_s2_s": rope_s[2],
        "norm_x_pre": prm["norm_x_pre"][l].reshape(1, -1),
        "norm_x_post": prm["norm_x_post"][l].reshape(1, -1),
        "norm_mem": prm["norm_mem"][l].reshape(1, -1),
        "norm_ffn_pre": prm["norm_ffn_pre"][l].reshape(1, -1),
        "norm_ffn_post": prm["norm_ffn_post"][l].reshape(1, -1),
        "ffn_conv_w": _pad_rows(prm["ffn_conv_w"][l]),
        "ffn_conv_b": prm["ffn_conv_b"][l].reshape(1, -1),
    }


def _pick_tile(n, pref):
    t = min(n, pref)
    while n % t:
        t //= 2
    return t


def _prompt_layer(x, mem, p, nb, seq, mem_len, depth, mem_carry):
    rows = nb * seq
    z, dtr = _in_proj(x, p["norm_mix_pre"], p["w_main"], p["w_dt"], p["layer"], _pick_tile(rows, 1024),
                      2304)
    ys, h_last, cbuf = _ssd_prompt(z, dtr, p, nb, seq)
    oc, k_new, v_new = _swa_prompt(z, p, nb, seq)
    x, bufa = _tail_prompt(z, ys, oc, x, p, nb, seq, _pick_tile(seq, 512))
    mk, mv, mk_all, mv_all = _mem_kv(mem, p, nb, mem_len, depth, mem_carry)
    x = _xattn_prompt(x, mk, mv, p, nb, seq, mem_len, _pick_tile(seq, 1024))
    x, buff = _ffn_prompt(x, p, nb, seq, _pick_tile(seq, 512))
    state = (bufa[:, SUBLANES - (CONV_A_W - 1):],
             cbuf[:, SUBLANES - (SSM_CONV_W - 1):],
             h_last.reshape(nb, H_SSM, SSM_HEADDIM, N_SSM),
             k_new.reshape(nb, WINDOW, KV_ATT, HD_ATT),
             v_new.reshape(nb, WINDOW, KV_ATT, HD_ATT),
             buff[:, SUBLANES - (FFN_CONV_W - 1):])
    return x, state, (mk_all, mv_all)


def _sample_layer(x, layer, st, big, carry, p):
    buf_a, buf_ssm, buf_ffn = st
    ssm_all, swa_k_all, swa_v_all, mem_k_all, mem_v_all = big
    nb = x.shape[0]
    kvw = KV_ATT * HD_ATT
    z, dtr = _in_proj(x, p["norm_mix_pre"], p["w_main"], p["w_dt"], layer, nb, 1536)
    qe = _norm_matmul(x, p["norm_mix_pre"], p["wq_exp"], layer, BF16, nb, 1024).reshape(nb, H_ATT, kvw)
    xdt, dec, bs, cs, xsd = _dec_prep(z, dtr, buf_ssm[:, 0], buf_ssm[:, 1], buf_ssm[:, 2], p)
    new_ssm, y_ssd = _dec_state(ssm_all, layer, dec[:, :H_SSM], xdt, bs.reshape(nb, G_SSM, N_SSM),
                                cs.reshape(nb, G_SSM, N_SSM), None if carry is None else carry[0])
    ys = _dec_post(y_ssd, xsd, z, p)
    new_k, new_v, oe = _dec_swa(qe, z, swa_k_all, swa_v_all, layer, p, _pick_tile(nb, 16),
                                None if carry is None else carry[1:3])
    x, cv = _tail_sample(z, ys, oe.reshape(nb, H_ATT * kvw), x, buf_a[:, 0], buf_a[:, 1], p)
    qx = _norm_matmul(x, p["norm_x_pre"], p["w_xq"], layer, F32, nb, 1024)
    ox = _dec_xattn(qx.reshape(nb, X_H, X_HD), mem_k_all, mem_v_all, layer, _pick_tile(nb, 4))
    x = _mm_norm_res(ox.reshape(nb, X_H * X_HD), p["w_xo"], layer, p["norm_x_post"], x)
    x, a_up = _ffn_sample(x, buf_ffn[:, 0], buf_ffn[:, 1], p)
    x_raw = z[:, OFF_XBC:OFF_XBC + SSM_CONV_DIM].astype(F32)
    small = (jnp.stack([buf_a[:, 1], cv], axis=1),
             jnp.concatenate([buf_ssm[:, 1:], x_raw[:, None, :]], axis=1),
             jnp.stack([buf_ffn[:, 1], a_up], axis=1))
    return x, small, (new_ssm, new_k, new_v)


def kernel(x_prompt, x_sample, mem_prompt, state_conv_a, state_ssm_conv, state_ssm, cache_swa_k, cache_swa_v, cache_mem_k, cache_mem_v, state_ffn_conv, norm_mix_pre, norm_mix_post, w_in, conv_a_w, w_a_out, ssm_conv_w, ssm_conv_b, ssm_dt_bias, ssm_a_log, ssm_d, ssm_norm, w_ssm_out, attn_sinks, w_attn_out, w_out, norm_x_pre, norm_x_post, norm_mem, w_xq, w_xk, w_xv, w_xo, norm_ffn_pre, norm_ffn_post, w_ffn_in, ffn_conv_w, ffn_conv_b, w_ffn_out):
    prm = dict(norm_mix_pre=norm_mix_pre, norm_mix_post=norm_mix_post, w_in=w_in, conv_a_w=conv_a_w,
               w_a_out=w_a_out, ssm_conv_w=ssm_conv_w, ssm_conv_b=ssm_conv_b, ssm_dt_bias=ssm_dt_bias,
               ssm_a_log=ssm_a_log, ssm_d=ssm_d, ssm_norm=ssm_norm, w_ssm_out=w_ssm_out,
               attn_sinks=attn_sinks, w_attn_out=w_attn_out, w_out=w_out, norm_x_pre=norm_x_pre,
               norm_x_post=norm_x_post, norm_mem=norm_mem, w_xq=w_xq, w_xk=w_xk, w_xv=w_xv, w_xo=w_xo,
               norm_ffn_pre=norm_ffn_pre, norm_ffn_post=norm_ffn_post, w_ffn_in=w_ffn_in,
               ffn_conv_w=ffn_conv_w, ffn_conv_b=ffn_conv_b, w_ffn_out=w_ffn_out)
    nb, seq, d = x_prompt.shape
    ns = x_sample.shape[0]
    mem_len = mem_prompt.shape[1]
    depth = w_in.shape[0]
    assert x_sample.shape[1] == 1 and seq % WINDOW == 0 and seq % SSD_CHUNK == 0
    assert cache_swa_k.shape[2] == WINDOW and PAST_LEN >= WINDOW

    yp = x_prompt.reshape(nb * seq, d)
    ys = x_sample.reshape(ns, d)
    mem = mem_prompt.reshape(nb * mem_len, d)
    flat = lambda a, *tail: a.reshape((depth * ns,) + tail)
    kvw = KV_ATT * HD_ATT
    big = (flat(state_ssm, D_SSM, N_SSM),
           flat(cache_swa_k, cache_swa_k.shape[2], kvw), flat(cache_swa_v, cache_swa_v.shape[2], kvw),
           cache_mem_k, cache_mem_v)
    w = _prep_weights(prm)
    new_p, new_s, carry, mem_carry = [], [], None, None
    for l in range(depth):
        p = _prep_layer(l, prm, w, seq)
        yp, st_p, mem_carry = _prompt_layer(yp, mem, p, nb, seq, mem_len, depth, mem_carry)
        new_p.append(st_p)
        ys, st_s, carry = _sample_layer(ys, l, (state_conv_a[l], state_ssm_conv[l], state_ffn_conv[l]),
                                        big, carry, p)
        new_s.append(st_s)
    stack = lambda lst, i: jnp.stack([s[i] for s in lst])
    s_ssm, s_swa_k, s_swa_v = (carry[0].reshape(state_ssm.shape), carry[1].reshape(cache_swa_k.shape),
                               carry[2].reshape(cache_swa_v.shape))
    return ((yp.reshape(nb, seq, d), ys.reshape(ns, 1, d))
            + tuple(stack(new_p, i) for i in range(6)) + tuple(mem_carry)
            + (stack(new_s, 0), stack(new_s, 1), s_ssm, s_swa_k, s_swa_v, stack(new_s, 2)))
```
